```python
import jax, jax.numpy as jnp
from jax import lax
import numpy as np

D_MODEL = 2048
BATCH = 8
SEQ = 2048
DEPTH = 4

CHUNK = 64
GDN_HEAD_DIM = 128
GDN_HEADS = D_MODEL // 256
GDN_WIDTH = GDN_HEADS * GDN_HEAD_DIM
CONV_WIDTH = 4
GMLP_WIDTH = D_MODEL // 2
GMLP_GROUPS = 8
GMLP_GROUP_DIM = GMLP_WIDTH // GMLP_GROUPS
GMLP_BLOCK = 128
SBA_HEAD_DIM = 128
SBA_HEADS = D_MODEL // 256
SBA_WIDTH = SBA_HEADS * SBA_HEAD_DIM
QUERY_BLOCK = 128
N_BRANCHES = 3
D_FF = 4 * D_MODEL
EPS = 1e-6
PROJ_SIZES = (3 * GDN_WIDTH,
              GDN_HEADS,
              GDN_HEADS,
              GDN_WIDTH,
              2 * GMLP_WIDTH,
              3 * SBA_WIDTH,
              N_BRANCHES * D_MODEL)
D_IN = sum(PROJ_SIZES)

kernel_name = "hybrid_gdn_gmlp_stickbreak_trunk"


def rms_norm(x, gain):
    xf = x.astype(jnp.float32)
    y = xf * lax.rsqrt(jnp.mean(xf * xf, axis=-1, keepdims=True) + EPS)
    return (y * gain.astype(jnp.float32)).astype(x.dtype)


def layer_norm(x, gain):
    xf = x.astype(jnp.float32)
    mu = jnp.mean(xf, axis=-1, keepdims=True)
    xc = xf - mu
    y = xc * lax.rsqrt(jnp.mean(xc * xc, axis=-1, keepdims=True) + EPS)
    return (y * gain.astype(jnp.float32)).astype(x.dtype)


def l2_norm(x):
    return x * lax.rsqrt(jnp.sum(x * x, axis=-1, keepdims=True) + EPS)


def causal_depthwise_conv(x, w):
    K, C = w.shape
    return lax.conv_general_dilated(x, w[:, None, :].astype(x.dtype), window_strides=(1,),
                                    padding=[(K - 1, 0)],
                                    dimension_numbers=('NWC', 'WIO', 'NWC'),
                                    feature_group_count=C)


def gated_delta_rule_chunked(q, k, v, g, beta):
    out_dtype = v.dtype
    B, H, T, dk = q.shape
    dv = v.shape[-1]
    N = T // CHUNK
    f32 = jnp.float32
    q = q.astype(f32).reshape(B, H, N, CHUNK, dk)
    k = k.astype(f32).reshape(B, H, N, CHUNK, dk)
    v = v.astype(f32).reshape(B, H, N, CHUNK, dv)
    g = jnp.cumsum(g.astype(f32).reshape(B, H, N, CHUNK), axis=-1)
    beta = beta.astype(f32).reshape(B, H, N, CHUNK)
    incl = jnp.tril(jnp.ones((CHUNK, CHUNK), dtype=bool))
    strict = jnp.tril(jnp.ones((CHUNK, CHUNK), dtype=bool), -1)
    decay = jnp.exp(jnp.where(incl, g[..., :, None] - g[..., None, :], -jnp.inf))
    kb = k * beta[..., None]
    L = jnp.where(strict, jnp.einsum('bhncd,bhnsd->bhncs', kb, k) * decay, 0.0)
    eye = jnp.eye(CHUNK, dtype=f32)
    rhs = jnp.concatenate([v * beta[..., None], kb * jnp.exp(g)[..., None]], axis=-1)
    sol = lax.linalg.triangular_solve(L + eye, rhs, left_side=True, lower=True,
                                      unit_diagonal=True)
    u, w = sol[..., :dv], sol[..., dv:]
    intra = jnp.einsum('bhncd,bhnsd->bhncs', q, k) * decay
    q_dec = q * jnp.exp(g)[..., None]
    g_last = g[..., -1]
    k_dec = k * jnp.exp(g_last[..., None] - g)[..., None]
    xs = tuple(jnp.moveaxis(t, 2, 0) for t in (u, w, intra, q_dec, k_dec, g_last))

    def step(S, inp):
        u_n, w_n, a_n, qd_n, kd_n, gl_n = inp
        v_new = u_n - jnp.einsum('bhck,bhkv->bhcv', w_n, S)
        o_n = jnp.einsum('bhck,bhkv->bhcv', qd_n, S) + jnp.einsum('bhcs,bhsv->bhcv', a_n, v_new)
        S = S * jnp.exp(gl_n)[..., None, None] + jnp.einsum('bhck,bhcv->bhkv', kd_n, v_new)
        return S, o_n

    S0 = jnp.zeros((B, H, dk, dv), f32)
    _, o = lax.scan(step, S0, xs)
    return jnp.moveaxis(o, 0, 2).reshape(B, H, T, dv).astype(out_dtype)


def stick_breaking_attention(q, k, v):
    B, H, T, d = q.shape
    nb = T // QUERY_BLOCK
    qb = jnp.moveaxis(q.reshape(B, H, nb, QUERY_BLOCK, d), 2, 0)
    kpos = jnp.arange(T)
    scale = d ** -0.5

    def block(args):
        q_blk, i = args
        z = jnp.einsum('bhqd,bhkd->bhqk', q_blk, k).astype(jnp.float32) * scale
        qpos = i * QUERY_BLOCK + jnp.arange(QUERY_BLOCK)
        strict = kpos[None, :] < qpos[:, None]
        log_keep = jnp.where(strict, jax.nn.log_sigmoid(-z), 0.0)
        suffix = lax.cumsum(log_keep, axis=3, reverse=True) - log_keep
        A = jnp.where(strict, jnp.exp(jax.nn.log_sigmoid(z) + suffix), 0.0)
        return jnp.einsum('bhqk,bhkd->bhqd', A.astype(v.dtype), v)

    out = lax.map(block, (qb, jnp.arange(nb)))
    return jnp.moveaxis(out, 0, 2).reshape(B, H, T, d)


def hybrid_mixer(h, w_in, conv_w, a_log, dt_bias, gdn_norm_g, gmlp_ln_g, w_spatial, b_spatial,
                 sba_q_g, sba_k_g, w_out_a, w_out_b, w_out_c, w_out):
    B, T, _ = h.shape
    z = h @ w_in
    split_idx = np.cumsum(PROJ_SIZES)[:-1].tolist()
    gdn_qkv, gdn_a, gdn_b, gdn_gate, gmlp_uv, sba_qkv, gate_logits = jnp.split(z, split_idx, axis=-1)

    qkv = jax.nn.silu(causal_depthwise_conv(gdn_qkv, conv_w))
    qa, ka, va = jnp.split(qkv, 3, axis=-1)
    to_heads = lambda t, H, d: jnp.transpose(t.reshape(B, T, H, d), (0, 2, 1, 3))
    qa = l2_norm(to_heads(qa, GDN_HEADS, GDN_HEAD_DIM).astype(jnp.float32)) * GDN_HEAD_DIM ** -0.5
    ka = l2_norm(to_heads(ka, GDN_HEADS, GDN_HEAD_DIM).astype(jnp.float32))
    va = to_heads(va, GDN_HEADS, GDN_HEAD_DIM)
    beta = jnp.transpose(jax.nn.sigmoid(gdn_b.astype(jnp.float32)), (0, 2, 1))
    g = -jnp.exp(a_log.astype(jnp.float32)) * jax.nn.softplus(
        gdn_a.astype(jnp.float32) + dt_bias.astype(jnp.float32))
    g = jnp.transpose(g, (0, 2, 1))
    oa = gated_delta_rule_chunked(qa, ka, va, g, beta)
    oa = jnp.transpose(oa, (0, 2, 1, 3))
    oa = rms_norm(oa, gdn_norm_g) * jax.nn.silu(gdn_gate.reshape(B, T, GDN_HEADS, GDN_HEAD_DIM))
    branch_a = oa.reshape(B, T, GDN_WIDTH) @ w_out_a

    uv = jax.nn.gelu(gmlp_uv, approximate=False)
    u, vb = jnp.split(uv, 2, axis=-1)
    vb = layer_norm(vb, gmlp_ln_g).reshape(B, T // GMLP_BLOCK, GMLP_BLOCK, GMLP_GROUPS, GMLP_GROUP_DIM)
    pos = jnp.arange(GMLP_BLOCK) // CHUNK
    chunk_causal = pos[None, :] <= pos[:, None]
    ws = jnp.where(chunk_causal[None], w_spatial, 0.0).astype(vb.dtype)
    s = jnp.einsum('gts,bnsgc->bntgc', ws, vb) + jnp.transpose(b_spatial)[None, None, :, :, None]
    branch_b = (u * s.reshape(B, T, GMLP_WIDTH)) @ w_out_b

    qc, kc, vc = jnp.split(sba_qkv, 3, axis=-1)
    qc = to_heads(rms_norm(qc.reshape(B, T, SBA_HEADS, SBA_HEAD_DIM), sba_q_g).reshape(B, T, SBA_WIDTH), SBA_HEADS, SBA_HEAD_DIM)
    kc = to_heads(rms_norm(kc.reshape(B, T, SBA_HEADS, SBA_HEAD_DIM), sba_k_g).reshape(B, T, SBA_WIDTH), SBA_HEADS, SBA_HEAD_DIM)
    vc = to_heads(vc, SBA_HEADS, SBA_HEAD_DIM)
    oc = stick_breaking_attention(qc, kc, vc)
    branch_c = jnp.transpose(oc, (0, 2, 1, 3)).reshape(B, T, SBA_WIDTH) @ w_out_c

    gates = jax.nn.sigmoid(gate_logits).reshape(B, T, N_BRANCHES, D_MODEL)
    y = gates[:, :, 0] * branch_a + gates[:, :, 1] * branch_b + gates[:, :, 2] * branch_c
    return y @ w_out


def _fwd_setup_inputs(seed: int = 0) -> dict:
    key = jax.random.key(seed)
    ks = jax.random.split(key, 20)
    nrm = lambda k, shape, scale: jax.random.normal(k, shape, jnp.float32) * scale
    gain = lambda k, shape: 1.0 + 0.02 * jax.random.normal(k, shape, jnp.float32)
    dt = jnp.exp(jax.random.uniform(ks[4], (DEPTH, GDN_HEADS), jnp.float32,
                                    np.log(1e-3).astype(np.float32), np.log(1e-1).astype(np.float32)))
    return {
        "x": nrm(ks[0], (BATCH, SEQ, D_MODEL), 1.0),
        "w_in": nrm(ks[1], (DEPTH, D_MODEL, D_IN), D_MODEL ** -0.5),
        "conv_w": nrm(ks[2], (DEPTH, CONV_WIDTH, 3 * GDN_WIDTH), CONV_WIDTH ** -0.5),
        "a_log": jnp.log(jax.random.uniform(ks[3], (DEPTH, GDN_HEADS), jnp.float32, 1.0, 16.0)),
        "dt_bias": dt + jnp.log(-jnp.expm1(-dt)),
        "gdn_norm_g": gain(ks[5], (DEPTH, GDN_HEAD_DIM)),
        "gmlp_ln_g": gain(ks[6], (DEPTH, GMLP_WIDTH)),
        "w_spatial": nrm(ks[7], (DEPTH, GMLP_GROUPS, GMLP_BLOCK, GMLP_BLOCK), GMLP_BLOCK ** -0.5),
        "b_spatial": gain(ks[8], (DEPTH, GMLP_GROUPS, GMLP_BLOCK)),
        "sba_q_g": gain(ks[9], (DEPTH, SBA_HEAD_DIM)),
        "sba_k_g": gain(ks[10], (DEPTH, SBA_HEAD_DIM)),
        "w_out_a": nrm(ks[11], (DEPTH, GDN_WIDTH, D_MODEL), GDN_WIDTH ** -0.5),
        "w_out_b": nrm(ks[12], (DEPTH, GMLP_WIDTH, D_MODEL), GMLP_WIDTH ** -0.5),
        "w_out_c": nrm(ks[13], (DEPTH, SBA_WIDTH, D_MODEL), SBA_WIDTH ** -0.5),
        "w_out": nrm(ks[14], (DEPTH, D_MODEL, D_MODEL), D_MODEL ** -0.5),
        "norm_mix_g": gain(ks[15], (DEPTH, D_MODEL)),
        "norm_mlp_g": gain(ks[16], (DEPTH, D_MODEL)),
        "w_ff1": nrm(ks[17], (DEPTH, D_MODEL, D_FF), D_MODEL ** -0.5),
        "w_ff2": nrm(ks[18], (DEPTH, D_FF, D_MODEL), D_FF ** -0.5),
    }


def _fwd_reference(x, w_in, conv_w, a_log, dt_bias, gdn_norm_g, gmlp_ln_g, w_spatial, b_spatial,
              sba_q_g, sba_k_g, w_out_a, w_out_b, w_out_c, w_out, norm_mix_g, norm_mlp_g,
              w_ff1, w_ff2):
    for l in range(DEPTH):
        h = rms_norm(x, norm_mix_g[l])
        x = x + hybrid_mixer(h, w_in[l], conv_w[l], a_log[l], dt_bias[l], gdn_norm_g[l],
                             gmlp_ln_g[l], w_spatial[l], b_spatial[l], sba_q_g[l], sba_k_g[l],
                             w_out_a[l], w_out_b[l], w_out_c[l], w_out[l])
        h = rms_norm(x, norm_mlp_g[l])
        x = x + jnp.square(jax.nn.relu(h @ w_ff1[l])) @ w_ff2[l]
    return x


import jax as _jax
import jax.numpy as _jnp

TWIN_FORMAT = 'train_step'
FWD_PARAMS = ['x', 'w_in', 'conv_w', 'a_log', 'dt_bias', 'gdn_norm_g', 'gmlp_ln_g', 'w_spatial', 'b_spatial', 'sba_q_g', 'sba_k_g', 'w_out_a', 'w_out_b', 'w_out_c', 'w_out', 'norm_mix_g', 'norm_mlp_g', 'w_ff1', 'w_ff2']
TWIN_WEIGHTS = ['w_in', 'conv_w', 'a_log', 'dt_bias', 'gdn_norm_g', 'gmlp_ln_g', 'w_spatial', 'b_spatial', 'sba_q_g', 'sba_k_g', 'w_out_a', 'w_out_b', 'w_out_c', 'w_out', 'norm_mix_g', 'norm_mlp_g', 'w_ff1', 'w_ff2']
TWIN_DIFF_INPUT = 'x'
TWIN_INPUTS = ['x', 'w_in', 'conv_w', 'a_log', 'dt_bias', 'gdn_norm_g', 'gmlp_ln_g', 'w_spatial', 'b_spatial', 'sba_q_g', 'sba_k_g', 'w_out_a', 'w_out_b', 'w_out_c', 'w_out', 'norm_mix_g', 'norm_mlp_g', 'w_ff1', 'w_ff2', 'loss_target', 'm_w_in', 'm_conv_w', 'm_a_log', 'm_dt_bias', 'm_gdn_norm_g', 'm_gmlp_ln_g', 'm_w_spatial', 'm_b_spatial', 'm_sba_q_g', 'm_sba_k_g', 'm_w_out_a', 'm_w_out_b', 'm_w_out_c', 'm_w_out', 'm_norm_mix_g', 'm_norm_mlp_g', 'm_w_ff1', 'm_w_ff2', 'v_w_in', 'v_conv_w', 'v_a_log', 'v_dt_bias', 'v_gdn_norm_g', 'v_gmlp_ln_g', 'v_w_spatial', 'v_b_spatial', 'v_sba_q_g', 'v_sba_k_g', 'v_w_out_a', 'v_w_out_b', 'v_w_out_c', 'v_w_out', 'v_norm_mix_g', 'v_norm_mlp_g', 'v_w_ff1', 'v_w_ff2']
TWIN_OUTPUTS = ['loss', 'grad_x', 'grad_w_in', 'grad_conv_w', 'grad_a_log', 'grad_dt_bias', 'grad_gdn_norm_g', 'grad_gmlp_ln_g', 'grad_w_spatial', 'grad_b_spatial', 'grad_sba_q_g', 'grad_sba_k_g', 'grad_w_out_a', 'grad_w_out_b', 'grad_w_out_c', 'grad_w_out', 'grad_norm_mix_g', 'grad_norm_mlp_g', 'grad_w_ff1', 'grad_w_ff2', 'delta_w_in', 'delta_conv_w', 'delta_a_log', 'delta_dt_bias', 'delta_gdn_norm_g', 'delta_gmlp_ln_g', 'delta_w_spatial', 'delta_b_spatial', 'delta_sba_q_g', 'delta_sba_k_g', 'delta_w_out_a', 'delta_w_out_b', 'delta_w_out_c', 'delta_w_out', 'delta_norm_mix_g', 'delta_norm_mlp_g', 'delta_w_ff1', 'delta_w_ff2', 'new_m_w_in', 'new_m_conv_w', 'new_m_a_log', 'new_m_dt_bias', 'new_m_gdn_norm_g', 'new_m_gmlp_ln_g', 'new_m_w_spatial', 'new_m_b_spatial', 'new_m_sba_q_g', 'new_m_sba_k_g', 'new_m_w_out_a', 'new_m_w_out_b', 'new_m_w_out_c', 'new_m_w_out', 'new_m_norm_mix_g', 'new_m_norm_mlp_g', 'new_m_w_ff1', 'new_m_w_ff2', 'new_v_w_in', 'new_v_conv_w', 'new_v_a_log', 'new_v_dt_bias', 'new_v_gdn_norm_g', 'new_v_gmlp_ln_g', 'new_v_w_spatial', 'new_v_b_spatial', 'new_v_sba_q_g', 'new_v_sba_k_g', 'new_v_w_out_a', 'new_v_w_out_b', 'new_v_w_out_c', 'new_v_w_out', 'new_v_norm_mix_g', 'new_v_norm_mlp_g', 'new_v_w_ff1', 'new_v_w_ff2']
TWIN_LEAF_KINDS = {'loss': 'loss', 'grad_x': 'grad_x', 'grad_w_in': 'grad_w', 'grad_conv_w': 'grad_w', 'grad_a_log': 'grad_w', 'grad_dt_bias': 'grad_w', 'grad_gdn_norm_g': 'grad_w', 'grad_gmlp_ln_g': 'grad_w', 'grad_w_spatial': 'grad_w', 'grad_b_spatial': 'grad_w', 'grad_sba_q_g': 'grad_w', 'grad_sba_k_g': 'grad_w', 'grad_w_out_a': 'grad_w', 'grad_w_out_b': 'grad_w', 'grad_w_out_c': 'grad_w', 'grad_w_out': 'grad_w', 'grad_norm_mix_g': 'grad_w', 'grad_norm_mlp_g': 'grad_w', 'grad_w_ff1': 'grad_w', 'grad_w_ff2': 'grad_w', 'delta_w_in': 'delta_w', 'delta_conv_w': 'delta_w', 'delta_a_log': 'delta_w', 'delta_dt_bias': 'delta_w', 'delta_gdn_norm_g': 'delta_w', 'delta_gmlp_ln_g': 'delta_w', 'delta_w_spatial': 'delta_w', 'delta_b_spatial': 'delta_w', 'delta_sba_q_g': 'delta_w', 'delta_sba_k_g': 'delta_w', 'delta_w_out_a': 'delta_w', 'delta_w_out_b': 'delta_w', 'delta_w_out_c': 'delta_w', 'delta_w_out': 'delta_w', 'delta_norm_mix_g': 'delta_w', 'delta_norm_mlp_g': 'delta_w', 'delta_w_ff1': 'delta_w', 'delta_w_ff2': 'delta_w', 'new_m_w_in': 'new_m', 'new_m_conv_w': 'new_m', 'new_m_a_log': 'new_m', 'new_m_dt_bias': 'new_m', 'new_m_gdn_norm_g': 'new_m', 'new_m_gmlp_ln_g': 'new_m', 'new_m_w_spatial': 'new_m', 'new_m_b_spatial': 'new_m', 'new_m_sba_q_g': 'new_m', 'new_m_sba_k_g': 'new_m', 'new_m_w_out_a': 'new_m', 'new_m_w_out_b': 'new_m', 'new_m_w_out_c': 'new_m', 'new_m_w_out': 'new_m', 'new_m_norm_mix_g': 'new_m', 'new_m_norm_mlp_g': 'new_m', 'new_m_w_ff1': 'new_m', 'new_m_w_ff2': 'new_m', 'new_v_w_in': 'new_v', 'new_v_conv_w': 'new_v', 'new_v_a_log': 'new_v', 'new_v_dt_bias': 'new_v', 'new_v_gdn_norm_g': 'new_v', 'new_v_gmlp_ln_g': 'new_v', 'new_v_w_spatial': 'new_v', 'new_v_b_spatial': 'new_v', 'new_v_sba_q_g': 'new_v', 'new_v_sba_k_g': 'new_v', 'new_v_w_out_a': 'new_v', 'new_v_w_out_b': 'new_v', 'new_v_w_out_c': 'new_v', 'new_v_w_out': 'new_v', 'new_v_norm_mix_g': 'new_v', 'new_v_norm_mlp_g': 'new_v', 'new_v_w_ff1': 'new_v', 'new_v_w_ff2': 'new_v'}


def _forward(args):
    return _fwd_reference(*[args[k] for k in FWD_PARAMS])


def _output_shape():
    out = _jax.eval_shape(lambda: _forward(_fwd_setup_inputs(0)))
    return out.shape, out.dtype

N_MICROBATCH = 1
ADAM_LR = 0.001
ADAM_B1 = 0.9
ADAM_B2 = 0.999
ADAM_EPS = 1e-08
ADAM_WD = 0.01
ADAM_STEP = 10
PER_EXAMPLE_BATCH_AXIS = {'x': 0, 'loss_target': 0}
SHARED_INPUTS = []
_WEIGHT_DTYPES = {'w_in': _jnp.float32, 'conv_w': _jnp.float32, 'a_log': _jnp.float32, 'dt_bias': _jnp.float32, 'gdn_norm_g': _jnp.float32, 'gmlp_ln_g': _jnp.float32, 'w_spatial': _jnp.float32, 'b_spatial': _jnp.float32, 'sba_q_g': _jnp.float32, 'sba_k_g': _jnp.float32, 'w_out_a': _jnp.float32, 'w_out_b': _jnp.float32, 'w_out_c': _jnp.float32, 'w_out': _jnp.float32, 'norm_mix_g': _jnp.float32, 'norm_mlp_g': _jnp.float32, 'w_ff1': _jnp.float32, 'w_ff2': _jnp.float32}
MOMENT_SCALE = {'w_in': 9.081212e-01, 'conv_w': 8.894301e-01, 'a_log': 5.755508e+00, 'dt_bias': 5.511051e+00, 'gdn_norm_g': 1.605222e+01, 'gmlp_ln_g': 1.731223e+00, 'w_spatial': 2.602991e-01, 'b_spatial': 1.841488e+00, 'sba_q_g': 2.189299e+00, 'sba_k_g': 2.193193e+00, 'w_out_a': 1.173672e+00, 'w_out_b': 2.291562e+00, 'w_out_c': 1.865559e+00, 'w_out': 3.176534e+00, 'norm_mix_g': 5.029964e+00, 'norm_mlp_g': 2.447504e+01, 'w_ff1': 1.912708e+00, 'w_ff2': 6.986904e+00}


def _to_microbatches(a, axis):
    t = _jnp.moveaxis(a, axis, 0)
    t = t.reshape((N_MICROBATCH, t.shape[0] // N_MICROBATCH) + t.shape[1:])
    return _jnp.moveaxis(t, 1, axis + 1)


def setup_inputs(seed: int = 0) -> dict:
    inp = _fwd_setup_inputs(seed)
    key = _jax.random.fold_in(_jax.random.key(seed), 7919)
    shape, _ = _output_shape()
    out = dict(inp)
    out["loss_target"] = _jax.random.normal(_jax.random.fold_in(key, 0), shape, _jnp.float32)
    for i, name in enumerate(TWIN_WEIGHTS):
        w = inp[name].astype(_jnp.float32)
        if MOMENT_SCALE is None:
            s = _jnp.sqrt(_jnp.mean(_jnp.square(w)) + 1e-30)
        else:
            s = MOMENT_SCALE[name]
        km, kv = _jax.random.split(_jax.random.fold_in(key, i + 1))
        out[name] = w
        out["m_" + name] = s * _jax.random.normal(km, w.shape, _jnp.float32)
        out["v_" + name] = (s * s) * _jax.random.uniform(kv, w.shape, _jnp.float32, 0.5, 1.5)
    if N_MICROBATCH > 1:
        for name, axis in PER_EXAMPLE_BATCH_AXIS.items():
            out[name] = _to_microbatches(out[name], axis)
    return {'x': out['x'], 'w_in': out['w_in'], 'conv_w': out['conv_w'], 'a_log': out['a_log'], 'dt_bias': out['dt_bias'], 'gdn_norm_g': out['gdn_norm_g'], 'gmlp_ln_g': out['gmlp_ln_g'], 'w_spatial': out['w_spatial'], 'b_spatial': out['b_spatial'], 'sba_q_g': out['sba_q_g'], 'sba_k_g': out['sba_k_g'], 'w_out_a': out['w_out_a'], 'w_out_b': out['w_out_b'], 'w_out_c': out['w_out_c'], 'w_out': out['w_out'], 'norm_mix_g': out['norm_mix_g'], 'norm_mlp_g': out['norm_mlp_g'], 'w_ff1': out['w_ff1'], 'w_ff2': out['w_ff2'], 'loss_target': out['loss_target'], 'm_w_in': out['m_w_in'], 'm_conv_w': out['m_conv_w'], 'm_a_log': out['m_a_log'], 'm_dt_bias': out['m_dt_bias'], 'm_gdn_norm_g': out['m_gdn_norm_g'], 'm_gmlp_ln_g': out['m_gmlp_ln_g'], 'm_w_spatial': out['m_w_spatial'], 'm_b_spatial': out['m_b_spatial'], 'm_sba_q_g': out['m_sba_q_g'], 'm_sba_k_g': out['m_sba_k_g'], 'm_w_out_a': out['m_w_out_a'], 'm_w_out_b': out['m_w_out_b'], 'm_w_out_c': out['m_w_out_c'], 'm_w_out': out['m_w_out'], 'm_norm_mix_g': out['m_norm_mix_g'], 'm_norm_mlp_g': out['m_norm_mlp_g'], 'm_w_ff1': out['m_w_ff1'], 'm_w_ff2': out['m_w_ff2'], 'v_w_in': out['v_w_in'], 'v_conv_w': out['v_conv_w'], 'v_a_log': out['v_a_log'], 'v_dt_bias': out['v_dt_bias'], 'v_gdn_norm_g': out['v_gdn_norm_g'], 'v_gmlp_ln_g': out['v_gmlp_ln_g'], 'v_w_spatial': out['v_w_spatial'], 'v_b_spatial': out['v_b_spatial'], 'v_sba_q_g': out['v_sba_q_g'], 'v_sba_k_g': out['v_sba_k_g'], 'v_w_out_a': out['v_w_out_a'], 'v_w_out_b': out['v_w_out_b'], 'v_w_out_c': out['v_w_out_c'], 'v_w_out': out['v_w_out'], 'v_norm_mix_g': out['v_norm_mix_g'], 'v_norm_mlp_g': out['v_norm_mlp_g'], 'v_w_ff1': out['v_w_ff1'], 'v_w_ff2': out['v_w_ff2']}


def _loss(weights, diff, rest, loss_target):
    with _jax.named_scope("forward"):
        args = {**rest, TWIN_DIFF_INPUT: diff, **{k: w.astype(_WEIGHT_DTYPES[k]) for k, w in weights.items()}}
        y = _forward(args)
    with _jax.named_scope("loss_head"):
        err = _jnp.square(y.astype(_jnp.float32) - loss_target)
        return 0.5 * _jnp.sum(_jnp.mean(err, axis=-1)) if err.ndim else 0.5 * err


def _adamw(w, g, m, v):
    m = ADAM_B1 * m + (1.0 - ADAM_B1) * g
    v = ADAM_B2 * v + (1.0 - ADAM_B2) * _jnp.square(g)
    m_hat = m / (1.0 - ADAM_B1 ** ADAM_STEP)
    v_hat = v / (1.0 - ADAM_B2 ** ADAM_STEP)
    delta = -ADAM_LR * (m_hat / (_jnp.sqrt(v_hat) + ADAM_EPS) + ADAM_WD * w)
    return delta, m, v


def reference(x, w_in, conv_w, a_log, dt_bias, gdn_norm_g, gmlp_ln_g, w_spatial, b_spatial, sba_q_g, sba_k_g, w_out_a, w_out_b, w_out_c, w_out, norm_mix_g, norm_mlp_g, w_ff1, w_ff2, loss_target, m_w_in, m_conv_w, m_a_log, m_dt_bias, m_gdn_norm_g, m_gmlp_ln_g, m_w_spatial, m_b_spatial, m_sba_q_g, m_sba_k_g, m_w_out_a, m_w_out_b, m_w_out_c, m_w_out, m_norm_mix_g, m_norm_mlp_g, m_w_ff1, m_w_ff2, v_w_in, v_conv_w, v_a_log, v_dt_bias, v_gdn_norm_g, v_gmlp_ln_g, v_w_spatial, v_b_spatial, v_sba_q_g, v_sba_k_g, v_w_out_a, v_w_out_b, v_w_out_c, v_w_out, v_norm_mix_g, v_norm_mlp_g, v_w_ff1, v_w_ff2):
    given = dict(x=x, w_in=w_in, conv_w=conv_w, a_log=a_log, dt_bias=dt_bias, gdn_norm_g=gdn_norm_g, gmlp_ln_g=gmlp_ln_g, w_spatial=w_spatial, b_spatial=b_spatial, sba_q_g=sba_q_g, sba_k_g=sba_k_g, w_out_a=w_out_a, w_out_b=w_out_b, w_out_c=w_out_c, w_out=w_out, norm_mix_g=norm_mix_g, norm_mlp_g=norm_mlp_g, w_ff1=w_ff1, w_ff2=w_ff2, loss_target=loss_target, m_w_in=m_w_in, m_conv_w=m_conv_w, m_a_log=m_a_log, m_dt_bias=m_dt_bias, m_gdn_norm_g=m_gdn_norm_g, m_gmlp_ln_g=m_gmlp_ln_g, m_w_spatial=m_w_spatial, m_b_spatial=m_b_spatial, m_sba_q_g=m_sba_q_g, m_sba_k_g=m_sba_k_g, m_w_out_a=m_w_out_a, m_w_out_b=m_w_out_b, m_w_out_c=m_w_out_c, m_w_out=m_w_out, m_norm_mix_g=m_norm_mix_g, m_norm_mlp_g=m_norm_mlp_g, m_w_ff1=m_w_ff1, m_w_ff2=m_w_ff2, v_w_in=v_w_in, v_conv_w=v_conv_w, v_a_log=v_a_log, v_dt_bias=v_dt_bias, v_gdn_norm_g=v_gdn_norm_g, v_gmlp_ln_g=v_gmlp_ln_g, v_w_spatial=v_w_spatial, v_b_spatial=v_b_spatial, v_sba_q_g=v_sba_q_g, v_sba_k_g=v_sba_k_g, v_w_out_a=v_w_out_a, v_w_out_b=v_w_out_b, v_w_out_c=v_w_out_c, v_w_out=v_w_out, v_norm_mix_g=v_norm_mix_g, v_norm_mlp_g=v_norm_mlp_g, v_w_ff1=v_w_ff1, v_w_ff2=v_w_ff2)
    weights = {n: given[n] for n in TWIN_WEIGHTS}
    shared = {n: given[n] for n in SHARED_INPUTS}
    per_example = {n: given[n] for n in ['x']}
    grad_fn = _jax.value_and_grad(_loss, argnums=(0, 1))

    def one_microbatch(ex, loss_target):
        ex = dict(ex)
        diff = ex.pop(TWIN_DIFF_INPUT)
        return grad_fn(weights, diff, {**shared, **ex}, loss_target)

    if N_MICROBATCH == 1:
        loss, (grad_w, grad_x) = one_microbatch(per_example, given["loss_target"])
    else:
        def body(carry, xs):
            loss_sum, grad_sum = carry
            l_k, (gw_k, gx_k) = one_microbatch(xs[0], xs[1])
            with _jax.named_scope("update"):
                return (loss_sum + l_k, _jax.tree.map(_jnp.add, grad_sum, gw_k)), gx_k

        init = (_jnp.zeros((), _jnp.float32), _jax.tree.map(_jnp.zeros_like, weights))
        (loss, grad_w), grad_x = _jax.lax.scan(body, init, (per_example, given["loss_target"]))
    with _jax.named_scope("update"):
        delta_w, new_m, new_v = {}, {}, {}
        for n in TWIN_WEIGHTS:
            delta_w[n], new_m[n], new_v[n] = _adamw(weights[n], grad_w[n], given["m_" + n], given["v_" + n])
    return (loss, grad_x, *[grad_w[n] for n in TWIN_WEIGHTS], *[delta_w[n] for n in TWIN_WEIGHTS],
            *[new_m[n] for n in TWIN_WEIGHTS], *[new_v[n] for n in TWIN_WEIGHTS])
```

```python
import functools

import jax
import jax.numpy as jnp
from jax import lax
from jax.experimental import pallas as pl
from jax.experimental.pallas import tpu as pltpu

F32 = jnp.float32
MXU = jnp.bfloat16
HI = lax.Precision.HIGHEST
N_DEV = 8
AXES = ("x", "y", "c")
CHUNK = 64
HD = 128
CONV_K = 4
EPS = 1e-6
LANE = 128
VMEM_LIMIT = 56 * 1024 * 1024
ADAM_LR, ADAM_B1, ADAM_B2, ADAM_EPS, ADAM_WD, ADAM_STEP = 0.001, 0.9, 0.999, 1e-08, 0.01, 10

_ANY = pl.BlockSpec(memory_space=pl.ANY)
_MESH = pl.DeviceIdType.MESH
_DN = {"nn": (((1,), (0,)), ((), ())), "nt": (((1,), (1,)), ((), ())), "tn": (((0,), (0,)), ((), ()))}


def _cp(*sem):
    return pltpu.CompilerParams(dimension_semantics=sem, vmem_limit_bytes=VMEM_LIMIT)


def _tile(n, cap):
    if n <= cap:
        return n
    best = LANE
    for t in range(LANE, cap + 1, LANE):
        if n % t == 0:
            best = t
    assert n % best == 0, (n, cap)
    return best


def _mm_raw(a, b, mode, hi):
    if hi:
        return lax.dot_general(a.astype(F32), b.astype(F32), _DN[mode], precision=HI, preferred_element_type=F32)
    return lax.dot_general(a.astype(MXU), b.astype(MXU), _DN[mode], preferred_element_type=F32)


@functools.partial(jax.custom_vjp, nondiff_argnums=(2, 3))
def mm(a, b, mode, hi):
    return _mm_raw(a, b, mode, hi)


def _mm_fwd(a, b, mode, hi):
    return _mm_raw(a, b, mode, hi), (a, b)


def _mm_bwd(mode, hi, res, ct):
    a, b = res
    if mode == "nn":
        da, db = _mm_raw(ct, b, "nt", hi), _mm_raw(a, ct, "tn", hi)
    elif mode == "nt":
        da, db = _mm_raw(ct, b, "nn", hi), _mm_raw(ct, a, "tn", hi)
    else:
        da, db = _mm_raw(b, ct, "nt", hi), _mm_raw(a, ct, "nn", hi)
    return da.astype(a.dtype), db.astype(b.dtype)


mm.defvjp(_mm_fwd, _mm_bwd)


def _shift_rows(x, j):
    n = x.shape[0]
    row = lax.broadcasted_iota(jnp.int32, x.shape, 0)
    if j > 0:
        return jnp.where(row >= j, pltpu.roll(x, j, 0), 0.0)
    return jnp.where(row < n + j, pltpu.roll(x, n + j, 0), 0.0)


@functools.partial(jax.custom_vjp, nondiff_argnums=(1,))
def shift(x, j):
    return _shift_rows(x, j)


shift.defvjp(lambda x, j: (_shift_rows(x, j), None), lambda j, _, ct: (_shift_rows(ct, -j),))


def _sigmoid(x):
    return 1.0 / (1.0 + jnp.exp(-x))


def _silu(x):
    return x * _sigmoid(x)


def _softplus(x):
    return jnp.maximum(x, 0.0) + jnp.log(1.0 + jnp.exp(-jnp.abs(x)))


def _logsig(x):
    return jnp.minimum(x, 0.0) - jnp.log(1.0 + jnp.exp(-jnp.abs(x)))


def _gelu(x):
    return 0.5 * x * (1.0 + lax.erf(x * (2.0 ** -0.5)))


def _rms(x, g):
    return x * lax.rsqrt(jnp.mean(x * x, axis=-1, keepdims=True) + EPS) * g


def _iota2(shape, dim):
    return lax.broadcasted_iota(jnp.int32, shape, dim)


def all_gather(x, name):
    def body(x_ref, out_ref, send_sems, recv_sems, local_sem):
        ix, iy, ic = lax.axis_index("x"), lax.axis_index("y"), lax.axis_index("c")
        me, sibling = (ix, iy, ic), (ix, iy, 1 - ic)
        chips = [(1 - ix, iy), (ix, 1 - iy), (1 - ix, 1 - iy)]

        def slot(px, py, pc):
            return out_ref.at[4 * px + 2 * py + pc]

        def copy(k, block, to, src=None):
            return pltpu.make_async_remote_copy(
                src_ref=slot(*block) if src is None else src, dst_ref=slot(*block),
                send_sem=send_sems.at[k], recv_sem=recv_sems.at[k], device_id=to, device_id_type=_MESH)

        mine = pltpu.make_async_copy(x_ref, slot(*me), local_sem)
        mine.start()
        first = [copy(0, me, sibling, src=x_ref)]
        first += [copy(1 + j, me, (*chip, ic), src=x_ref) for j, chip in enumerate(chips)]
        for cp in first:
            cp.start()
        passed = [copy(4 + j, (*chip, ic), sibling) for j, chip in enumerate(chips)]
        for j, chip in enumerate(chips):
            copy(1 + j, (*chip, ic), me).wait_recv()
            passed[j].start()
        copy(0, sibling, me).wait_recv()
        for j, chip in enumerate(chips):
            copy(4 + j, (*chip, 1 - ic), me).wait_recv()
        for cp in first + passed:
            cp.wait_send()
        mine.wait()

    return pl.pallas_call(
        body, name=name, out_shape=jax.ShapeDtypeStruct((N_DEV,) + x.shape, x.dtype),
        in_specs=[_ANY], out_specs=_ANY,
        scratch_shapes=[pltpu.SemaphoreType.DMA((7,)), pltpu.SemaphoreType.DMA((7,)), pltpu.SemaphoreType.DMA(())],
    )(x)


def rs_pair_exchange(g8, name):
    def body(g_ref, r_ref, send_sems, recv_sems):
        ix, iy, ic = lax.axis_index("x"), lax.axis_index("y"), lax.axis_index("c")
        copies = [
            pltpu.make_async_remote_copy(
                src_ref=g_ref.at[2 * xy + (1 - ic)], dst_ref=r_ref.at[xy],
                send_sem=send_sems.at[xy], recv_sem=recv_sems.at[xy],
                device_id=(ix, iy, 1 - ic), device_id_type=_MESH)
            for xy in range(4)
        ]
        for cp in copies:
            cp.start()
        for cp in copies:
            cp.wait()

    return pl.pallas_call(
        body, name=name, out_shape=jax.ShapeDtypeStruct((4,) + g8.shape[1:], g8.dtype),
        in_specs=[_ANY], out_specs=_ANY,
        scratch_shapes=[pltpu.SemaphoreType.DMA((4,)), pltpu.SemaphoreType.DMA((4,))],
    )(g8)


def rs_chip_exchange(p4, name):
    def body(p_ref, r_ref, send_sems, recv_sems, local_sem):
        ix, iy, ic = lax.axis_index("x"), lax.axis_index("y"), lax.axis_index("c")
        my_xy = 2 * ix + iy
        local = pltpu.make_async_copy(p_ref.at[my_xy], r_ref.at[my_xy], local_sem)
        local.start()
        chips = [(1 - ix, iy), (ix, 1 - iy), (1 - ix, 1 - iy)]
        copies = [
            pltpu.make_async_remote_copy(
                src_ref=p_ref.at[2 * px + py], dst_ref=r_ref.at[my_xy],
                send_sem=send_sems.at[k], recv_sem=recv_sems.at[k],
                device_id=(px, py, ic), device_id_type=_MESH)
            for k, (px, py) in enumerate(chips)
        ]
        for cp in copies:
            cp.start()
        for cp in copies:
            cp.wait()
        local.wait()

    return pl.pallas_call(
        body, name=name, out_shape=jax.ShapeDtypeStruct(p4.shape, p4.dtype),
        in_specs=[_ANY], out_specs=_ANY,
        scratch_shapes=[pltpu.SemaphoreType.DMA((3,)), pltpu.SemaphoreType.DMA((3,)), pltpu.SemaphoreType.DMA(())],
    )(p4)


def pair_sum(g8, r4, c_idx, name):
    _, rows, cols = g8.shape
    tr = _tile_rows(rows, cols)

    def body(c_ref, g_ref, r_ref, o_ref):
        o_ref[...] = (g_ref[...].astype(F32) + r_ref[...].astype(F32)).astype(o_ref.dtype)

    grid_spec = pltpu.PrefetchScalarGridSpec(
        num_scalar_prefetch=1, grid=(4, rows // tr),
        in_specs=[pl.BlockSpec((None, tr, cols), lambda s, i, c: (2 * s + c[0], i, 0)),
                  pl.BlockSpec((None, tr, cols), lambda s, i, c: (s, i, 0))],
        out_specs=pl.BlockSpec((None, tr, cols), lambda s, i, c: (s, i, 0)))
    return pl.pallas_call(
        body, name=name, grid_spec=grid_spec, out_shape=jax.ShapeDtypeStruct((4, rows, cols), g8.dtype),
        compiler_params=_cp("parallel", "parallel"),
    )(c_idx, g8, r4)


def _tile_rows(rows, cols):
    tr = rows
    while tr * cols > 128 * 2048 and tr % 32 == 0:
        tr //= 2
    return tr


def _adam_math(w, g, m, v):
    m2 = ADAM_B1 * m + (1.0 - ADAM_B1) * g
    v2 = ADAM_B2 * v + (1.0 - ADAM_B2) * (g * g)
    m_hat = m2 / (1.0 - ADAM_B1 ** ADAM_STEP)
    v_hat = v2 / (1.0 - ADAM_B2 ** ADAM_STEP)
    delta = -ADAM_LR * (m_hat / (jnp.sqrt(v_hat) + ADAM_EPS) + ADAM_WD * w)
    return delta, m2, v2


def adam_layer(parts, w, m, v, bufs, layer, name):
    n_parts, rows, cols = parts.shape
    tr = _tile_rows(rows, cols)

    def body(p_ref, w_ref, m_ref, v_ref, g_in, d_in, m_in, v_in, g_out, d_out, m_out, v_out):
        g = p_ref[0].astype(F32)
        for k in range(1, n_parts):
            g = g + p_ref[k].astype(F32)
        delta, m2, v2 = _adam_math(w_ref[...], g, m_ref[...], v_ref[...])
        g_out[...] = g
        d_out[...] = delta
        m_out[...] = m2
        v_out[...] = v2

    lay = pl.BlockSpec((None, tr, cols), lambda i: (layer, i, 0))
    return pl.pallas_call(
        body, name=name, grid=(rows // tr,),
        in_specs=[pl.BlockSpec((n_parts, tr, cols), lambda i: (0, i, 0)), lay, lay, lay, _ANY, _ANY, _ANY, _ANY],
        out_specs=[lay, lay, lay, lay],
        out_shape=[jax.ShapeDtypeStruct(w.shape, F32)] * 4,
        input_output_aliases={4: 0, 5: 1, 6: 2, 7: 3},
        compiler_params=_cp("parallel"),
    )(parts, w, m, v, *bufs)


def matmul(a, b, mode, out_dtype, name, res=None, caps=(1024, 1024, 512)):
    if mode == "tn":
        k_dim, m_dim = a.shape
    else:
        m_dim, k_dim = a.shape
    n_dim = b.shape[0] if mode == "nt" else b.shape[1]
    tm, tn, tk = _tile(m_dim, caps[0]), _tile(n_dim, caps[1]), _tile(k_dim, caps[2])
    nk = k_dim // tk

    def body(*refs):
        if res is None:
            a_ref, b_ref, o_ref, acc = refs
        else:
            a_ref, b_ref, r_ref, o_ref, acc = refs
        k = pl.program_id(2)

        @pl.when(k == 0)
        def _():
            acc[...] = jnp.zeros_like(acc)

        acc[...] += lax.dot_general(a_ref[...], b_ref[...], _DN[mode], preferred_element_type=F32)

        @pl.when(k == nk - 1)
        def _():
            r = acc[...]
            if res is not None:
                r = r + r_ref[...]
            o_ref[...] = r.astype(out_dtype)

    a_spec = pl.BlockSpec((tk, tm), lambda i, j, k: (k, i)) if mode == "tn" else pl.BlockSpec((tm, tk), lambda i, j, k: (i, k))
    b_spec = pl.BlockSpec((tn, tk), lambda i, j, k: (j, k)) if mode == "nt" else pl.BlockSpec((tk, tn), lambda i, j, k: (k, j))
    o_spec = pl.BlockSpec((tm, tn), lambda i, j, k: (i, j))
    in_specs, args = [a_spec, b_spec], [a, b]
    if res is not None:
        in_specs.append(o_spec)
        args.append(res)
    return pl.pallas_call(
        body, name=name, grid=(m_dim // tm, n_dim // tn, nk), in_specs=in_specs, out_specs=o_spec,
        out_shape=jax.ShapeDtypeStruct((m_dim, n_dim), out_dtype),
        scratch_shapes=[pltpu.VMEM((tm, tn), F32)],
        compiler_params=_cp("parallel", "parallel", "arbitrary"),
    )(*args)


def rms_fwd(x, gain, name):
    t, d = x.shape
    tt = _tile(t, 256)

    def body(x_ref, g_ref, o_ref):
        o_ref[...] = _rms(x_ref[...], g_ref[...]).astype(o_ref.dtype)

    return pl.pallas_call(
        body, name=name, grid=(t // tt,),
        in_specs=[pl.BlockSpec((tt, d), lambda i: (i, 0)), pl.BlockSpec((1, d), lambda i: (0, 0))],
        out_specs=pl.BlockSpec((tt, d), lambda i: (i, 0)),
        out_shape=jax.ShapeDtypeStruct((t, d), MXU), compiler_params=_cp("parallel"),
    )(x, gain)


def rms_bwd(x, gain, dh, dres, name):
    t, d = x.shape
    tt = _tile(t, 256)

    def body(x_ref, g_ref, dh_ref, dr_ref, dx_ref, dxb_ref, dg_ref):
        _, vjp = jax.vjp(_rms, x_ref[...], g_ref[...])
        dx, dg = vjp(dh_ref[...])
        dx = dx + dr_ref[...]
        dx_ref[...] = dx
        dxb_ref[...] = dx.astype(dxb_ref.dtype)

        @pl.when(pl.program_id(0) == 0)
        def _():
            dg_ref[...] = jnp.zeros_like(dg_ref)

        dg_ref[...] += dg

    row = pl.BlockSpec((tt, d), lambda i: (i, 0))
    vec = pl.BlockSpec((1, d), lambda i: (0, 0))
    return pl.pallas_call(
        body, name=name, grid=(t // tt,), in_specs=[row, vec, row, row], out_specs=[row, row, vec],
        out_shape=[jax.ShapeDtypeStruct((t, d), F32), jax.ShapeDtypeStruct((t, d), MXU), jax.ShapeDtypeStruct((1, d), F32)],
        compiler_params=_cp("arbitrary"),
    )(x, gain, dh, dres)


def sqrelu_fwd(f, name):
    t, n = f.shape
    tt, tn = _tile(t, 256), _tile(n, 2048)

    def body(f_ref, o_ref):
        r = jnp.maximum(f_ref[...], 0.0)
        o_ref[...] = (r * r).astype(o_ref.dtype)

    blk = pl.BlockSpec((tt, tn), lambda i, j: (i, j))
    return pl.pallas_call(body, name=name, grid=(t // tt, n // tn), in_specs=[blk], out_specs=blk,
                          out_shape=jax.ShapeDtypeStruct((t, n), MXU), compiler_params=_cp("parallel", "parallel"))(f)


def sqrelu_bwd(f, da, name):
    t, n = f.shape
    tt, tn = _tile(t, 256), _tile(n, 2048)

    def body(f_ref, da_ref, o_ref):
        o_ref[...] = (da_ref[...] * (2.0 * jnp.maximum(f_ref[...], 0.0))).astype(o_ref.dtype)

    blk = pl.BlockSpec((tt, tn), lambda i, j: (i, j))
    return pl.pallas_call(body, name=name, grid=(t // tt, n // tn), in_specs=[blk, blk], out_specs=blk,
                          out_shape=jax.ShapeDtypeStruct((t, n), MXU), compiler_params=_cp("parallel", "parallel"))(f, da)


def loss_head(x, target, name):
    t, d = x.shape
    tt = _tile(t, 256)

    def body(x_ref, t_ref, dx_ref, dxb_ref, l_ref):
        e = x_ref[...] - t_ref[...]
        dx = e * (1.0 / d)
        dx_ref[...] = dx
        dxb_ref[...] = dx.astype(dxb_ref.dtype)

        @pl.when(pl.program_id(0) == 0)
        def _():
            l_ref[...] = jnp.zeros_like(l_ref)

        part = jnp.sum(jnp.sum(e * e, axis=-1, keepdims=True) * (1.0 / d), axis=0, keepdims=True)
        l_ref[...] += 0.5 * part

    row = pl.BlockSpec((tt, d), lambda i: (i, 0))
    return pl.pallas_call(
        body, name=name, grid=(t // tt,), in_specs=[row, row],
        out_specs=[row, row, pl.BlockSpec((8, LANE), lambda i: (0, 0))],
        out_shape=[jax.ShapeDtypeStruct((t, d), F32), jax.ShapeDtypeStruct((t, d), MXU), jax.ShapeDtypeStruct((8, LANE), F32)],
        compiler_params=_cp("arbitrary"),
    )(x, target)


def _merge_f(g0, g1, g2, ba, bb, bc):
    return _sigmoid(g0) * ba + _sigmoid(g1) * bb + _sigmoid(g2) * bc


def merge_fwd(z, off, ba, bb, bc, name):
    t, d = ba.shape
    tt, td = _tile(t, 256), _tile(d // 2, 1024)
    nd, ob = d // td, off // td

    def body(g0, g1, g2, a, b, c, o_ref):
        o_ref[...] = _merge_f(g0[...], g1[...], g2[...], a[...], b[...], c[...]).astype(o_ref.dtype)

    gates = [pl.BlockSpec((tt, td), functools.partial(lambda i, j, s: (i, ob + s * nd + j), s=s)) for s in range(3)]
    blk = pl.BlockSpec((tt, td), lambda i, j: (i, j))
    return pl.pallas_call(body, name=name, grid=(t // tt, nd), in_specs=gates + [blk] * 3, out_specs=blk,
                          out_shape=jax.ShapeDtypeStruct((t, d), MXU), compiler_params=_cp("parallel", "parallel"))(z, z, z, ba, bb, bc)


def merge_bwd(z, off, ba, bb, bc, dy, name):
    t, d = ba.shape
    tt, td = _tile(t, 256), _tile(d // 2, 1024)
    nd, ob = d // td, off // td

    def body(g0, g1, g2, a, b, c, dy_ref, dgl, da, db, dc):
        _, vjp = jax.vjp(_merge_f, g0[...], g1[...], g2[...], a[...], b[...], c[...])
        d0, d1, d2, xa, xb, xc = vjp(dy_ref[...])
        for s, dv in enumerate((d0, d1, d2)):
            dgl[s] = dv.astype(dgl.dtype)
        da[...] = xa.astype(da.dtype)
        db[...] = xb.astype(db.dtype)
        dc[...] = xc.astype(dc.dtype)

    gates = [pl.BlockSpec((tt, td), functools.partial(lambda i, j, s: (i, ob + s * nd + j), s=s)) for s in range(3)]
    blk = pl.BlockSpec((tt, td), lambda i, j: (i, j))
    dgl, da, db, dc = pl.pallas_call(
        body, name=name, grid=(t // tt, nd), in_specs=gates + [blk] * 4,
        out_specs=[pl.BlockSpec((3, tt, td), lambda i, j: (0, i, j)), blk, blk, blk],
        out_shape=[jax.ShapeDtypeStruct((3, t, d), MXU)] + [jax.ShapeDtypeStruct((t, d), MXU)] * 3,
        compiler_params=_cp("parallel", "parallel"),
    )(z, z, z, ba, bb, bc, dy)
    return dgl, da, db, dc


def _gdn_pre_f(qp, kp, vp, ab, cq, ck, cv, alog, dtb, h, n_heads):
    def conv(xp, cw):
        acc = xp * cw[CONV_K - 1]
        for j in range(1, CONV_K):
            acc = acc + shift(xp, j) * cw[CONV_K - 1 - j]
        return _silu(acc)

    q, k, v = conv(qp, cq), conv(kp, ck), conv(vp, cv)
    q = q * lax.rsqrt(jnp.sum(q * q, axis=-1, keepdims=True) + EPS) * (HD ** -0.5)
    k = k * lax.rsqrt(jnp.sum(k * k, axis=-1, keepdims=True) + EPS)
    lane = _iota2(ab.shape, 1)
    a_col = jnp.sum(jnp.where(lane == h, ab, 0.0), axis=-1, keepdims=True)
    b_col = jnp.sum(jnp.where(lane == n_heads + h, ab, 0.0), axis=-1, keepdims=True)
    lane1 = _iota2(alog.shape, 1)
    al = jnp.sum(jnp.where(lane1 == h, alog, 0.0), axis=-1, keepdims=True)
    dt = jnp.sum(jnp.where(lane1 == h, dtb, 0.0), axis=-1, keepdims=True)
    g = -jnp.exp(al) * _softplus(a_col + dt)
    return q, k, v, g, _sigmoid(b_col)


def _gdn_pre_specs(t, n_heads, ab_blk):
    zq = [pl.BlockSpec((t, HD), functools.partial(lambda h, s: (0, s * n_heads + h), s=s)) for s in range(3)]
    ab = pl.BlockSpec((t, LANE), lambda h: (0, ab_blk))
    cw = [pl.BlockSpec((CONV_K, HD), functools.partial(lambda h, s: (0, s * n_heads + h), s=s)) for s in range(3)]
    vec = pl.BlockSpec((1, LANE), lambda h: (0, 0))
    return zq, ab, cw, vec


def gdn_pre_fwd(z, ab_blk, conv_w, alog, dtb, n_heads, name):
    t = z.shape[0]
    zq, ab, cw, vec = _gdn_pre_specs(t, n_heads, ab_blk)

    def body(qp, kp, vp, ab_ref, cq, ck, cv, al, dt, q_o, k_o, v_o, g_o, b_o):
        rows = lambda r: tuple(r[j:j + 1, :] for j in range(CONV_K))
        outs = _gdn_pre_f(qp[...], kp[...], vp[...], ab_ref[...], rows(cq), rows(ck), rows(cv), al[...], dt[...],
                          pl.program_id(0), n_heads)
        for o_ref, val in zip((q_o, k_o, v_o, g_o, b_o), outs):
            o_ref[...] = val

    head = pl.BlockSpec((None, t, HD), lambda h: (h, 0, 0))
    col = pl.BlockSpec((None, t, 1), lambda h: (h, 0, 0))
    return pl.pallas_call(
        body, name=name, grid=(n_heads,), in_specs=zq + [ab] + cw + [vec, vec], out_specs=[head] * 3 + [col] * 2,
        out_shape=[jax.ShapeDtypeStruct((n_heads, t, HD), F32)] * 3 + [jax.ShapeDtypeStruct((n_heads, t, 1), F32)] * 2,
        compiler_params=_cp("parallel"),
    )(z, z, z, z, conv_w, conv_w, conv_w, alog, dtb)


def gdn_pre_bwd(z, ab_blk, conv_w, alog, dtb, n_heads, dq, dk, dv, dg, db, name):
    t = z.shape[0]
    gw = n_heads * HD
    zq, ab, cw, vec = _gdn_pre_specs(t, n_heads, ab_blk)

    def body(qp, kp, vp, ab_ref, cq, ck, cv, al, dt, dq_r, dk_r, dv_r, dg_r, db_r,
             dqp, dkp, dvp, dab, dcq, dck, dcv, dal, ddt):
        h = pl.program_id(0)
        rows = lambda r: tuple(r[j:j + 1, :] for j in range(CONV_K))
        f = functools.partial(_gdn_pre_f, h=h, n_heads=n_heads)
        _, vjp = jax.vjp(f, qp[...], kp[...], vp[...], ab_ref[...], rows(cq), rows(ck), rows(cv), al[...], dt[...])
        gq, gk, gv, gab, gcq, gck, gcv, gal, gdt = vjp((dq_r[...], dk_r[...], dv_r[...], dg_r[...], db_r[...]))
        dqp[...] = gq.astype(dqp.dtype)
        dkp[...] = gk.astype(dkp.dtype)
        dvp[...] = gv.astype(dvp.dtype)
        for ref, gr in ((dcq, gcq), (dck, gck), (dcv, gcv)):
            for j in range(CONV_K):
                ref[j:j + 1, :] = gr[j]

        @pl.when(h == 0)
        def _():
            dab[...] = jnp.zeros_like(dab)
            dal[...] = jnp.zeros_like(dal)
            ddt[...] = jnp.zeros_like(ddt)

        dab[...] += gab
        dal[...] += gal
        ddt[...] += gdt

    head = pl.BlockSpec((None, t, HD), lambda h: (h, 0, 0))
    col = pl.BlockSpec((None, t, 1), lambda h: (h, 0, 0))
    seg = pl.BlockSpec((t, HD), lambda h: (0, h))
    cseg = pl.BlockSpec((CONV_K, HD), lambda h: (0, h))
    return pl.pallas_call(
        body, name=name, grid=(n_heads,),
        in_specs=zq + [ab] + cw + [vec, vec] + [head] * 3 + [col] * 2,
        out_specs=[seg] * 3 + [pl.BlockSpec((t, LANE), lambda h: (0, 0))] + [cseg] * 3 + [vec, vec],
        out_shape=[jax.ShapeDtypeStruct((t, gw), MXU)] * 3 + [jax.ShapeDtypeStruct((t, LANE), F32)]
        + [jax.ShapeDtypeStruct((CONV_K, gw), F32)] * 3 + [jax.ShapeDtypeStruct((1, LANE), F32)] * 2,
        compiler_params=_cp("arbitrary"),
    )(z, z, z, z, conv_w, conv_w, conv_w, alog, dtb, dq, dk, dv, dg, db)


def _gdn_chunk_f(q, k, v, g, b):
    c = CHUNK
    r, s = _iota2((c, c), 0), _iota2((c, c), 1)
    tril = (s <= r).astype(F32)
    gc_w = mm(tril, jnp.broadcast_to(g, (c, HD)), "nn", True)
    gc_i = mm(tril, jnp.broadcast_to(g, (c, c)), "nn", True)
    gc_j = mm(jnp.ones((c, c), F32), jnp.where(r == s, gc_i, 0.0), "nn", True)
    decay = jnp.exp(jnp.where(s <= r, gc_i - gc_j, -1e30))
    kb = k * b
    low = jnp.where(s < r, mm(kb, k, "nt", True) * decay, 0.0)
    inv = jnp.where(r == s, 1.0, 0.0) - low
    pw = mm(low, low, "nn", True)
    n_sq = 1
    while 2 * n_sq < c:
        inv = inv + mm(inv, pw, "nn", True)
        n_sq *= 2
        if 2 * n_sq < c:
            pw = mm(pw, pw, "nn", True)
    egc = jnp.exp(gc_w)
    u = mm(inv, v * b, "nn", True)
    w = mm(inv, kb * egc, "nn", True)
    intra = mm(q, k, "nt", False) * decay
    g_last = jnp.sum(g, axis=0, keepdims=True)
    kd = k * jnp.exp(g_last - gc_w)
    egl = jnp.exp(jnp.broadcast_to(g_last, (1, HD)))
    return u, w, intra, q * egc, kd, egl


def _chunk_specs():
    vec = pl.BlockSpec((None, CHUNK, HD), lambda h, n: (h, n, 0))
    col = pl.BlockSpec((None, CHUNK, 1), lambda h, n: (h, n, 0))
    sq = pl.BlockSpec((None, CHUNK, CHUNK), lambda h, n: (h, n, 0))
    one = pl.BlockSpec((None, None, 1, HD), lambda h, n: (h, n, 0, 0))
    return vec, col, sq, one


def _chunk_shapes(n_heads, t):
    vec = jax.ShapeDtypeStruct((n_heads, t, HD), F32)
    return [vec, vec, jax.ShapeDtypeStruct((n_heads, t, CHUNK), F32), vec, vec,
            jax.ShapeDtypeStruct((n_heads, t // CHUNK, 1, HD), F32)]


def gdn_chunk_fwd(q, k, v, g, b, name):
    n_heads, t, _ = q.shape
    vec, col, sq, one = _chunk_specs()

    def body(q_r, k_r, v_r, g_r, b_r, *outs):
        for o_ref, val in zip(outs, _gdn_chunk_f(q_r[...], k_r[...], v_r[...], g_r[...], b_r[...])):
            o_ref[...] = val

    return pl.pallas_call(
        body, name=name, grid=(n_heads, t // CHUNK), in_specs=[vec] * 3 + [col] * 2,
        out_specs=[vec, vec, sq, vec, vec, one], out_shape=_chunk_shapes(n_heads, t),
        compiler_params=_cp("parallel", "parallel"),
    )(q, k, v, g, b)


def gdn_chunk_bwd(q, k, v, g, b, cts, name):
    n_heads, t, _ = q.shape
    vec, col, sq, one = _chunk_specs()

    def body(q_r, k_r, v_r, g_r, b_r, du, dw, di, dqd, dkd, degl, dq, dk, dv, dg, db):
        _, vjp = jax.vjp(_gdn_chunk_f, q_r[...], k_r[...], v_r[...], g_r[...], b_r[...])
        grads = vjp((du[...], dw[...], di[...], dqd[...], dkd[...], degl[...]))
        for o_ref, val in zip((dq, dk, dv, dg, db), grads):
            o_ref[...] = val

    col_shape = jax.ShapeDtypeStruct((n_heads, t, 1), F32)
    return pl.pallas_call(
        body, name=name, grid=(n_heads, t // CHUNK),
        in_specs=[vec] * 3 + [col] * 2 + [vec, vec, sq, vec, vec, one],
        out_specs=[vec] * 3 + [col] * 2,
        out_shape=[jax.ShapeDtypeStruct((n_heads, t, HD), F32)] * 3 + [col_shape] * 2,
        compiler_params=_cp("parallel", "parallel"),
    )(q, k, v, g, b, *cts)


def _scan_f(s, u, w, a, qd, kd, egl):
    vn = u - mm(w, s, "nn", False)
    o = mm(qd, s, "nn", False) + mm(a, vn, "nn", False)
    return o, s * egl + mm(kd, vn, "tn", False)


def gdn_scan_fwd(chunks, name):
    u = chunks[0]
    n_heads, t, _ = u.shape
    nc = t // CHUNK
    vec, _, sq, one = _chunk_specs()

    def body(u_r, w_r, a_r, qd_r, kd_r, e_r, o_ref, s_ref, state):
        @pl.when(pl.program_id(1) == 0)
        def _():
            state[...] = jnp.zeros_like(state)

        s = state[...]
        s_ref[...] = s
        o, s2 = _scan_f(s, u_r[...], w_r[...], a_r[...], qd_r[...], kd_r[...], e_r[...])
        o_ref[...] = o
        state[...] = s2

    return pl.pallas_call(
        body, name=name, grid=(n_heads, nc), in_specs=[vec, vec, sq, vec, vec, one],
        out_specs=[vec, pl.BlockSpec((None, None, HD, HD), lambda h, n: (h, n, 0, 0))],
        out_shape=[jax.ShapeDtypeStruct((n_heads, t, HD), F32), jax.ShapeDtypeStruct((n_heads, nc, HD, HD), F32)],
        scratch_shapes=[pltpu.VMEM((HD, HD), F32)], compiler_params=_cp("parallel", "arbitrary"),
    )(*chunks)


def gdn_scan_bwd(chunks, states, do, name):
    n_heads, t, _ = do.shape
    nc = t // CHUNK
    rev = lambda spec_shape, nd: pl.BlockSpec(spec_shape, (lambda h, n: (h, nc - 1 - n, 0)) if nd == 3 else (lambda h, n: (h, nc - 1 - n, 0, 0)))
    vec, sq = rev((None, CHUNK, HD), 3), rev((None, CHUNK, CHUNK), 3)
    one, st = rev((None, None, 1, HD), 4), rev((None, None, HD, HD), 4)

    def body(u_r, w_r, a_r, qd_r, kd_r, e_r, s_r, do_r, du, dw, da, dqd, dkd, de, dstate):
        @pl.when(pl.program_id(1) == 0)
        def _():
            dstate[...] = jnp.zeros_like(dstate)

        _, vjp = jax.vjp(_scan_f, s_r[...], u_r[...], w_r[...], a_r[...], qd_r[...], kd_r[...], e_r[...])
        grads = vjp((do_r[...], dstate[...]))
        dstate[...] = grads[0]
        for o_ref, val in zip((du, dw, da, dqd, dkd, de), grads[1:]):
            o_ref[...] = val

    return pl.pallas_call(
        body, name=name, grid=(n_heads, nc), in_specs=[vec, vec, sq, vec, vec, one, st, vec],
        out_specs=[vec, vec, sq, vec, vec, one], out_shape=_chunk_shapes(n_heads, t),
        scratch_shapes=[pltpu.VMEM((HD, HD), F32)], compiler_params=_cp("parallel", "arbitrary"),
    )(*chunks, states, do)


def _post_f(o, gate, gain):
    return _rms(o, gain) * _silu(gate)


def gdn_post_fwd(o, z, gate_blk, gain, name):
    n_heads, t, _ = o.shape
    tt = _tile(t, 512)

    def body(o_r, gt_r, gn_r, out):
        out[...] = _post_f(o_r[...], gt_r[...], gn_r[...]).astype(out.dtype)

    return pl.pallas_call(
        body, name=name, grid=(n_heads, t // tt),
        in_specs=[pl.BlockSpec((None, tt, HD), lambda h, i: (h, i, 0)), pl.BlockSpec((tt, HD), lambda h, i: (i, gate_blk + h)),
                  pl.BlockSpec((1, HD), lambda h, i: (0, 0))],
        out_specs=pl.BlockSpec((tt, HD), lambda h, i: (i, h)),
        out_shape=jax.ShapeDtypeStruct((t, n_heads * HD), MXU), compiler_params=_cp("parallel", "parallel"),
    )(o, z, gain)


def gdn_post_bwd(o, z, gate_blk, gain, doa, name):
    n_heads, t, _ = o.shape
    tt = _tile(t, 512)

    def body(o_r, gt_r, gn_r, d_r, do_ref, dgt_ref, dgn_ref):
        _, vjp = jax.vjp(_post_f, o_r[...], gt_r[...], gn_r[...])
        go, ggt, ggn = vjp(d_r[...])
        do_ref[...] = go
        dgt_ref[...] = ggt.astype(dgt_ref.dtype)

        @pl.when((pl.program_id(0) == 0) & (pl.program_id(1) == 0))
        def _():
            dgn_ref[...] = jnp.zeros_like(dgn_ref)

        dgn_ref[...] += ggn

    tok = pl.BlockSpec((tt, HD), lambda h, i: (i, h))
    vec = pl.BlockSpec((1, HD), lambda h, i: (0, 0))
    head = pl.BlockSpec((None, tt, HD), lambda h, i: (h, i, 0))
    return pl.pallas_call(
        body, name=name, grid=(n_heads, t // tt),
        in_specs=[head, pl.BlockSpec((tt, HD), lambda h, i: (i, gate_blk + h)), vec, tok],
        out_specs=[head, tok, vec],
        out_shape=[jax.ShapeDtypeStruct((n_heads, t, HD), F32), jax.ShapeDtypeStruct((t, n_heads * HD), MXU),
                   jax.ShapeDtypeStruct((1, HD), F32)],
        compiler_params=_cp("arbitrary", "arbitrary"),
    )(o, z, gain, doa)


def _gmlp_f(ups, vps, lngs, wss, bcols):
    n_groups = len(ups)
    width = HD * n_groups
    us = [_gelu(a) for a in ups]
    vs = [_gelu(a) for a in vps]
    mu = sum(jnp.sum(a, axis=-1, keepdims=True) for a in vs) * (1.0 / width)
    xcs = [a - mu for a in vs]
    var = sum(jnp.sum(a * a, axis=-1, keepdims=True) for a in xcs) * (1.0 / width)
    rstd = lax.rsqrt(var + EPS)
    r, s = _iota2((HD, HD), 0), _iota2((HD, HD), 1)
    causal = (s // CHUNK) <= (r // CHUNK)
    outs = []
    for gi in range(n_groups):
        vb = xcs[gi] * rstd * lngs[gi]
        sp = mm(jnp.where(causal, wss[gi], 0.0), vb, "nn", False) + bcols[gi]
        outs.append(us[gi] * sp)
    return tuple(outs)


def _gmlp_load(uv_u, uv_v, lng, ws, bt, n_groups):
    seg = lambda ref, gi: ref[:, gi * HD:(gi + 1) * HD]
    return ([seg(uv_u, gi) for gi in range(n_groups)], [seg(uv_v, gi) for gi in range(n_groups)],
            [seg(lng, gi) for gi in range(n_groups)], [ws[gi] for gi in range(n_groups)],
            [bt[:, gi:gi + 1] for gi in range(n_groups)])


def _gmlp_specs(width, u_blk, n_groups):
    u = pl.BlockSpec((HD, width), lambda i: (i, u_blk))
    v = pl.BlockSpec((HD, width), lambda i: (i, u_blk + 1))
    lng = pl.BlockSpec((1, width), lambda i: (0, 0))
    ws = pl.BlockSpec((n_groups, HD, HD), lambda i: (0, 0, 0))
    bt = pl.BlockSpec((HD, LANE), lambda i: (0, 0))
    return u, v, lng, ws, bt


def gmlp_fwd(z, uv_off, width, lng, ws, bt, name):
    t = z.shape[0]
    n_groups = width // HD
    specs = _gmlp_specs(width, uv_off // width, n_groups)

    def body(u_r, v_r, l_r, w_r, b_r, out):
        outs = _gmlp_f(*_gmlp_load(u_r, v_r, l_r, w_r, b_r, n_groups))
        for gi in range(n_groups):
            out[:, gi * HD:(gi + 1) * HD] = outs[gi].astype(out.dtype)

    return pl.pallas_call(
        body, name=name, grid=(t // HD,), in_specs=list(specs), out_specs=pl.BlockSpec((HD, width), lambda i: (i, 0)),
        out_shape=jax.ShapeDtypeStruct((t, width), MXU), compiler_params=_cp("parallel"),
    )(z, z, lng, ws, bt)


def gmlp_bwd(z, uv_off, width, lng, ws, bt, dob, name):
    t = z.shape[0]
    n_groups = width // HD
    specs = _gmlp_specs(width, uv_off // width, n_groups)

    def body(u_r, v_r, l_r, w_r, b_r, d_r, duv, dl, dws, dbt):
        _, vjp = jax.vjp(_gmlp_f, *_gmlp_load(u_r, v_r, l_r, w_r, b_r, n_groups))
        gu, gv, gl, gw, gb = vjp(tuple(d_r[:, gi * HD:(gi + 1) * HD] for gi in range(n_groups)))

        @pl.when(pl.program_id(0) == 0)
        def _():
            dl[...] = jnp.zeros_like(dl)
            dws[...] = jnp.zeros_like(dws)
            dbt[...] = jnp.zeros_like(dbt)

        for gi in range(n_groups):
            duv[:, gi * HD:(gi + 1) * HD] = gu[gi].astype(duv.dtype)
            duv[:, width + gi * HD:width + (gi + 1) * HD] = gv[gi].astype(duv.dtype)
            dl[:, gi * HD:(gi + 1) * HD] += gl[gi]
            dws[gi] += gw[gi]
            dbt[:, gi:gi + 1] += gb[gi]

    return pl.pallas_call(
        body, name=name, grid=(t // HD,), in_specs=list(specs) + [pl.BlockSpec((HD, width), lambda i: (i, 0))],
        out_specs=[pl.BlockSpec((HD, 2 * width), lambda i: (i, 0)), specs[2], specs[3], specs[4]],
        out_shape=[jax.ShapeDtypeStruct((t, 2 * width), MXU), jax.ShapeDtypeStruct((1, width), F32),
                   jax.ShapeDtypeStruct((n_groups, HD, HD), F32), jax.ShapeDtypeStruct((HD, LANE), F32)],
        compiler_params=_cp("arbitrary"),
    )(z, z, lng, ws, bt, dob)


def _sba_pre_f(qp, kp, qg, kg):
    return _rms(qp, qg), _rms(kp, kg)


def sba_pre_fwd(z, c_blk, n_heads, qg, kg, name):
    t = z.shape[0]
    tt = _tile(t, 512)
    zs = [pl.BlockSpec((tt, HD), functools.partial(lambda h, i, s: (i, c_blk + s * n_heads + h), s=s)) for s in range(3)]
    vec = pl.BlockSpec((1, HD), lambda h, i: (0, 0))
    head = pl.BlockSpec((None, tt, HD), lambda h, i: (h, i, 0))

    def body(qp, kp, vp, qg_r, kg_r, q_o, k_o, v_o):
        q, k = _sba_pre_f(qp[...], kp[...], qg_r[...], kg_r[...])
        q_o[...] = q.astype(q_o.dtype)
        k_o[...] = k.astype(k_o.dtype)
        v_o[...] = vp[...].astype(v_o.dtype)

    return pl.pallas_call(
        body, name=name, grid=(n_heads, t // tt), in_specs=zs + [vec, vec], out_specs=[head] * 3,
        out_shape=[jax.ShapeDtypeStruct((n_heads, t, HD), MXU)] * 3, compiler_params=_cp("parallel", "parallel"),
    )(z, z, z, qg, kg)


def sba_pre_bwd(z, c_blk, n_heads, qg, kg, dq, dk, dv, name):
    t = z.shape[0]
    tt = _tile(t, 512)
    zs = [pl.BlockSpec((tt, HD), functools.partial(lambda h, i, s: (i, c_blk + s * n_heads + h), s=s)) for s in range(2)]
    vec = pl.BlockSpec((1, HD), lambda h, i: (0, 0))
    head = pl.BlockSpec((None, tt, HD), lambda h, i: (h, i, 0))
    tok = pl.BlockSpec((tt, HD), lambda h, i: (i, h))

    def body(qp, kp, qg_r, kg_r, dq_r, dk_r, dv_r, dqp, dkp, dvp, dqg, dkg):
        _, vjp = jax.vjp(_sba_pre_f, qp[...], kp[...], qg_r[...], kg_r[...])
        gq, gk, gqg, gkg = vjp((dq_r[...], dk_r[...]))
        dqp[...] = gq.astype(dqp.dtype)
        dkp[...] = gk.astype(dkp.dtype)
        dvp[...] = dv_r[...].astype(dvp.dtype)

        @pl.when((pl.program_id(0) == 0) & (pl.program_id(1) == 0))
        def _():
            dqg[...] = jnp.zeros_like(dqg)
            dkg[...] = jnp.zeros_like(dkg)

        dqg[...] += gqg
        dkg[...] += gkg

    return pl.pallas_call(
        body, name=name, grid=(n_heads, t // tt), in_specs=zs + [vec, vec] + [head] * 3,
        out_specs=[tok] * 3 + [vec, vec],
        out_shape=[jax.ShapeDtypeStruct((t, n_heads * HD), MXU)] * 3 + [jax.ShapeDtypeStruct((1, HD), F32)] * 2,
        compiler_params=_cp("arbitrary", "arbitrary"),
    )(z, z, qg, kg, dq, dk, dv)


def _sba_block(q, kj, i, j):
    zz = lax.dot_general(q, kj, _DN["nt"], preferred_element_type=F32) * (HD ** -0.5)
    ls = _logsig(zz)
    strict = (j * HD + _iota2((HD, HD), 1)) < (i * HD + _iota2((HD, HD), 0))
    return zz, ls, jnp.where(strict, ls - zz, 0.0), strict


def sba_fwd(q, k, v, name):
    n_heads, t, _ = q.shape

    def body(q_r, k_r, v_r, o_ref, tot_ref):
        i = pl.program_id(1)
        qb = q_r[...]
        after = (_iota2((HD, HD), 0) > _iota2((HD, HD), 1)).astype(F32)

        def step(it, carry):
            acc, cs = carry
            j = i - it
            rows = pl.ds(pl.multiple_of(j * HD, HD), HD)
            _, ls, lk, strict = _sba_block(qb, k_r[rows, :], i, j)
            suffix = _mm_raw(lk, after, "nn", True) + cs
            att = jnp.where(strict, jnp.exp(ls + suffix), 0.0)
            acc = acc + _mm_raw(att, v_r[rows, :], "nn", False)
            return acc, cs + jnp.sum(lk, axis=-1, keepdims=True)

        acc, cs = lax.fori_loop(0, i + 1, step, (jnp.zeros((HD, HD), F32), jnp.zeros((HD, 1), F32)))
        o_ref[...] = acc.astype(o_ref.dtype)
        tot_ref[...] = cs

    full = pl.BlockSpec((None, t, HD), lambda h, i: (h, 0, 0))
    return pl.pallas_call(
        body, name=name, grid=(n_heads, t // HD),
        in_specs=[pl.BlockSpec((None, HD, HD), lambda h, i: (h, i, 0)), full, full],
        out_specs=[pl.BlockSpec((HD, HD), lambda h, i: (i, h)), pl.BlockSpec((None, HD, 1), lambda h, i: (h, i, 0))],
        out_shape=[jax.ShapeDtypeStruct((t, n_heads * HD), MXU), jax.ShapeDtypeStruct((n_heads, t, 1), F32)],
        compiler_params=_cp("parallel", "parallel"),
    )(q, k, v)


def sba_bwd(q, k, v, tot, do, name):
    n_heads, t, _ = q.shape

    def body(q_r, k_r, v_r, tot_r, do_r, dq_ref, dk_ref, dv_ref):
        i = pl.program_id(1)

        @pl.when(i == 0)
        def _():
            dk_ref[...] = jnp.zeros_like(dk_ref)
            dv_ref[...] = jnp.zeros_like(dv_ref)

        qb, dob, tot_b = q_r[...], do_r[...], tot_r[...]
        r, s = _iota2((HD, HD), 0), _iota2((HD, HD), 1)
        upto = (r <= s).astype(F32)
        before = (r < s).astype(F32)

        def step(j, carry):
            dq, cp, cd = carry
            rows = pl.ds(pl.multiple_of(j * HD, HD), HD)
            kj, vj = k_r[rows, :], v_r[rows, :]
            _, ls, lk, strict = _sba_block(qb, kj, i, j)
            sig = jnp.exp(ls)
            suffix = tot_b - (cp + _mm_raw(lk, upto, "nn", True))
            att = jnp.where(strict, jnp.exp(ls + suffix), 0.0)
            dp = _mm_raw(dob, vj, "nt", False) * att
            dlk = cd + _mm_raw(dp, before, "nn", True)
            dz = jnp.where(strict, dp * (1.0 - sig) - dlk * sig, 0.0) * (HD ** -0.5)
            dk_ref[rows, :] += _mm_raw(dz, qb, "tn", False)
            dv_ref[rows, :] += _mm_raw(att, dob, "tn", False)
            return (dq + _mm_raw(dz, kj, "nn", False), cp + jnp.sum(lk, axis=-1, keepdims=True),
                    cd + jnp.sum(dp, axis=-1, keepdims=True))

        zero_col = jnp.zeros((HD, 1), F32)
        dq, _, _ = lax.fori_loop(0, i + 1, step, (jnp.zeros((HD, HD), F32), zero_col, zero_col))
        dq_ref[...] = dq

    full = pl.BlockSpec((None, t, HD), lambda h, i: (h, 0, 0))
    blk = pl.BlockSpec((None, HD, HD), lambda h, i: (h, i, 0))
    return pl.pallas_call(
        body, name=name, grid=(n_heads, t // HD),
        in_specs=[blk, full, full, pl.BlockSpec((None, HD, 1), lambda h, i: (h, i, 0)), pl.BlockSpec((HD, HD), lambda h, i: (i, h))],
        out_specs=[blk, full, full], out_shape=[jax.ShapeDtypeStruct((n_heads, t, HD), F32)] * 3,
        compiler_params=_cp("parallel", "arbitrary"),
    )(q, k, v, tot, do)


def small_adam(parts, w, m, v, name):
    n_parts, rows, _ = parts.shape
    tr = _tile(rows, 512) if rows % LANE == 0 else rows

    def body(p_ref, w_ref, m_ref, v_ref, g_out, d_out, m_out, v_out):
        g = p_ref[0]
        for k in range(1, n_parts):
            g = g + p_ref[k]
        delta, m2, v2 = _adam_math(w_ref[...], g, m_ref[...], v_ref[...])
        g_out[...] = g
        d_out[...] = delta
        m_out[...] = m2
        v_out[...] = v2

    blk = pl.BlockSpec((tr, LANE), lambda i: (i, 0))
    return pl.pallas_call(
        body, name=name, grid=(rows // tr,),
        in_specs=[pl.BlockSpec((n_parts, tr, LANE), lambda i: (0, i, 0)), blk, blk, blk], out_specs=[blk] * 4,
        out_shape=[jax.ShapeDtypeStruct((rows, LANE), F32)] * 4, compiler_params=_cp("parallel"),
    )(parts, w, m, v)


def _pack(arrays):
    flat = jnp.concatenate([a.reshape(-1).astype(F32) for a in arrays])
    pad = (-flat.shape[0]) % (8 * LANE)
    return jnp.pad(flat, (0, pad)).reshape(-1, LANE)


def _unpack(packed, shapes):
    flat, outs, pos = packed.reshape(-1), [], 0
    for shp in shapes:
        n = 1
        for s in shp:
            n *= s
        outs.append(flat[pos:pos + n].reshape(shp))
        pos += n
    return outs


def _pad_lanes(a):
    return jnp.pad(a, ((0, 0), (0, LANE - a.shape[1])))


def kernel(x, w_in, conv_w, a_log, dt_bias, gdn_norm_g, gmlp_ln_g, w_spatial, b_spatial, sba_q_g, sba_k_g, w_out_a, w_out_b, w_out_c, w_out, norm_mix_g, norm_mlp_g, w_ff1, w_ff2, loss_target, m_w_in, m_conv_w, m_a_log, m_dt_bias, m_gdn_norm_g, m_gmlp_ln_g, m_w_spatial, m_b_spatial, m_sba_q_g, m_sba_k_g, m_w_out_a, m_w_out_b, m_w_out_c, m_w_out, m_norm_mix_g, m_norm_mlp_g, m_w_ff1, m_w_ff2, v_w_in, v_conv_w, v_a_log, v_dt_bias, v_gdn_norm_g, v_gmlp_ln_g, v_w_spatial, v_b_spatial, v_sba_q_g, v_sba_k_g, v_w_out_a, v_w_out_b, v_w_out_c, v_w_out, v_norm_mix_g, v_norm_mlp_g, v_w_ff1, v_w_ff2):
    depth = w_in.shape[0]
    _, t, d = x.shape
    n_heads = d // 256
    gw = n_heads * HD
    width = d // 2
    n_groups = width // HD
    d_ff = w_ff1.shape[2] * N_DEV
    off_gate, off_uv, off_c, off_gl = 3 * gw, 4 * gw, 4 * gw + 2 * width, 7 * gw + 2 * width
    off_ab = off_gl + 3 * d
    n_packed = off_ab + LANE
    n_in = off_ab + 2 * n_heads
    assert w_in.shape[2] * N_DEV == n_in and t % LANE == 0 and d % 256 == 0

    ix, iy, ic = lax.axis_index("x"), lax.axis_index("y"), lax.axis_index("c")
    dev = 4 * ix + 2 * iy + ic
    c_idx = jnp.reshape(ic, (1,)).astype(jnp.int32)
    xs, target = x[0], loss_target[0]

    def gather_cols(w, nm):
        g = all_gather(w.astype(MXU), nm)
        return lambda l: jnp.transpose(g[:, l], (1, 0, 2)).reshape(w.shape[1], N_DEV * w.shape[2])

    def gather_rows(w, nm):
        g = all_gather(w.astype(MXU), nm)
        return lambda l: g[:, l].reshape(N_DEV * w.shape[1], w.shape[2])

    get_in, get_oa, get_ob, get_oc = (gather_cols(w_in, "ag_w_in"), gather_cols(w_out_a, "ag_w_out_a"),
                                      gather_cols(w_out_b, "ag_w_out_b"), gather_cols(w_out_c, "ag_w_out_c"))
    get_out, get_ff1, get_ff2 = gather_rows(w_out, "ag_w_out"), gather_cols(w_ff1, "ag_w_ff1"), gather_rows(w_ff2, "ag_w_ff2")
    conv_all = all_gather(conv_w, "ag_conv_w")
    conv_full = jnp.transpose(conv_all, (1, 2, 0, 3)).reshape(depth, CONV_K, 3 * gw)

    def pack_in(w):
        return jnp.concatenate([w[:, :3 * gw], w[:, 3 * gw + 2 * n_heads:], w[:, 3 * gw:3 * gw + 2 * n_heads],
                                jnp.zeros((w.shape[0], LANE - 2 * n_heads), w.dtype)], axis=1)

    def unpack_in(wp):
        return jnp.concatenate([wp[:, :3 * gw], wp[:, off_ab:off_ab + 2 * n_heads], wp[:, 3 * gw:off_ab]], axis=1)

    alog_p, dtb_p = _pad_lanes(a_log), _pad_lanes(dt_bias)
    bt_all = jnp.pad(jnp.transpose(b_spatial, (0, 2, 1)), ((0, 0), (0, 0), (0, LANE - n_groups)))

    saved = []
    cur = xs
    for l in range(depth):
        lw = dict(w_in=pack_in(get_in(l)), w_oa=get_oa(l), w_ob=get_ob(l), w_oc=get_oc(l), w_out=get_out(l),
                  w_ff1=get_ff1(l), w_ff2=get_ff2(l), conv=conv_full[l], alog=alog_p[l:l + 1], dtb=dtb_p[l:l + 1],
                  gng=gdn_norm_g[l:l + 1], lng=gmlp_ln_g[l:l + 1], ws=w_spatial[l], bt=bt_all[l],
                  qg=sba_q_g[l:l + 1], kg=sba_k_g[l:l + 1], gmix=norm_mix_g[l:l + 1], gmlp=norm_mlp_g[l:l + 1])
        s = dict(lw=lw, x=cur)
        s["h1"] = rms_fwd(cur, lw["gmix"], "rms_mix")
        z = s["z"] = matmul(s["h1"], lw["w_in"], "nn", F32, "mm_in", caps=(1024, 1408, 512))
        s["pre"] = gdn_pre_fwd(z, off_ab // LANE, lw["conv"], lw["alog"], lw["dtb"], n_heads, "gdn_pre")
        s["chunks"] = gdn_chunk_fwd(*s["pre"], "gdn_chunk")
        s["o"], s["states"] = gdn_scan_fwd(s["chunks"], "gdn_scan")
        s["oa"] = gdn_post_fwd(s["o"], z, off_gate // HD, lw["gng"], "gdn_post")
        s["ob"] = gmlp_fwd(z, off_uv, width, lw["lng"], lw["ws"], lw["bt"], "gmlp")
        s["qkv_c"] = sba_pre_fwd(z, off_c // HD, n_heads, lw["qg"], lw["kg"], "sba_pre")
        s["oc"], s["tot"] = sba_fwd(*s["qkv_c"], "sba")
        s["ba"] = matmul(s["oa"], lw["w_oa"], "nn", F32, "mm_oa")
        s["bb"] = matmul(s["ob"], lw["w_ob"], "nn", F32, "mm_ob")
        s["bc"] = matmul(s["oc"], lw["w_oc"], "nn", F32, "mm_oc")
        s["y"] = merge_fwd(z, off_gl, s["ba"], s["bb"], s["bc"], "merge")
        s["x1"] = matmul(s["y"], lw["w_out"], "nn", F32, "mm_out", res=cur)
        s["h2"] = rms_fwd(s["x1"], lw["gmlp"], "rms_mlp")
        s["f"] = matmul(s["h2"], lw["w_ff1"], "nn", F32, "mm_ff1")
        s["a"] = sqrelu_fwd(s["f"], "sqrelu")
        cur = matmul(s["a"], lw["w_ff2"], "nn", F32, "mm_ff2", res=s["x1"])
        saved.append(s)

    dx, dxb, loss_tile = loss_head(cur, target, "loss_head")
    loss = lax.psum(loss_tile[0, 0], AXES)

    big = dict(w_in=(w_in, m_w_in, v_w_in), w_out_a=(w_out_a, m_w_out_a, v_w_out_a), w_out_b=(w_out_b, m_w_out_b, v_w_out_b),
               w_out_c=(w_out_c, m_w_out_c, v_w_out_c), w_out=(w_out, m_w_out, v_w_out), w_ff1=(w_ff1, m_w_ff1, v_w_ff1),
               w_ff2=(w_ff2, m_w_ff2, v_w_ff2))
    bufs = {nm: tuple(lax.empty(w.shape, F32) for _ in range(4)) for nm, (w, _, _) in big.items()}

    def reduce_update(nm, grad_full, by_rows, l):
        w, m, v = big[nm]
        if by_rows:
            g8 = grad_full.reshape(N_DEV, w.shape[1], w.shape[2])
        else:
            g8 = jnp.transpose(grad_full.reshape(w.shape[1], N_DEV, w.shape[2]), (1, 0, 2))
        r4 = rs_pair_exchange(g8, "rs_pair_" + nm)
        p4 = pair_sum(g8, r4, c_idx, "rs_pair_sum_" + nm)
        parts = rs_chip_exchange(p4, "rs_chip_" + nm)
        bufs[nm] = tuple(adam_layer(parts, w, m, v, bufs[nm], l, "adam_" + nm))

    small_grads = []
    for l in reversed(range(depth)):
        s = saved[l]
        lw, z = s["lw"], s["z"]
        da = matmul(dxb, lw["w_ff2"], "nt", F32, "mm_ff2_dx")
        reduce_update("w_ff2", matmul(s["a"], dxb, "tn", MXU, "mm_ff2_dw"), True, l)
        df = sqrelu_bwd(s["f"], da, "sqrelu_bwd")
        dh2 = matmul(df, lw["w_ff1"], "nt", F32, "mm_ff1_dx")
        reduce_update("w_ff1", matmul(s["h2"], df, "tn", MXU, "mm_ff1_dw"), False, l)
        dx1, dx1b, d_gmlp = rms_bwd(s["x1"], lw["gmlp"], dh2, dx, "rms_mlp_bwd")
        dy = matmul(dx1b, lw["w_out"], "nt", F32, "mm_out_dx")
        reduce_update("w_out", matmul(s["y"], dx1b, "tn", MXU, "mm_out_dw"), True, l)
        dgl, dba, dbb, dbc = merge_bwd(z, off_gl, s["ba"], s["bb"], s["bc"], dy, "merge_bwd")
        doa = matmul(dba, lw["w_oa"], "nt", F32, "mm_oa_dx")
        dob = matmul(dbb, lw["w_ob"], "nt", F32, "mm_ob_dx")
        doc = matmul(dbc, lw["w_oc"], "nt", F32, "mm_oc_dx")
        reduce_update("w_out_a", matmul(s["oa"], dba, "tn", MXU, "mm_oa_dw"), False, l)
        reduce_update("w_out_b", matmul(s["ob"], dbb, "tn", MXU, "mm_ob_dw"), False, l)
        reduce_update("w_out_c", matmul(s["oc"], dbc, "tn", MXU, "mm_oc_dw"), False, l)
        dqc, dkc, dvc = sba_bwd(*s["qkv_c"], s["tot"], doc, "sba_bwd")
        dz_qc, dz_kc, dz_vc, d_qg, d_kg = sba_pre_bwd(z, off_c // HD, n_heads, lw["qg"], lw["kg"], dqc, dkc, dvc, "sba_pre_bwd")
        dz_uv, d_lng, d_ws, d_bt = gmlp_bwd(z, off_uv, width, lw["lng"], lw["ws"], lw["bt"], dob, "gmlp_bwd")
        do, dz_gate, d_gng = gdn_post_bwd(s["o"], z, off_gate // HD, lw["gng"], doa, "gdn_post_bwd")
        chunk_cts = gdn_scan_bwd(s["chunks"], s["states"], do, "gdn_scan_bwd")
        dqa, dka, dva, dga, dba_ = gdn_chunk_bwd(*s["pre"], chunk_cts, "gdn_chunk_bwd")
        dz_q, dz_k, dz_v, d_ab, d_cq, d_ck, d_cv, d_alog, d_dtb = gdn_pre_bwd(
            z, off_ab // LANE, lw["conv"], lw["alog"], lw["dtb"], n_heads, dqa, dka, dva, dga, dba_, "gdn_pre_bwd")
        dz = jnp.concatenate([dz_q, dz_k, dz_v, dz_gate, dz_uv, dz_qc, dz_kc, dz_vc, dgl[0], dgl[1], dgl[2],
                              d_ab.astype(MXU)], axis=1)
        dh1 = matmul(dz, lw["w_in"], "nt", F32, "mm_in_dx", caps=(1024, 1024, 1408))
        reduce_update("w_in", unpack_in(matmul(s["h1"], dz, "tn", MXU, "mm_in_dw", caps=(1024, 1408, 512))), False, l)
        dx, dxb, d_gmix = rms_bwd(s["x"], lw["gmix"], dh1, dx1, "rms_mix_bwd")
        small_grads.append(dict(
            conv_w=jnp.concatenate([d_cq, d_ck, d_cv], axis=1), a_log=d_alog[0, :n_heads], dt_bias=d_dtb[0, :n_heads],
            gdn_norm_g=d_gng[0], gmlp_ln_g=d_lng[0], w_spatial=d_ws, b_spatial=jnp.transpose(d_bt[:, :n_groups]),
            sba_q_g=d_qg[0], sba_k_g=d_kg[0], norm_mix_g=d_gmix[0], norm_mlp_g=d_gmlp[0]))
    small_grads = small_grads[::-1]

    rep_names = ["a_log", "dt_bias", "gdn_norm_g", "gmlp_ln_g", "w_spatial", "b_spatial", "sba_q_g", "sba_k_g",
                 "norm_mix_g", "norm_mlp_g"]
    rep = dict(a_log=(a_log, m_a_log, v_a_log), dt_bias=(dt_bias, m_dt_bias, v_dt_bias),
               gdn_norm_g=(gdn_norm_g, m_gdn_norm_g, v_gdn_norm_g), gmlp_ln_g=(gmlp_ln_g, m_gmlp_ln_g, v_gmlp_ln_g),
               w_spatial=(w_spatial, m_w_spatial, v_w_spatial), b_spatial=(b_spatial, m_b_spatial, v_b_spatial),
               sba_q_g=(sba_q_g, m_sba_q_g, v_sba_q_g), sba_k_g=(sba_k_g, m_sba_k_g, v_sba_k_g),
               norm_mix_g=(norm_mix_g, m_norm_mix_g, v_norm_mix_g), norm_mlp_g=(norm_mlp_g, m_norm_mlp_g, v_norm_mlp_g))
    stack = lambda nm: jnp.stack([sg[nm] for sg in small_grads])
    conv_cols = conv_w.shape[2]
    conv_pad = jnp.zeros((depth, CONV_K, 3 * gw - conv_cols), F32)
    widen = lambda a: jnp.concatenate([a, conv_pad], axis=2)
    grads_packed = _pack([stack(nm) for nm in rep_names] + [stack("conv_w")])
    gathered = all_gather(grads_packed, "ag_small_grads")
    packed = [_pack([rep[nm][k] for nm in rep_names] + [widen((conv_w, m_conv_w, v_conv_w)[k])]) for k in range(3)]
    shapes = [rep[nm][0].shape for nm in rep_names] + [(depth, CONV_K, 3 * gw)]
    flat = gathered.reshape(N_DEV, -1)
    n_rep = sum(int(rep[nm][0].size) for nm in rep_names)
    conv_part = flat[:, n_rep:n_rep + depth * CONV_K * 3 * gw].reshape(N_DEV, depth, CONV_K, 3 * gw)
    conv_mine = lax.dynamic_slice_in_dim(conv_part, dev * conv_cols, conv_cols, axis=3)
    conv_mine = jnp.concatenate([conv_mine, jnp.zeros((N_DEV, depth, CONV_K, 3 * gw - conv_cols), F32)], axis=3)
    tail = flat[:, n_rep + depth * CONV_K * 3 * gw:]
    parts = jnp.concatenate([flat[:, :n_rep], conv_mine.reshape(N_DEV, -1), tail], axis=1).reshape(gathered.shape)
    outs_small = small_adam(parts, packed[0], packed[1], packed[2], "adam_small")
    small = [dict(zip(rep_names + ["conv_w"], _unpack(o, shapes))) for o in outs_small]
    for sm in small:
        sm["conv_w"] = sm["conv_w"][:, :, :conv_cols]

    order = ["w_in", "conv_w", "a_log", "dt_bias", "gdn_norm_g", "gmlp_ln_g", "w_spatial", "b_spatial", "sba_q_g",
             "sba_k_g", "w_out_a", "w_out_b", "w_out_c", "w_out", "norm_mix_g", "norm_mlp_g", "w_ff1", "w_ff2"]
    result = [loss, dx[None]]
    for kind in range(4):
        for nm in order:
            result.append(bufs[nm][kind] if nm in bufs else small[kind][nm])
    return tuple(result)
```

```python
import functools

import jax
import jax.numpy as jnp
from jax import lax
from jax.experimental import pallas as pl
from jax.experimental.pallas import tpu as pltpu

F32 = jnp.float32
MXU = jnp.bfloat16
N_DEV = 8
AXES = ("x", "y", "c")
CHUNK = 64
HD = 128
CONV_K = 4
EPS = 1e-6
LANE = 128
VMEM_LIMIT = 56 * 1024 * 1024
ADAM_LR, ADAM_B1, ADAM_B2, ADAM_EPS, ADAM_WD, ADAM_STEP = 0.001, 0.9, 0.999, 1e-08, 0.01, 10

_ANY = pl.BlockSpec(memory_space=pl.ANY)
_MESH = pl.DeviceIdType.MESH
_DN = {"nn": (((1,), (0,)), ((), ())), "nt": (((1,), (1,)), ((), ())), "tn": (((0,), (0,)), ((), ()))}


def _cp(*sem):
    return pltpu.CompilerParams(dimension_semantics=sem, vmem_limit_bytes=VMEM_LIMIT)


def _tile(n, cap):
    if n <= cap:
        return n
    best = LANE
    for t in range(LANE, cap + 1, LANE):
        if n % t == 0:
            best = t
    assert n % best == 0, (n, cap)
    return best


def _split(x, n):
    parts, rest = [], x.astype(F32)
    for _ in range(n):
        p = rest.astype(MXU)
        parts.append(p)
        rest = rest - p.astype(F32)
    return parts


def _mm_raw(a, b, mode, prec):
    dot = lambda p, q: lax.dot_general(p, q, _DN[mode], preferred_element_type=F32)
    if prec == "lo" or MXU == F32:
        return dot(a.astype(MXU), b.astype(MXU))
    if prec == "x3":
        (ah, al), (bh, bl) = _split(a, 2), _split(b, 2)
        return dot(ah, bh) + (dot(al, bh) + dot(ah, bl))
    if prec == "la":
        ae, (bh, bl) = a.astype(MXU), _split(b, 2)
        return dot(ae, bh) + dot(ae, bl)
    assert prec == "rb"
    (ah, al), be = _split(a, 2), b.astype(MXU)
    return dot(ah, be) + dot(al, be)


@functools.partial(jax.custom_vjp, nondiff_argnums=(2, 3))
def mm(a, b, mode, prec):
    return _mm_raw(a, b, mode, prec)


def _mm_fwd(a, b, mode, prec):
    return _mm_raw(a, b, mode, prec), (a, b)


def _mm_bwd(mode, prec, res, ct):
    a, b = res
    pa = {"la": None, "rb": "rb"}.get(prec, prec)
    pb = {"la": "la", "rb": None}.get(prec, prec)
    if mode == "nn":
        da = _mm_raw(ct, b, "nt", pa) if pa else None
        db = _mm_raw(a, ct, "tn", pb) if pb else None
    elif mode == "nt":
        da = _mm_raw(ct, b, "nn", pa) if pa else None
        db = _mm_raw(ct, a, "tn", {"la": "rb"}.get(pb, pb)) if pb else None
    else:
        da = _mm_raw(b, ct, "nt", {"rb": "la"}.get(pa, pa)) if pa else None
        db = _mm_raw(a, ct, "nn", pb) if pb else None
    da = jnp.zeros_like(a) if da is None else da.astype(a.dtype)
    db = jnp.zeros_like(b) if db is None else db.astype(b.dtype)
    return da, db


mm.defvjp(_mm_fwd, _mm_bwd)


def _shift_rows(x, j):
    n = x.shape[0]
    row = lax.broadcasted_iota(jnp.int32, x.shape, 0)
    if j > 0:
        return jnp.where(row >= j, pltpu.roll(x, j, 0), 0.0)
    return jnp.where(row < n + j, pltpu.roll(x, n + j, 0), 0.0)


@functools.partial(jax.custom_vjp, nondiff_argnums=(1,))
def shift(x, j):
    return _shift_rows(x, j)


shift.defvjp(lambda x, j: (_shift_rows(x, j), None), lambda j, _, ct: (_shift_rows(ct, -j),))


def _sigmoid(x):
    return 1.0 / (1.0 + jnp.exp(-x))


def _silu(x):
    return x * _sigmoid(x)


def _softplus(x):
    return jnp.maximum(x, 0.0) + jnp.log(1.0 + jnp.exp(-jnp.abs(x)))


def _logsig(x):
    return jnp.minimum(x, 0.0) - jnp.log(1.0 + jnp.exp(-jnp.abs(x)))


def _gelu(x):
    return 0.5 * x * (1.0 + lax.erf(x * (2.0 ** -0.5)))


def _rms(x, g):
    return x * lax.rsqrt(jnp.mean(x * x, axis=-1, keepdims=True) + EPS) * g


def _iota2(shape, dim):
    return lax.broadcasted_iota(jnp.int32, shape, dim)


def all_gather(x, name):
    def body(x_ref, out_ref, send_sems, recv_sems, local_sem):
        ix, iy, ic = lax.axis_index("x"), lax.axis_index("y"), lax.axis_index("c")
        me, sibling = (ix, iy, ic), (ix, iy, 1 - ic)
        chips = [(1 - ix, iy), (ix, 1 - iy), (1 - ix, 1 - iy)]

        def slot(px, py, pc):
            return out_ref.at[4 * px + 2 * py + pc]

        def copy(k, block, to, src=None):
            return pltpu.make_async_remote_copy(
                src_ref=slot(*block) if src is None else src, dst_ref=slot(*block),
                send_sem=send_sems.at[k], recv_sem=recv_sems.at[k], device_id=to, device_id_type=_MESH)

        mine = pltpu.make_async_copy(x_ref, slot(*me), local_sem)
        mine.start()
        first = [copy(0, me, sibling, src=x_ref)]
        first += [copy(1 + j, me, (*chip, ic), src=x_ref) for j, chip in enumerate(chips)]
        for cp in first:
            cp.start()
        passed = [copy(4 + j, (*chip, ic), sibling) for j, chip in enumerate(chips)]
        for j, chip in enumerate(chips):
            copy(1 + j, (*chip, ic), me).wait_recv()
            passed[j].start()
        copy(0, sibling, me).wait_recv()
        for j, chip in enumerate(chips):
            copy(4 + j, (*chip, 1 - ic), me).wait_recv()
        for cp in first + passed:
            cp.wait_send()
        mine.wait()

    return pl.pallas_call(
        body, name=name, out_shape=jax.ShapeDtypeStruct((N_DEV,) + x.shape, x.dtype),
        in_specs=[_ANY], out_specs=_ANY,
        scratch_shapes=[pltpu.SemaphoreType.DMA((7,)), pltpu.SemaphoreType.DMA((7,)), pltpu.SemaphoreType.DMA(())],
    )(x)


def rs_pair_exchange(g8, name):
    def body(g_ref, r_ref, send_sems, recv_sems):
        ix, iy, ic = lax.axis_index("x"), lax.axis_index("y"), lax.axis_index("c")
        copies = [
            pltpu.make_async_remote_copy(
                src_ref=g_ref.at[2 * xy + (1 - ic)], dst_ref=r_ref.at[xy],
                send_sem=send_sems.at[xy], recv_sem=recv_sems.at[xy],
                device_id=(ix, iy, 1 - ic), device_id_type=_MESH)
            for xy in range(4)
        ]
        for cp in copies:
            cp.start()
        for cp in copies:
            cp.wait()

    return pl.pallas_call(
        body, name=name, out_shape=jax.ShapeDtypeStruct((4,) + g8.shape[1:], g8.dtype),
        in_specs=[_ANY], out_specs=_ANY,
        scratch_shapes=[pltpu.SemaphoreType.DMA((4,)), pltpu.SemaphoreType.DMA((4,))],
    )(g8)


def rs_chip_exchange(p4, name):
    def body(p_ref, r_ref, send_sems, recv_sems, local_sem):
        ix, iy, ic = lax.axis_index("x"), lax.axis_index("y"), lax.axis_index("c")
        my_xy = 2 * ix + iy
        local = pltpu.make_async_copy(p_ref.at[my_xy], r_ref.at[my_xy], local_sem)
        local.start()
        chips = [(1 - ix, iy), (ix, 1 - iy), (1 - ix, 1 - iy)]
        copies = [
            pltpu.make_async_remote_copy(
                src_ref=p_ref.at[2 * px + py], dst_ref=r_ref.at[my_xy],
                send_sem=send_sems.at[k], recv_sem=recv_sems.at[k],
                device_id=(px, py, ic), device_id_type=_MESH)
            for k, (px, py) in enumerate(chips)
        ]
        for cp in copies:
            cp.start()
        for cp in copies:
            cp.wait()
        local.wait()

    return pl.pallas_call(
        body, name=name, out_shape=jax.ShapeDtypeStruct(p4.shape, p4.dtype),
        in_specs=[_ANY], out_specs=_ANY,
        scratch_shapes=[pltpu.SemaphoreType.DMA((3,)), pltpu.SemaphoreType.DMA((3,)), pltpu.SemaphoreType.DMA(())],
    )(p4)


def pair_sum(g8, r4, c_idx, name):
    _, rows, cols = g8.shape
    tr = _tile_rows(rows, cols)

    def body(c_ref, g_ref, r_ref, o_ref):
        o_ref[...] = (g_ref[...].astype(F32) + r_ref[...].astype(F32)).astype(o_ref.dtype)

    grid_spec = pltpu.PrefetchScalarGridSpec(
        num_scalar_prefetch=1, grid=(4, rows // tr),
        in_specs=[pl.BlockSpec((None, tr, cols), lambda s, i, c: (2 * s + c[0], i, 0)),
                  pl.BlockSpec((None, tr, cols), lambda s, i, c: (s, i, 0))],
        out_specs=pl.BlockSpec((None, tr, cols), lambda s, i, c: (s, i, 0)))
    return pl.pallas_call(
        body, name=name, grid_spec=grid_spec, out_shape=jax.ShapeDtypeStruct((4, rows, cols), g8.dtype),
        compiler_params=_cp("parallel", "parallel"),
    )(c_idx, g8, r4)


def _tile_rows(rows, cols):
    tr = rows
    while tr * cols > 128 * 2048 and tr % 32 == 0:
        tr //= 2
    return tr


def _adam_math(w, g, m, v):
    m2 = ADAM_B1 * m + (1.0 - ADAM_B1) * g
    v2 = ADAM_B2 * v + (1.0 - ADAM_B2) * (g * g)
    m_hat = m2 / (1.0 - ADAM_B1 ** ADAM_STEP)
    v_hat = v2 / (1.0 - ADAM_B2 ** ADAM_STEP)
    delta = -ADAM_LR * (m_hat / (jnp.sqrt(v_hat) + ADAM_EPS) + ADAM_WD * w)
    return delta, m2, v2


def adam_layer(parts, w, m, v, bufs, layer, name):
    n_parts, rows, cols = parts.shape
    tr = _tile_rows(rows, cols)

    def body(p_ref, w_ref, m_ref, v_ref, g_in, d_in, m_in, v_in, g_out, d_out, m_out, v_out):
        g = p_ref[0].astype(F32)
        for k in range(1, n_parts):
            g = g + p_ref[k].astype(F32)
        delta, m2, v2 = _adam_math(w_ref[...], g, m_ref[...], v_ref[...])
        g_out[...] = g
        d_out[...] = delta
        m_out[...] = m2
        v_out[...] = v2

    lay = pl.BlockSpec((None, tr, cols), lambda i: (layer, i, 0))
    return pl.pallas_call(
        body, name=name, grid=(rows // tr,),
        in_specs=[pl.BlockSpec((n_parts, tr, cols), lambda i: (0, i, 0)), lay, lay, lay, _ANY, _ANY, _ANY, _ANY],
        out_specs=[lay, lay, lay, lay],
        out_shape=[jax.ShapeDtypeStruct(w.shape, F32)] * 4,
        input_output_aliases={4: 0, 5: 1, 6: 2, 7: 3},
        compiler_params=_cp("parallel"),
    )(parts, w, m, v, *bufs)


def matmul(a, b, mode, out_dtype, name, res=None, caps=(1024, 1024, 512), b_view=None, out_slabs=None):
    if mode == "tn":
        k_dim, m_dim = a.shape
    else:
        m_dim, k_dim = a.shape
    if b_view is None:
        b_rows, b_cols = b.shape
    else:
        kind, layer = b_view
        shard_r, shard_c = b.shape[2:]
        b_rows, b_cols = (shard_r, N_DEV * shard_c) if kind == "cols" else (N_DEV * shard_r, shard_c)
    n_dim = b_rows if mode == "nt" else b_cols
    cap_n, cap_k = caps[1], caps[2]
    tn = _tile(out_slabs, cap_n) if out_slabs else _tile(n_dim, cap_n)
    tk = _tile(k_dim, cap_k)
    if b_view is not None:
        along_n = (kind == "cols") == (mode == "nn")
        shard_len = shard_c if kind == "cols" else shard_r
        if along_n:
            tn = _tile(shard_len, cap_n)
        else:
            tk = _tile(shard_len, cap_k)
    tm = _tile(m_dim, caps[0])
    nk = k_dim // tk

    def body(*refs):
        if res is None:
            a_ref, b_ref, o_ref, acc = refs
        else:
            a_ref, b_ref, r_ref, o_ref, acc = refs
        k = pl.program_id(2)

        @pl.when(k == 0)
        def _():
            acc[...] = jnp.zeros_like(acc)

        acc[...] += lax.dot_general(a_ref[...], b_ref[...], _DN[mode], preferred_element_type=F32)

        @pl.when(k == nk - 1)
        def _():
            r = acc[...]
            if res is not None:
                r = r + r_ref[...]
            o_ref[...] = r.astype(out_dtype)

    a_spec = pl.BlockSpec((tk, tm), lambda i, j, k: (k, i)) if mode == "tn" else pl.BlockSpec((tm, tk), lambda i, j, k: (i, k))
    b_blk = (tn, tk) if mode == "nt" else (tk, tn)
    b_pos = (lambda i, j, k: (j, k)) if mode == "nt" else (lambda i, j, k: (k, j))
    if b_view is None:
        b_spec = pl.BlockSpec(b_blk, b_pos)
    elif kind == "cols":
        per = shard_c // b_blk[1]
        b_spec = pl.BlockSpec((None, None) + b_blk, lambda i, j, k: (b_pos(i, j, k)[1] // per, layer, b_pos(i, j, k)[0], b_pos(i, j, k)[1] % per))
    else:
        per = shard_r // b_blk[0]
        b_spec = pl.BlockSpec((None, None) + b_blk, lambda i, j, k: (b_pos(i, j, k)[0] // per, layer, b_pos(i, j, k)[0] % per, b_pos(i, j, k)[1]))
    if out_slabs:
        per_o = out_slabs // tn
        o_spec = pl.BlockSpec((None, tm, tn), lambda i, j, k: (j // per_o, i, j % per_o))
        out_shape = jax.ShapeDtypeStruct((n_dim // out_slabs, m_dim, out_slabs), out_dtype)
    else:
        o_spec = pl.BlockSpec((tm, tn), lambda i, j, k: (i, j))
        out_shape = jax.ShapeDtypeStruct((m_dim, n_dim), out_dtype)
    in_specs, args = [a_spec, b_spec], [a, b]
    if res is not None:
        in_specs.append(o_spec)
        args.append(res)
    return pl.pallas_call(
        body, name=name, grid=(m_dim // tm, n_dim // tn, nk), in_specs=in_specs, out_specs=o_spec,
        out_shape=out_shape,
        scratch_shapes=[pltpu.VMEM((tm, tn), F32)],
        compiler_params=_cp("parallel", "parallel", "arbitrary"),
    )(*args)


def rms_fwd(x, gain, name):
    t, d = x.shape
    tt = _tile(t, 256)

    def body(x_ref, g_ref, o_ref):
        o_ref[...] = _rms(x_ref[...], g_ref[...]).astype(o_ref.dtype)

    return pl.pallas_call(
        body, name=name, grid=(t // tt,),
        in_specs=[pl.BlockSpec((tt, d), lambda i: (i, 0)), pl.BlockSpec((1, d), lambda i: (0, 0))],
        out_specs=pl.BlockSpec((tt, d), lambda i: (i, 0)),
        out_shape=jax.ShapeDtypeStruct((t, d), MXU), compiler_params=_cp("parallel"),
    )(x, gain)


def rms_bwd(x, gain, dh, dres, name):
    t, d = x.shape
    tt = _tile(t, 256)

    def body(x_ref, g_ref, dh_ref, dr_ref, dx_ref, dxb_ref, dg_ref):
        _, vjp = jax.vjp(_rms, x_ref[...], g_ref[...])
        dx, dg = vjp(dh_ref[...])
        dx = dx + dr_ref[...]
        dx_ref[...] = dx
        dxb_ref[...] = dx.astype(dxb_ref.dtype)

        @pl.when(pl.program_id(0) == 0)
        def _():
            dg_ref[...] = jnp.zeros_like(dg_ref)

        dg_ref[...] += dg

    row = pl.BlockSpec((tt, d), lambda i: (i, 0))
    vec = pl.BlockSpec((1, d), lambda i: (0, 0))
    return pl.pallas_call(
        body, name=name, grid=(t // tt,), in_specs=[row, vec, row, row], out_specs=[row, row, vec],
        out_shape=[jax.ShapeDtypeStruct((t, d), F32), jax.ShapeDtypeStruct((t, d), MXU), jax.ShapeDtypeStruct((1, d), F32)],
        compiler_params=_cp("arbitrary"),
    )(x, gain, dh, dres)


def sqrelu_fwd(f, name):
    t, n = f.shape
    tt, tn = _tile(t, 256), _tile(n, 2048)

    def body(f_ref, o_ref):
        r = jnp.maximum(f_ref[...], 0.0)
        o_ref[...] = (r * r).astype(o_ref.dtype)

    blk = pl.BlockSpec((tt, tn), lambda i, j: (i, j))
    return pl.pallas_call(body, name=name, grid=(t // tt, n // tn), in_specs=[blk], out_specs=blk,
                          out_shape=jax.ShapeDtypeStruct((t, n), MXU), compiler_params=_cp("parallel", "parallel"))(f)


def sqrelu_bwd(f, da, name):
    t, n = f.shape
    tt, tn = _tile(t, 256), _tile(n, 2048)

    def body(f_ref, da_ref, o_ref):
        o_ref[...] = (da_ref[...] * (2.0 * jnp.maximum(f_ref[...], 0.0))).astype(o_ref.dtype)

    blk = pl.BlockSpec((tt, tn), lambda i, j: (i, j))
    return pl.pallas_call(body, name=name, grid=(t // tt, n // tn), in_specs=[blk, blk], out_specs=blk,
                          out_shape=jax.ShapeDtypeStruct((t, n), MXU), compiler_params=_cp("parallel", "parallel"))(f, da)


def loss_head(x, target, name):
    t, d = x.shape
    tt = _tile(t, 256)

    def body(x_ref, t_ref, dx_ref, dxb_ref, l_ref):
        e = x_ref[...] - t_ref[...]
        dx = e * (1.0 / d)
        dx_ref[...] = dx
        dxb_ref[...] = dx.astype(dxb_ref.dtype)

        @pl.when(pl.program_id(0) == 0)
        def _():
            l_ref[...] = jnp.zeros_like(l_ref)

        part = jnp.sum(jnp.sum(e * e, axis=-1, keepdims=True) * (1.0 / d), axis=0, keepdims=True)
        l_ref[...] += 0.5 * part

    row = pl.BlockSpec((tt, d), lambda i: (i, 0))
    return pl.pallas_call(
        body, name=name, grid=(t // tt,), in_specs=[row, row],
        out_specs=[row, row, pl.BlockSpec((8, LANE), lambda i: (0, 0))],
        out_shape=[jax.ShapeDtypeStruct((t, d), F32), jax.ShapeDtypeStruct((t, d), MXU), jax.ShapeDtypeStruct((8, LANE), F32)],
        compiler_params=_cp("arbitrary"),
    )(x, target)


def _merge_f(g0, g1, g2, ba, bb, bc):
    return _sigmoid(g0) * ba + _sigmoid(g1) * bb + _sigmoid(g2) * bc


def merge_fwd(z, off, ba, bb, bc, name):
    t, d = ba.shape
    tt, td = _tile(t, 256), _tile(d // 2, 1024)
    nd, ob = d // td, off // td

    def body(g0, g1, g2, a, b, c, o_ref):
        o_ref[...] = _merge_f(g0[...], g1[...], g2[...], a[...], b[...], c[...]).astype(o_ref.dtype)

    gates = [pl.BlockSpec((tt, td), functools.partial(lambda i, j, s: (i, ob + s * nd + j), s=s)) for s in range(3)]
    blk = pl.BlockSpec((tt, td), lambda i, j: (i, j))
    return pl.pallas_call(body, name=name, grid=(t // tt, nd), in_specs=gates + [blk] * 3, out_specs=blk,
                          out_shape=jax.ShapeDtypeStruct((t, d), MXU), compiler_params=_cp("parallel", "parallel"))(z, z, z, ba, bb, bc)


def merge_bwd(z, off, ba, bb, bc, dy, name):
    t, d = ba.shape
    tt, td = _tile(t, 256), _tile(d // 2, 1024)
    nd, ob = d // td, off // td

    def body(g0, g1, g2, a, b, c, dy_ref, dgl, da, db, dc):
        _, vjp = jax.vjp(_merge_f, g0[...], g1[...], g2[...], a[...], b[...], c[...])
        d0, d1, d2, xa, xb, xc = vjp(dy_ref[...])
        for s, dv in enumerate((d0, d1, d2)):
            dgl[s] = dv.astype(dgl.dtype)
        da[...] = xa.astype(da.dtype)
        db[...] = xb.astype(db.dtype)
        dc[...] = xc.astype(dc.dtype)

    gates = [pl.BlockSpec((tt, td), functools.partial(lambda i, j, s: (i, ob + s * nd + j), s=s)) for s in range(3)]
    blk = pl.BlockSpec((tt, td), lambda i, j: (i, j))
    dgl, da, db, dc = pl.pallas_call(
        body, name=name, grid=(t // tt, nd), in_specs=gates + [blk] * 4,
        out_specs=[pl.BlockSpec((3, tt, td), lambda i, j: (0, i, j)), blk, blk, blk],
        out_shape=[jax.ShapeDtypeStruct((3, t, d), MXU)] + [jax.ShapeDtypeStruct((t, d), MXU)] * 3,
        compiler_params=_cp("parallel", "parallel"),
    )(z, z, z, ba, bb, bc, dy)
    return dgl, da, db, dc


def _gdn_pre_f(qp, kp, vp, ab, cq, ck, cv, alog, dtb, h, n_heads):
    def conv(xp, cw):
        acc = xp * cw[CONV_K - 1]
        for j in range(1, CONV_K):
            acc = acc + shift(xp, j) * cw[CONV_K - 1 - j]
        return _silu(acc)

    q, k, v = conv(qp, cq), conv(kp, ck), conv(vp, cv)
    q = q * lax.rsqrt(jnp.sum(q * q, axis=-1, keepdims=True) + EPS) * (HD ** -0.5)
    k = k * lax.rsqrt(jnp.sum(k * k, axis=-1, keepdims=True) + EPS)
    lane = _iota2(ab.shape, 1)
    a_col = jnp.sum(jnp.where(lane == h, ab, 0.0), axis=-1, keepdims=True)
    b_col = jnp.sum(jnp.where(lane == n_heads + h, ab, 0.0), axis=-1, keepdims=True)
    lane1 = _iota2(alog.shape, 1)
    al = jnp.sum(jnp.where(lane1 == h, alog, 0.0), axis=-1, keepdims=True)
    dt = jnp.sum(jnp.where(lane1 == h, dtb, 0.0), axis=-1, keepdims=True)
    g = -jnp.exp(al) * _softplus(a_col + dt)
    return q, k, v, g, _sigmoid(b_col)


def _gdn_pre_specs(t, n_heads, ab_blk):
    zq = [pl.BlockSpec((t, HD), functools.partial(lambda h, s: (0, s * n_heads + h), s=s)) for s in range(3)]
    ab = pl.BlockSpec((t, LANE), lambda h: (0, ab_blk))
    cw = [pl.BlockSpec((CONV_K, HD), functools.partial(lambda h, s: (0, s * n_heads + h), s=s)) for s in range(3)]
    vec = pl.BlockSpec((1, LANE), lambda h: (0, 0))
    return zq, ab, cw, vec


def gdn_pre_fwd(z, ab_blk, conv_w, alog, dtb, n_heads, name):
    t = z.shape[0]
    zq, ab, cw, vec = _gdn_pre_specs(t, n_heads, ab_blk)

    def body(qp, kp, vp, ab_ref, cq, ck, cv, al, dt, q_o, k_o, v_o, g_o, b_o):
        rows = lambda r: tuple(r[j:j + 1, :] for j in range(CONV_K))
        outs = _gdn_pre_f(qp[...], kp[...], vp[...], ab_ref[...], rows(cq), rows(ck), rows(cv), al[...], dt[...],
                          pl.program_id(0), n_heads)
        for o_ref, val in zip((q_o, k_o, v_o, g_o, b_o), outs):
            o_ref[...] = val

    head = pl.BlockSpec((None, t, HD), lambda h: (h, 0, 0))
    col = pl.BlockSpec((None, t, 1), lambda h: (h, 0, 0))
    return pl.pallas_call(
        body, name=name, grid=(n_heads,), in_specs=zq + [ab] + cw + [vec, vec], out_specs=[head] * 3 + [col] * 2,
        out_shape=[jax.ShapeDtypeStruct((n_heads, t, HD), F32)] * 3 + [jax.ShapeDtypeStruct((n_heads, t, 1), F32)] * 2,
        compiler_params=_cp("parallel"),
    )(z, z, z, z, conv_w, conv_w, conv_w, alog, dtb)


def gdn_pre_bwd(z, ab_blk, conv_w, alog, dtb, n_heads, dq, dk, dv, dg, db, name):
    t = z.shape[0]
    gw = n_heads * HD
    zq, ab, cw, vec = _gdn_pre_specs(t, n_heads, ab_blk)

    def body(qp, kp, vp, ab_ref, cq, ck, cv, al, dt, dq_r, dk_r, dv_r, dg_r, db_r,
             dqp, dkp, dvp, dab, dcq, dck, dcv, dal, ddt):
        h = pl.program_id(0)
        rows = lambda r: tuple(r[j:j + 1, :] for j in range(CONV_K))
        f = functools.partial(_gdn_pre_f, h=h, n_heads=n_heads)
        _, vjp = jax.vjp(f, qp[...], kp[...], vp[...], ab_ref[...], rows(cq), rows(ck), rows(cv), al[...], dt[...])
        gq, gk, gv, gab, gcq, gck, gcv, gal, gdt = vjp((dq_r[...], dk_r[...], dv_r[...], dg_r[...], db_r[...]))
        dqp[...] = gq.astype(dqp.dtype)
        dkp[...] = gk.astype(dkp.dtype)
        dvp[...] = gv.astype(dvp.dtype)
        for ref, gr in ((dcq, gcq), (dck, gck), (dcv, gcv)):
            for j in range(CONV_K):
                ref[j:j + 1, :] = gr[j]

        @pl.when(h == 0)
        def _():
            dab[...] = jnp.zeros_like(dab)
            dal[...] = jnp.zeros_like(dal)
            ddt[...] = jnp.zeros_like(ddt)

        dab[...] += gab
        dal[...] += gal
        ddt[...] += gdt

    head = pl.BlockSpec((None, t, HD), lambda h: (h, 0, 0))
    col = pl.BlockSpec((None, t, 1), lambda h: (h, 0, 0))
    seg = pl.BlockSpec((t, HD), lambda h: (0, h))
    cseg = pl.BlockSpec((CONV_K, HD), lambda h: (0, h))
    return pl.pallas_call(
        body, name=name, grid=(n_heads,),
        in_specs=zq + [ab] + cw + [vec, vec] + [head] * 3 + [col] * 2,
        out_specs=[seg] * 3 + [pl.BlockSpec((t, LANE), lambda h: (0, 0))] + [cseg] * 3 + [vec, vec],
        out_shape=[jax.ShapeDtypeStruct((t, gw), MXU)] * 3 + [jax.ShapeDtypeStruct((t, LANE), F32)]
        + [jax.ShapeDtypeStruct((CONV_K, gw), F32)] * 3 + [jax.ShapeDtypeStruct((1, LANE), F32)] * 2,
        compiler_params=_cp("arbitrary"),
    )(z, z, z, z, conv_w, conv_w, conv_w, alog, dtb, dq, dk, dv, dg, db)


def _gdn_chunk_f(q, k, v, g, b):
    c = CHUNK
    r, s = _iota2((c, c), 0), _iota2((c, c), 1)
    tril = (s <= r).astype(F32)
    gc_w = mm(tril, jnp.broadcast_to(g, (c, HD)), "nn", "la")
    gc_i = mm(tril, jnp.broadcast_to(g, (c, c)), "nn", "la")
    gc_j = mm(jnp.ones((c, c), F32), jnp.where(r == s, gc_i, 0.0), "nn", "la")
    decay = jnp.exp(jnp.where(s <= r, gc_i - gc_j, -1e30))
    kb = k * b
    low = jnp.where(s < r, mm(kb, k, "nt", "x3") * decay, 0.0)
    inv = jnp.where(r == s, 1.0, 0.0) - low
    pw = mm(low, low, "nn", "x3")
    n_sq = 1
    while 2 * n_sq < c:
        inv = inv + mm(inv, pw, "nn", "x3")
        n_sq *= 2
        if 2 * n_sq < c:
            pw = mm(pw, pw, "nn", "x3")
    egc = jnp.exp(gc_w)
    u = mm(inv, v * b, "nn", "x3")
    w = mm(inv, kb * egc, "nn", "x3")
    intra = mm(q, k, "nt", "lo") * decay
    g_last = jnp.sum(g, axis=0, keepdims=True)
    kd = k * jnp.exp(g_last - gc_w)
    egl = jnp.exp(jnp.broadcast_to(g_last, (1, HD)))
    return u, w, intra, q * egc, kd, egl


def _group(n, cap=4):
    return max(g for g in range(1, cap + 1) if n % g == 0)


def _chunk_specs(nb_h, nb_c, n_chunks=None):
    cn = (lambda n: n) if n_chunks is None else (lambda n: n_chunks // nb_c - 1 - n)
    rows = nb_c * CHUNK
    vec = pl.BlockSpec((nb_h, rows, HD), lambda h, n: (h, cn(n), 0))
    col = pl.BlockSpec((nb_h, rows, 1), lambda h, n: (h, cn(n), 0))
    sq = pl.BlockSpec((nb_h, rows, CHUNK), lambda h, n: (h, cn(n), 0))
    one = pl.BlockSpec((nb_h, nb_c, 1, HD), lambda h, n: (h, cn(n), 0, 0))
    st = pl.BlockSpec((nb_h, nb_c, HD, HD), lambda h, n: (h, cn(n), 0, 0))
    return vec, col, sq, one, st


def _chunk_shapes(n_heads, t):
    vec = jax.ShapeDtypeStruct((n_heads, t, HD), F32)
    return [vec, vec, jax.ShapeDtypeStruct((n_heads, t, CHUNK), F32), vec, vec,
            jax.ShapeDtypeStruct((n_heads, t // CHUNK, 1, HD), F32)]


def _chunk_rows(ci):
    return slice(ci * CHUNK, (ci + 1) * CHUNK)


def gdn_chunk_fwd(q, k, v, g, b, name):
    n_heads, t, _ = q.shape
    nb_c = _group(t // CHUNK)
    vec, col, sq, one, _ = _chunk_specs(1, nb_c)

    def body(q_r, k_r, v_r, g_r, b_r, *outs):
        for ci in range(nb_c):
            rows = _chunk_rows(ci)
            vals = _gdn_chunk_f(*(r[0, rows, :] for r in (q_r, k_r, v_r, g_r, b_r)))
            for o_ref, val in zip(outs[:5], vals[:5]):
                o_ref[0, rows, :] = val
            outs[5][0, ci] = vals[5]

    return pl.pallas_call(
        body, name=name, grid=(n_heads, t // CHUNK // nb_c), in_specs=[vec] * 3 + [col] * 2,
        out_specs=[vec, vec, sq, vec, vec, one], out_shape=_chunk_shapes(n_heads, t),
        compiler_params=_cp("parallel", "parallel"),
    )(q, k, v, g, b)


def gdn_chunk_bwd(q, k, v, g, b, cts, name):
    n_heads, t, _ = q.shape
    nb_c = _group(t // CHUNK)
    vec, col, sq, one, _ = _chunk_specs(1, nb_c)

    def body(q_r, k_r, v_r, g_r, b_r, du, dw, di, dqd, dkd, degl, dq, dk, dv, dg, db):
        for ci in range(nb_c):
            rows = _chunk_rows(ci)
            _, vjp = jax.vjp(_gdn_chunk_f, *(r[0, rows, :] for r in (q_r, k_r, v_r, g_r, b_r)))
            grads = vjp(tuple(r[0, rows, :] for r in (du, dw, di, dqd, dkd)) + (degl[0, ci],))
            for o_ref, val in zip((dq, dk, dv, dg, db), grads):
                o_ref[0, rows, :] = val

    col_shape = jax.ShapeDtypeStruct((n_heads, t, 1), F32)
    return pl.pallas_call(
        body, name=name, grid=(n_heads, t // CHUNK // nb_c),
        in_specs=[vec] * 3 + [col] * 2 + [vec, vec, sq, vec, vec, one],
        out_specs=[vec] * 3 + [col] * 2,
        out_shape=[jax.ShapeDtypeStruct((n_heads, t, HD), F32)] * 3 + [col_shape] * 2,
        compiler_params=_cp("parallel", "parallel"),
    )(q, k, v, g, b, *cts)


def _scan_f(s, u, w, a, qd, kd, egl):
    vn = u - mm(w, s, "nn", "lo")
    o = mm(qd, s, "nn", "lo") + mm(a, vn, "nn", "lo")
    return o, s * egl + mm(kd, vn, "tn", "lo")


def gdn_scan_fwd(chunks, name):
    u = chunks[0]
    n_heads, t, _ = u.shape
    nc = t // CHUNK
    nb_h = _group(n_heads)
    vec, _, sq, one, st = _chunk_specs(nb_h, 1)

    def body(u_r, w_r, a_r, qd_r, kd_r, e_r, o_ref, s_ref, state):
        @pl.when(pl.program_id(1) == 0)
        def _():
            state[...] = jnp.zeros_like(state)

        for hh in range(nb_h):
            s = state[hh]
            s_ref[hh, 0] = s
            o, s2 = _scan_f(s, u_r[hh], w_r[hh], a_r[hh], qd_r[hh], kd_r[hh], e_r[hh, 0])
            o_ref[hh] = o
            state[hh] = s2

    return pl.pallas_call(
        body, name=name, grid=(n_heads // nb_h, nc), in_specs=[vec, vec, sq, vec, vec, one], out_specs=[vec, st],
        out_shape=[jax.ShapeDtypeStruct((n_heads, t, HD), F32), jax.ShapeDtypeStruct((n_heads, nc, HD, HD), F32)],
        scratch_shapes=[pltpu.VMEM((nb_h, HD, HD), F32)], compiler_params=_cp("parallel", "arbitrary"),
    )(*chunks)


def gdn_scan_bwd(chunks, states, do, name):
    n_heads, t, _ = do.shape
    nc = t // CHUNK
    nb_h = _group(n_heads)
    vec, _, sq, one, st = _chunk_specs(nb_h, 1, n_chunks=nc)

    def body(u_r, w_r, a_r, qd_r, kd_r, e_r, s_r, do_r, du, dw, da, dqd, dkd, de, dstate):
        @pl.when(pl.program_id(1) == 0)
        def _():
            dstate[...] = jnp.zeros_like(dstate)

        for hh in range(nb_h):
            _, vjp = jax.vjp(_scan_f, s_r[hh, 0], u_r[hh], w_r[hh], a_r[hh], qd_r[hh], kd_r[hh], e_r[hh, 0])
            grads = vjp((do_r[hh], dstate[hh]))
            dstate[hh] = grads[0]
            for o_ref, val in zip((du, dw, da, dqd, dkd), grads[1:6]):
                o_ref[hh] = val
            de[hh, 0] = grads[6]

    return pl.pallas_call(
        body, name=name, grid=(n_heads // nb_h, nc), in_specs=[vec, vec, sq, vec, vec, one, st, vec],
        out_specs=[vec, vec, sq, vec, vec, one], out_shape=_chunk_shapes(n_heads, t),
        scratch_shapes=[pltpu.VMEM((nb_h, HD, HD), F32)], compiler_params=_cp("parallel", "arbitrary"),
    )(*chunks, states, do)


def _post_f(o, gate, gain):
    return _rms(o, gain) * _silu(gate)


def gdn_post_fwd(o, z, gate_blk, gain, name):
    n_heads, t, _ = o.shape
    tt = _tile(t, 512)

    def body(o_r, gt_r, gn_r, out):
        out[...] = _post_f(o_r[...], gt_r[...], gn_r[...]).astype(out.dtype)

    return pl.pallas_call(
        body, name=name, grid=(n_heads, t // tt),
        in_specs=[pl.BlockSpec((None, tt, HD), lambda h, i: (h, i, 0)), pl.BlockSpec((tt, HD), lambda h, i: (i, gate_blk + h)),
                  pl.BlockSpec((1, HD), lambda h, i: (0, 0))],
        out_specs=pl.BlockSpec((tt, HD), lambda h, i: (i, h)),
        out_shape=jax.ShapeDtypeStruct((t, n_heads * HD), MXU), compiler_params=_cp("parallel", "parallel"),
    )(o, z, gain)


def gdn_post_bwd(o, z, gate_blk, gain, doa, name):
    n_heads, t, _ = o.shape
    tt = _tile(t, 512)

    def body(o_r, gt_r, gn_r, d_r, do_ref, dgt_ref, dgn_ref):
        _, vjp = jax.vjp(_post_f, o_r[...], gt_r[...], gn_r[...])
        go, ggt, ggn = vjp(d_r[...])
        do_ref[...] = go
        dgt_ref[...] = ggt.astype(dgt_ref.dtype)

        @pl.when((pl.program_id(0) == 0) & (pl.program_id(1) == 0))
        def _():
            dgn_ref[...] = jnp.zeros_like(dgn_ref)

        dgn_ref[...] += ggn

    tok = pl.BlockSpec((tt, HD), lambda h, i: (i, h))
    vec = pl.BlockSpec((1, HD), lambda h, i: (0, 0))
    head = pl.BlockSpec((None, tt, HD), lambda h, i: (h, i, 0))
    return pl.pallas_call(
        body, name=name, grid=(n_heads, t // tt),
        in_specs=[head, pl.BlockSpec((tt, HD), lambda h, i: (i, gate_blk + h)), vec, tok],
        out_specs=[head, tok, vec],
        out_shape=[jax.ShapeDtypeStruct((n_heads, t, HD), F32), jax.ShapeDtypeStruct((t, n_heads * HD), MXU),
                   jax.ShapeDtypeStruct((1, HD), F32)],
        compiler_params=_cp("arbitrary", "arbitrary"),
    )(o, z, gain, doa)


def _gmlp_f(ups, vps, lngs, wss, bcols):
    n_groups = len(ups)
    width = HD * n_groups
    us = [_gelu(a) for a in ups]
    vs = [_gelu(a) for a in vps]
    mu = sum(jnp.sum(a, axis=-1, keepdims=True) for a in vs) * (1.0 / width)
    xcs = [a - mu for a in vs]
    var = sum(jnp.sum(a * a, axis=-1, keepdims=True) for a in xcs) * (1.0 / width)
    rstd = lax.rsqrt(var + EPS)
    r, s = _iota2((HD, HD), 0), _iota2((HD, HD), 1)
    causal = (s // CHUNK) <= (r // CHUNK)
    outs = []
    for gi in range(n_groups):
        vb = xcs[gi] * rstd * lngs[gi]
        sp = mm(jnp.where(causal, wss[gi], 0.0), vb, "nn", "lo") + bcols[gi]
        outs.append(us[gi] * sp)
    return tuple(outs)


def _gmlp_load(uv_u, uv_v, lng, ws, bt, n_groups):
    seg = lambda ref, gi: ref[:, gi * HD:(gi + 1) * HD]
    return ([seg(uv_u, gi) for gi in range(n_groups)], [seg(uv_v, gi) for gi in range(n_groups)],
            [seg(lng, gi) for gi in range(n_groups)], [ws[gi] for gi in range(n_groups)],
            [bt[:, gi:gi + 1] for gi in range(n_groups)])


def _gmlp_specs(width, u_blk, n_groups):
    u = pl.BlockSpec((HD, width), lambda i: (i, u_blk))
    v = pl.BlockSpec((HD, width), lambda i: (i, u_blk + 1))
    lng = pl.BlockSpec((1, width), lambda i: (0, 0))
    ws = pl.BlockSpec((n_groups, HD, HD), lambda i: (0, 0, 0))
    bt = pl.BlockSpec((HD, LANE), lambda i: (0, 0))
    return u, v, lng, ws, bt


def gmlp_fwd(z, uv_off, width, lng, ws, bt, name):
    t = z.shape[0]
    n_groups = width // HD
    specs = _gmlp_specs(width, uv_off // width, n_groups)

    def body(u_r, v_r, l_r, w_r, b_r, out):
        outs = _gmlp_f(*_gmlp_load(u_r, v_r, l_r, w_r, b_r, n_groups))
        for gi in range(n_groups):
            out[:, gi * HD:(gi + 1) * HD] = outs[gi].astype(out.dtype)

    return pl.pallas_call(
        body, name=name, grid=(t // HD,), in_specs=list(specs), out_specs=pl.BlockSpec((HD, width), lambda i: (i, 0)),
        out_shape=jax.ShapeDtypeStruct((t, width), MXU), compiler_params=_cp("parallel"),
    )(z, z, lng, ws, bt)


def gmlp_bwd(z, uv_off, width, lng, ws, bt, dob, name):
    t = z.shape[0]
    n_groups = width // HD
    specs = _gmlp_specs(width, uv_off // width, n_groups)

    def body(u_r, v_r, l_r, w_r, b_r, d_r, duv, dl, dws, dbt):
        _, vjp = jax.vjp(_gmlp_f, *_gmlp_load(u_r, v_r, l_r, w_r, b_r, n_groups))
        gu, gv, gl, gw, gb = vjp(tuple(d_r[:, gi * HD:(gi + 1) * HD] for gi in range(n_groups)))

        @pl.when(pl.program_id(0) == 0)
        def _():
            dl[...] = jnp.zeros_like(dl)
            dws[...] = jnp.zeros_like(dws)
            dbt[...] = jnp.zeros_like(dbt)

        for gi in range(n_groups):
            duv[:, gi * HD:(gi + 1) * HD] = gu[gi].astype(duv.dtype)
            duv[:, width + gi * HD:width + (gi + 1) * HD] = gv[gi].astype(duv.dtype)
            dl[:, gi * HD:(gi + 1) * HD] += gl[gi]
            dws[gi] += gw[gi]
            dbt[:, gi:gi + 1] += gb[gi]

    return pl.pallas_call(
        body, name=name, grid=(t // HD,), in_specs=list(specs) + [pl.BlockSpec((HD, width), lambda i: (i, 0))],
        out_specs=[pl.BlockSpec((HD, 2 * width), lambda i: (i, 0)), specs[2], specs[3], specs[4]],
        out_shape=[jax.ShapeDtypeStruct((t, 2 * width), MXU), jax.ShapeDtypeStruct((1, width), F32),
                   jax.ShapeDtypeStruct((n_groups, HD, HD), F32), jax.ShapeDtypeStruct((HD, LANE), F32)],
        compiler_params=_cp("arbitrary"),
    )(z, z, lng, ws, bt, dob)


def _sba_pre_f(qp, kp, qg, kg):
    return _rms(qp, qg), _rms(kp, kg)


def sba_pre_fwd(z, c_blk, n_heads, qg, kg, name):
    t = z.shape[0]
    tt = _tile(t, 512)
    zs = [pl.BlockSpec((tt, HD), functools.partial(lambda h, i, s: (i, c_blk + s * n_heads + h), s=s)) for s in range(3)]
    vec = pl.BlockSpec((1, HD), lambda h, i: (0, 0))
    head = pl.BlockSpec((None, tt, HD), lambda h, i: (h, i, 0))

    def body(qp, kp, vp, qg_r, kg_r, q_o, k_o, v_o):
        q, k = _sba_pre_f(qp[...], kp[...], qg_r[...], kg_r[...])
        q_o[...] = q.astype(q_o.dtype)
        k_o[...] = k.astype(k_o.dtype)
        v_o[...] = vp[...].astype(v_o.dtype)

    return pl.pallas_call(
        body, name=name, grid=(n_heads, t // tt), in_specs=zs + [vec, vec], out_specs=[head] * 3,
        out_shape=[jax.ShapeDtypeStruct((n_heads, t, HD), MXU)] * 3, compiler_params=_cp("parallel", "parallel"),
    )(z, z, z, qg, kg)


def sba_pre_bwd(z, c_blk, n_heads, qg, kg, dq, dk, dv, name):
    t = z.shape[0]
    tt = _tile(t, 512)
    zs = [pl.BlockSpec((tt, HD), functools.partial(lambda h, i, s: (i, c_blk + s * n_heads + h), s=s)) for s in range(2)]
    vec = pl.BlockSpec((1, HD), lambda h, i: (0, 0))
    head = pl.BlockSpec((None, tt, HD), lambda h, i: (h, i, 0))
    tok = pl.BlockSpec((tt, HD), lambda h, i: (i, h))

    def body(qp, kp, qg_r, kg_r, dq_r, dk_r, dv_r, dqp, dkp, dvp, dqg, dkg):
        _, vjp = jax.vjp(_sba_pre_f, qp[...], kp[...], qg_r[...], kg_r[...])
        gq, gk, gqg, gkg = vjp((dq_r[...], dk_r[...]))
        dqp[...] = gq.astype(dqp.dtype)
        dkp[...] = gk.astype(dkp.dtype)
        dvp[...] = dv_r[...].astype(dvp.dtype)

        @pl.when((pl.program_id(0) == 0) & (pl.program_id(1) == 0))
        def _():
            dqg[...] = jnp.zeros_like(dqg)
            dkg[...] = jnp.zeros_like(dkg)

        dqg[...] += gqg
        dkg[...] += gkg

    return pl.pallas_call(
        body, name=name, grid=(n_heads, t // tt), in_specs=zs + [vec, vec] + [head] * 3,
        out_specs=[tok] * 3 + [vec, vec],
        out_shape=[jax.ShapeDtypeStruct((t, n_heads * HD), MXU)] * 3 + [jax.ShapeDtypeStruct((1, HD), F32)] * 2,
        compiler_params=_cp("arbitrary", "arbitrary"),
    )(z, z, qg, kg, dq, dk, dv)


def _sba_block(q, kj, i, j):
    zz = lax.dot_general(q, kj, _DN["nt"], preferred_element_type=F32) * (HD ** -0.5)
    ls = _logsig(zz)
    strict = (j * HD + _iota2((HD, HD), 1)) < (i * HD + _iota2((HD, HD), 0))
    return zz, ls, jnp.where(strict, ls - zz, 0.0), strict


def sba_fwd(q, k, v, name):
    n_heads, t, _ = q.shape
    nb_h = _group(n_heads)

    def body(q_r, k_r, v_r, o_ref, tot_ref):
        i = pl.program_id(1)
        after = (_iota2((HD, HD), 0) > _iota2((HD, HD), 1)).astype(F32)

        def step(it, carry):
            j = i - it
            rows = pl.ds(pl.multiple_of(j * HD, HD), HD)
            new = []
            for hh in range(nb_h):
                acc, cs = carry[hh]
                _, ls, lk, strict = _sba_block(q_r[hh], k_r[hh, rows, :], i, j)
                suffix = _mm_raw(lk, after, "nn", "rb") + cs
                att = jnp.where(strict, jnp.exp(ls + suffix), 0.0)
                acc = acc + _mm_raw(att, v_r[hh, rows, :], "nn", "lo")
                new.append((acc, cs + jnp.sum(lk, axis=-1, keepdims=True)))
            return tuple(new)

        init = tuple((jnp.zeros((HD, HD), F32), jnp.zeros((HD, 1), F32)) for _ in range(nb_h))
        final = lax.fori_loop(0, i + 1, step, init)
        for hh in range(nb_h):
            o_ref[:, hh * HD:(hh + 1) * HD] = final[hh][0].astype(o_ref.dtype)
            tot_ref[hh] = final[hh][1]

    full = pl.BlockSpec((nb_h, t, HD), lambda h, i: (h, 0, 0))
    return pl.pallas_call(
        body, name=name, grid=(n_heads // nb_h, t // HD),
        in_specs=[pl.BlockSpec((nb_h, HD, HD), lambda h, i: (h, i, 0)), full, full],
        out_specs=[pl.BlockSpec((HD, nb_h * HD), lambda h, i: (i, h)), pl.BlockSpec((nb_h, HD, 1), lambda h, i: (h, i, 0))],
        out_shape=[jax.ShapeDtypeStruct((t, n_heads * HD), MXU), jax.ShapeDtypeStruct((n_heads, t, 1), F32)],
        compiler_params=_cp("parallel", "parallel"),
    )(q, k, v)


def sba_bwd(q, k, v, tot, do, name):
    n_heads, t, _ = q.shape
    nb_h = _group(n_heads)

    def body(q_r, k_r, v_r, tot_r, do_r, dq_ref, dk_ref, dv_ref):
        i = pl.program_id(1)

        @pl.when(i == 0)
        def _():
            dk_ref[...] = jnp.zeros_like(dk_ref)
            dv_ref[...] = jnp.zeros_like(dv_ref)

        r, s = _iota2((HD, HD), 0), _iota2((HD, HD), 1)
        upto = (r <= s).astype(F32)
        before = (r < s).astype(F32)

        def step(j, carry):
            rows = pl.ds(pl.multiple_of(j * HD, HD), HD)
            new = []
            for hh in range(nb_h):
                dq, cp, cd = carry[hh]
                qb, dob = q_r[hh], do_r[:, hh * HD:(hh + 1) * HD]
                kj, vj = k_r[hh, rows, :], v_r[hh, rows, :]
                _, ls, lk, strict = _sba_block(qb, kj, i, j)
                sig = jnp.exp(ls)
                suffix = tot_r[hh] - (cp + _mm_raw(lk, upto, "nn", "rb"))
                att = jnp.where(strict, jnp.exp(ls + suffix), 0.0)
                dp = _mm_raw(dob, vj, "nt", "lo") * att
                dlk = cd + _mm_raw(dp, before, "nn", "rb")
                dz = jnp.where(strict, dp * (1.0 - sig) - dlk * sig, 0.0) * (HD ** -0.5)
                dk_ref[hh, rows, :] += _mm_raw(dz, qb, "tn", "lo")
                dv_ref[hh, rows, :] += _mm_raw(att, dob, "tn", "lo")
                new.append((dq + _mm_raw(dz, kj, "nn", "lo"), cp + jnp.sum(lk, axis=-1, keepdims=True),
                            cd + jnp.sum(dp, axis=-1, keepdims=True)))
            return tuple(new)

        zero_col = jnp.zeros((HD, 1), F32)
        final = lax.fori_loop(0, i + 1, step, tuple((jnp.zeros((HD, HD), F32), zero_col, zero_col) for _ in range(nb_h)))
        for hh in range(nb_h):
            dq_ref[hh] = final[hh][0]

    full = pl.BlockSpec((nb_h, t, HD), lambda h, i: (h, 0, 0))
    blk = pl.BlockSpec((nb_h, HD, HD), lambda h, i: (h, i, 0))
    return pl.pallas_call(
        body, name=name, grid=(n_heads // nb_h, t // HD),
        in_specs=[blk, full, full, pl.BlockSpec((nb_h, HD, 1), lambda h, i: (h, i, 0)),
                  pl.BlockSpec((HD, nb_h * HD), lambda h, i: (i, h))],
        out_specs=[blk, full, full], out_shape=[jax.ShapeDtypeStruct((n_heads, t, HD), F32)] * 3,
        compiler_params=_cp("parallel", "arbitrary"),
    )(q, k, v, tot, do)


def small_adam(parts, w, m, v, name):
    n_parts, rows, _ = parts.shape
    tr = _tile(rows, 512) if rows % LANE == 0 else rows

    def body(p_ref, w_ref, m_ref, v_ref, g_out, d_out, m_out, v_out):
        g = p_ref[0]
        for k in range(1, n_parts):
            g = g + p_ref[k]
        delta, m2, v2 = _adam_math(w_ref[...], g, m_ref[...], v_ref[...])
        g_out[...] = g
        d_out[...] = delta
        m_out[...] = m2
        v_out[...] = v2

    blk = pl.BlockSpec((tr, LANE), lambda i: (i, 0))
    return pl.pallas_call(
        body, name=name, grid=(rows // tr,),
        in_specs=[pl.BlockSpec((n_parts, tr, LANE), lambda i: (0, i, 0)), blk, blk, blk], out_specs=[blk] * 4,
        out_shape=[jax.ShapeDtypeStruct((rows, LANE), F32)] * 4, compiler_params=_cp("parallel"),
    )(parts, w, m, v)


def _pack(arrays):
    flat = jnp.concatenate([a.reshape(-1).astype(F32) for a in arrays])
    pad = (-flat.shape[0]) % (8 * LANE)
    return jnp.pad(flat, (0, pad)).reshape(-1, LANE)


def _unpack(packed, shapes):
    flat, outs, pos = packed.reshape(-1), [], 0
    for shp in shapes:
        n = 1
        for s in shp:
            n *= s
        outs.append(flat[pos:pos + n].reshape(shp))
        pos += n
    return outs


def _pad_lanes(a):
    return jnp.pad(a, ((0, 0), (0, LANE - a.shape[1])))


def kernel(x, w_in, conv_w, a_log, dt_bias, gdn_norm_g, gmlp_ln_g, w_spatial, b_spatial, sba_q_g, sba_k_g, w_out_a, w_out_b, w_out_c, w_out, norm_mix_g, norm_mlp_g, w_ff1, w_ff2, loss_target, m_w_in, m_conv_w, m_a_log, m_dt_bias, m_gdn_norm_g, m_gmlp_ln_g, m_w_spatial, m_b_spatial, m_sba_q_g, m_sba_k_g, m_w_out_a, m_w_out_b, m_w_out_c, m_w_out, m_norm_mix_g, m_norm_mlp_g, m_w_ff1, m_w_ff2, v_w_in, v_conv_w, v_a_log, v_dt_bias, v_gdn_norm_g, v_gmlp_ln_g, v_w_spatial, v_b_spatial, v_sba_q_g, v_sba_k_g, v_w_out_a, v_w_out_b, v_w_out_c, v_w_out, v_norm_mix_g, v_norm_mlp_g, v_w_ff1, v_w_ff2):
    depth = w_in.shape[0]
    _, t, d = x.shape
    n_heads = d // 256
    gw = n_heads * HD
    width = d // 2
    n_groups = width // HD
    off_gate, off_uv, off_c, off_gl = 3 * gw, 4 * gw, 4 * gw + 2 * width, 7 * gw + 2 * width
    off_ab = off_gl + 3 * d
    n_packed = off_ab + LANE
    n_in = off_ab + 2 * n_heads
    assert w_in.shape[2] * N_DEV == n_in and t % LANE == 0 and d % 256 == 0

    ix, iy, ic = lax.axis_index("x"), lax.axis_index("y"), lax.axis_index("c")
    dev = 4 * ix + 2 * iy + ic
    c_idx = jnp.reshape(ic, (1,)).astype(jnp.int32)
    xs, target = x[0], loss_target[0]

    g_in = all_gather(w_in.astype(MXU), "ag_w_in")
    get_in = lambda l: jnp.transpose(g_in[:, l], (1, 0, 2)).reshape(w_in.shape[1], N_DEV * w_in.shape[2])
    g_oa, g_ob, g_oc = (all_gather(w.astype(MXU), nm) for w, nm in
                        ((w_out_a, "ag_w_out_a"), (w_out_b, "ag_w_out_b"), (w_out_c, "ag_w_out_c")))
    g_out, g_ff1, g_ff2 = (all_gather(w.astype(MXU), nm) for w, nm in
                           ((w_out, "ag_w_out"), (w_ff1, "ag_w_ff1"), (w_ff2, "ag_w_ff2")))
    conv_all = all_gather(conv_w, "ag_conv_w")
    conv_full = jnp.transpose(conv_all, (1, 2, 0, 3)).reshape(depth, CONV_K, 3 * gw)

    def pack_in(w):
        return jnp.concatenate([w[:, :3 * gw], w[:, 3 * gw + 2 * n_heads:], w[:, 3 * gw:3 * gw + 2 * n_heads],
                                jnp.zeros((w.shape[0], LANE - 2 * n_heads), w.dtype)], axis=1)

    def unpack_in(wp):
        return jnp.concatenate([wp[:, :3 * gw], wp[:, off_ab:off_ab + 2 * n_heads], wp[:, 3 * gw:off_ab]], axis=1)

    alog_p, dtb_p = _pad_lanes(a_log), _pad_lanes(dt_bias)
    bt_all = jnp.pad(jnp.transpose(b_spatial, (0, 2, 1)), ((0, 0), (0, 0), (0, LANE - n_groups)))

    saved = []
    cur = xs
    for l in range(depth):
        lw = dict(w_in=pack_in(get_in(l)), conv=conv_full[l], alog=alog_p[l:l + 1], dtb=dtb_p[l:l + 1],
                  gng=gdn_norm_g[l:l + 1], lng=gmlp_ln_g[l:l + 1], ws=w_spatial[l], bt=bt_all[l],
                  qg=sba_q_g[l:l + 1], kg=sba_k_g[l:l + 1], gmix=norm_mix_g[l:l + 1], gmlp=norm_mlp_g[l:l + 1])
        s = dict(lw=lw, x=cur)
        s["h1"] = rms_fwd(cur, lw["gmix"], "rms_mix")
        z = s["z"] = matmul(s["h1"], lw["w_in"], "nn", F32, "mm_in", caps=(1024, 1408, 512))
        s["pre"] = gdn_pre_fwd(z, off_ab // LANE, lw["conv"], lw["alog"], lw["dtb"], n_heads, "gdn_pre")
        s["chunks"] = gdn_chunk_fwd(*s["pre"], "gdn_chunk")
        s["o"], s["states"] = gdn_scan_fwd(s["chunks"], "gdn_scan")
        s["oa"] = gdn_post_fwd(s["o"], z, off_gate // HD, lw["gng"], "gdn_post")
        s["ob"] = gmlp_fwd(z, off_uv, width, lw["lng"], lw["ws"], lw["bt"], "gmlp")
        s["qkv_c"] = sba_pre_fwd(z, off_c // HD, n_heads, lw["qg"], lw["kg"], "sba_pre")
        s["oc"], s["tot"] = sba_fwd(*s["qkv_c"], "sba")
        s["ba"] = matmul(s["oa"], g_oa, "nn", F32, "mm_oa", b_view=("cols", l))
        s["bb"] = matmul(s["ob"], g_ob, "nn", F32, "mm_ob", b_view=("cols", l))
        s["bc"] = matmul(s["oc"], g_oc, "nn", F32, "mm_oc", b_view=("cols", l))
        s["y"] = merge_fwd(z, off_gl, s["ba"], s["bb"], s["bc"], "merge")
        s["x1"] = matmul(s["y"], g_out, "nn", F32, "mm_out", res=cur, b_view=("rows", l))
        s["h2"] = rms_fwd(s["x1"], lw["gmlp"], "rms_mlp")
        s["f"] = matmul(s["h2"], g_ff1, "nn", F32, "mm_ff1", b_view=("cols", l))
        s["a"] = sqrelu_fwd(s["f"], "sqrelu")
        cur = matmul(s["a"], g_ff2, "nn", F32, "mm_ff2", res=s["x1"], b_view=("rows", l))
        saved.append(s)

    dx, dxb, loss_tile = loss_head(cur, target, "loss_head")
    loss = lax.psum(loss_tile[0, 0], AXES)

    big = dict(w_in=(w_in, m_w_in, v_w_in), w_out_a=(w_out_a, m_w_out_a, v_w_out_a), w_out_b=(w_out_b, m_w_out_b, v_w_out_b),
               w_out_c=(w_out_c, m_w_out_c, v_w_out_c), w_out=(w_out, m_w_out, v_w_out), w_ff1=(w_ff1, m_w_ff1, v_w_ff1),
               w_ff2=(w_ff2, m_w_ff2, v_w_ff2))
    bufs = {nm: tuple(lax.empty(w.shape, F32) for _ in range(4)) for nm, (w, _, _) in big.items()}

    def reduce_update(nm, g8, l):
        w, m, v = big[nm]
        g8 = g8.reshape(N_DEV, w.shape[1], w.shape[2])
        r4 = rs_pair_exchange(g8, "rs_pair_" + nm)
        p4 = pair_sum(g8, r4, c_idx, "rs_pair_sum_" + nm)
        parts = rs_chip_exchange(p4, "rs_chip_" + nm)
        bufs[nm] = tuple(adam_layer(parts, w, m, v, bufs[nm], l, "adam_" + nm))

    small_grads = []
    for l in reversed(range(depth)):
        s = saved[l]
        lw, z = s["lw"], s["z"]
        da = matmul(dxb, g_ff2, "nt", F32, "mm_ff2_dx", b_view=("rows", l))
        reduce_update("w_ff2", matmul(s["a"], dxb, "tn", MXU, "mm_ff2_dw"), l)
        df = sqrelu_bwd(s["f"], da, "sqrelu_bwd")
        dh2 = matmul(df, g_ff1, "nt", F32, "mm_ff1_dx", b_view=("cols", l))
        reduce_update("w_ff1", matmul(s["h2"], df, "tn", MXU, "mm_ff1_dw", out_slabs=w_ff1.shape[2]), l)
        dx1, dx1b, d_gmlp = rms_bwd(s["x1"], lw["gmlp"], dh2, dx, "rms_mlp_bwd")
        dy = matmul(dx1b, g_out, "nt", F32, "mm_out_dx", b_view=("rows", l))
        reduce_update("w_out", matmul(s["y"], dx1b, "tn", MXU, "mm_out_dw"), l)
        dgl, dba, dbb, dbc = merge_bwd(z, off_gl, s["ba"], s["bb"], s["bc"], dy, "merge_bwd")
        doa = matmul(dba, g_oa, "nt", F32, "mm_oa_dx", b_view=("cols", l))
        dob = matmul(dbb, g_ob, "nt", F32, "mm_ob_dx", b_view=("cols", l))
        doc = matmul(dbc, g_oc, "nt", F32, "mm_oc_dx", b_view=("cols", l))
        slab = w_out_a.shape[2]
        reduce_update("w_out_a", matmul(s["oa"], dba, "tn", MXU, "mm_oa_dw", out_slabs=slab), l)
        reduce_update("w_out_b", matmul(s["ob"], dbb, "tn", MXU, "mm_ob_dw", out_slabs=slab), l)
        reduce_update("w_out_c", matmul(s["oc"], dbc, "tn", MXU, "mm_oc_dw", out_slabs=slab), l)
        dqc, dkc, dvc = sba_bwd(*s["qkv_c"], s["tot"], doc, "sba_bwd")
        dz_qc, dz_kc, dz_vc, d_qg, d_kg = sba_pre_bwd(z, off_c // HD, n_heads, lw["qg"], lw["kg"], dqc, dkc, dvc, "sba_pre_bwd")
        dz_uv, d_lng, d_ws, d_bt = gmlp_bwd(z, off_uv, width, lw["lng"], lw["ws"], lw["bt"], dob, "gmlp_bwd")
        do, dz_gate, d_gng = gdn_post_bwd(s["o"], z, off_gate // HD, lw["gng"], doa, "gdn_post_bwd")
        chunk_cts = gdn_scan_bwd(s["chunks"], s["states"], do, "gdn_scan_bwd")
        dqa, dka, dva, dga, dba_ = gdn_chunk_bwd(*s["pre"], chunk_cts, "gdn_chunk_bwd")
        dz_q, dz_k, dz_v, d_ab, d_cq, d_ck, d_cv, d_alog, d_dtb = gdn_pre_bwd(
            z, off_ab // LANE, lw["conv"], lw["alog"], lw["dtb"], n_heads, dqa, dka, dva, dga, dba_, "gdn_pre_bwd")
        dz = jnp.concatenate([dz_q, dz_k, dz_v, dz_gate, dz_uv, dz_qc, dz_kc, dz_vc, dgl[0], dgl[1], dgl[2],
                              d_ab.astype(MXU)], axis=1)
        dh1 = matmul(dz, lw["w_in"], "nt", F32, "mm_in_dx", caps=(1024, 1024, 1408))
        dw_in = unpack_in(matmul(s["h1"], dz, "tn", MXU, "mm_in_dw", caps=(1024, 1408, 512)))
        reduce_update("w_in", jnp.transpose(dw_in.reshape(d, N_DEV, w_in.shape[2]), (1, 0, 2)), l)
        dx, dxb, d_gmix = rms_bwd(s["x"], lw["gmix"], dh1, dx1, "rms_mix_bwd")
        small_grads.append(dict(
            conv_w=jnp.concatenate([d_cq, d_ck, d_cv], axis=1), a_log=d_alog[0, :n_heads], dt_bias=d_dtb[0, :n_heads],
            gdn_norm_g=d_gng[0], gmlp_ln_g=d_lng[0], w_spatial=d_ws, b_spatial=jnp.transpose(d_bt[:, :n_groups]),
            sba_q_g=d_qg[0], sba_k_g=d_kg[0], norm_mix_g=d_gmix[0], norm_mlp_g=d_gmlp[0]))
    small_grads = small_grads[::-1]

    rep_names = ["a_log", "dt_bias", "gdn_norm_g", "gmlp_ln_g", "w_spatial", "b_spatial", "sba_q_g", "sba_k_g",
                 "norm_mix_g", "norm_mlp_g"]
    rep = dict(a_log=(a_log, m_a_log, v_a_log), dt_bias=(dt_bias, m_dt_bias, v_dt_bias),
               gdn_norm_g=(gdn_norm_g, m_gdn_norm_g, v_gdn_norm_g), gmlp_ln_g=(gmlp_ln_g, m_gmlp_ln_g, v_gmlp_ln_g),
               w_spatial=(w_spatial, m_w_spatial, v_w_spatial), b_spatial=(b_spatial, m_b_spatial, v_b_spatial),
               sba_q_g=(sba_q_g, m_sba_q_g, v_sba_q_g), sba_k_g=(sba_k_g, m_sba_k_g, v_sba_k_g),
               norm_mix_g=(norm_mix_g, m_norm_mix_g, v_norm_mix_g), norm_mlp_g=(norm_mlp_g, m_norm_mlp_g, v_norm_mlp_g))
    stack = lambda nm: jnp.stack([sg[nm] for sg in small_grads])
    conv_cols = conv_w.shape[2]
    conv_pad = jnp.zeros((depth, CONV_K, 3 * gw - conv_cols), F32)
    widen = lambda a: jnp.concatenate([a, conv_pad], axis=2)
    grads_packed = _pack([stack(nm) for nm in rep_names] + [stack("conv_w")])
    gathered = all_gather(grads_packed, "ag_small_grads")
    packed = [_pack([rep[nm][k] for nm in rep_names] + [widen((conv_w, m_conv_w, v_conv_w)[k])]) for k in range(3)]
    shapes = [rep[nm][0].shape for nm in rep_names] + [(depth, CONV_K, 3 * gw)]
    flat = gathered.reshape(N_DEV, -1)
    n_rep = sum(int(rep[nm][0].size) for nm in rep_names)
    conv_part = flat[:, n_rep:n_rep + depth * CONV_K * 3 * gw].reshape(N_DEV, depth, CONV_K, 3 * gw)
    conv_mine = lax.dynamic_slice_in_dim(conv_part, dev * conv_cols, conv_cols, axis=3)
    conv_mine = jnp.concatenate([conv_mine, jnp.zeros((N_DEV, depth, CONV_K, 3 * gw - conv_cols), F32)], axis=3)
    tail = flat[:, n_rep + depth * CONV_K * 3 * gw:]
    parts = jnp.concatenate([flat[:, :n_rep], conv_mine.reshape(N_DEV, -1), tail], axis=1).reshape(gathered.shape)
    outs_small = small_adam(parts, packed[0], packed[1], packed[2], "adam_small")
    small = [dict(zip(rep_names + ["conv_w"], _unpack(o, shapes))) for o in outs_small]
    for sm in small:
        sm["conv_w"] = sm["conv_w"][:, :, :conv_cols]

    order = ["w_in", "conv_w", "a_log", "dt_bias", "gdn_norm_g", "gmlp_ln_g", "w_spatial", "b_spatial", "sba_q_g",
             "sba_k_g", "w_out_a", "w_out_b", "w_out_c", "w_out", "norm_mix_g", "norm_mlp_g", "w_ff1", "w_ff2"]
    result = [loss, dx[None]]
    for kind in range(4):
        for nm in order:
            result.append(bufs[nm][kind] if nm in bufs else small[kind][nm])
    return tuple(result)
```

```python
import functools

import jax
import jax.numpy as jnp
from jax import lax
from jax.experimental import pallas as pl
from jax.experimental.pallas import tpu as pltpu

F32 = jnp.float32
MXU = jnp.bfloat16
N_DEV = 8
AXES = ("x", "y", "c")
CHUNK = 64
HD = 128
CONV_K = 4
EPS = 1e-6
LANE = 128
VMEM_LIMIT = 56 * 1024 * 1024
ADAM_LR, ADAM_B1, ADAM_B2, ADAM_EPS, ADAM_WD, ADAM_STEP = 0.001, 0.9, 0.999, 1e-08, 0.01, 10

_ANY = pl.BlockSpec(memory_space=pl.ANY)
_MESH = pl.DeviceIdType.MESH
_DN = {"nn": (((1,), (0,)), ((), ())), "nt": (((1,), (1,)), ((), ())), "tn": (((0,), (0,)), ((), ()))}


def _cp(*sem):
    return pltpu.CompilerParams(dimension_semantics=sem, vmem_limit_bytes=VMEM_LIMIT)


def _tile(n, cap):
    if n <= cap:
        return n
    best = LANE
    for t in range(LANE, cap + 1, LANE):
        if n % t == 0:
            best = t
    assert n % best == 0, (n, cap)
    return best


def _split(x, n):
    parts, rest = [], x.astype(F32)
    for _ in range(n):
        p = rest.astype(MXU)
        parts.append(p)
        rest = rest - p.astype(F32)
    return parts


def _mm_raw(a, b, mode, prec):
    dot = lambda p, q: lax.dot_general(p, q, _DN[mode], preferred_element_type=F32)
    if prec == "lo" or MXU == F32:
        return dot(a.astype(MXU), b.astype(MXU))
    if prec == "x3":
        (ah, al), (bh, bl) = _split(a, 2), _split(b, 2)
        return dot(ah, bh) + (dot(al, bh) + dot(ah, bl))
    if prec == "la":
        ae, (bh, bl) = a.astype(MXU), _split(b, 2)
        return dot(ae, bh) + dot(ae, bl)
    assert prec == "rb"
    (ah, al), be = _split(a, 2), b.astype(MXU)
    return dot(ah, be) + dot(al, be)


@functools.partial(jax.custom_vjp, nondiff_argnums=(2, 3))
def mm(a, b, mode, prec):
    return _mm_raw(a, b, mode, prec)


def _mm_fwd(a, b, mode, prec):
    return _mm_raw(a, b, mode, prec), (a, b)


def _mm_bwd(mode, prec, res, ct):
    a, b = res
    pa = {"la": None, "rb": "rb"}.get(prec, prec)
    pb = {"la": "la", "rb": None}.get(prec, prec)
    if mode == "nn":
        da = _mm_raw(ct, b, "nt", pa) if pa else None
        db = _mm_raw(a, ct, "tn", pb) if pb else None
    elif mode == "nt":
        da = _mm_raw(ct, b, "nn", pa) if pa else None
        db = _mm_raw(ct, a, "tn", {"la": "rb"}.get(pb, pb)) if pb else None
    else:
        da = _mm_raw(b, ct, "nt", {"rb": "la"}.get(pa, pa)) if pa else None
        db = _mm_raw(a, ct, "nn", pb) if pb else None
    da = jnp.zeros_like(a) if da is None else da.astype(a.dtype)
    db = jnp.zeros_like(b) if db is None else db.astype(b.dtype)
    return da, db


mm.defvjp(_mm_fwd, _mm_bwd)


def _shift_rows(x, j):
    n = x.shape[0]
    row = lax.broadcasted_iota(jnp.int32, x.shape, 0)
    if j > 0:
        return jnp.where(row >= j, pltpu.roll(x, j, 0), 0.0)
    return jnp.where(row < n + j, pltpu.roll(x, n + j, 0), 0.0)


@functools.partial(jax.custom_vjp, nondiff_argnums=(1,))
def shift(x, j):
    return _shift_rows(x, j)


shift.defvjp(lambda x, j: (_shift_rows(x, j), None), lambda j, _, ct: (_shift_rows(ct, -j),))


def _sigmoid(x):
    return 1.0 / (1.0 + jnp.exp(-x))


def _silu(x):
    return x * _sigmoid(x)


def _softplus(x):
    return jnp.maximum(x, 0.0) + jnp.log(1.0 + jnp.exp(-jnp.abs(x)))


def _logsig(x):
    return jnp.minimum(x, 0.0) - jnp.log(1.0 + jnp.exp(-jnp.abs(x)))


def _gelu(x):
    return 0.5 * x * (1.0 + lax.erf(x * (2.0 ** -0.5)))


def _rms(x, g):
    return x * lax.rsqrt(jnp.mean(x * x, axis=-1, keepdims=True) + EPS) * g


def _iota2(shape, dim):
    return lax.broadcasted_iota(jnp.int32, shape, dim)


class GatherJob:
    def __init__(self, x):
        self.inputs = [x]
        self.out_shapes = [jax.ShapeDtypeStruct((N_DEV,) + x.shape, x.dtype)]
        self.sems = [pltpu.SemaphoreType.DMA((7,)), pltpu.SemaphoreType.DMA((7,)), pltpu.SemaphoreType.DMA(())]
        self.results = None

    @staticmethod
    def _plan(ins, outs, sems):
        (x_ref,), (out_ref,), (send_sems, recv_sems, local_sem) = ins, outs, sems
        ix, iy, ic = lax.axis_index("x"), lax.axis_index("y"), lax.axis_index("c")
        me, sibling = (ix, iy, ic), (ix, iy, 1 - ic)
        chips = [(1 - ix, iy), (ix, 1 - iy), (1 - ix, 1 - iy)]

        def slot(px, py, pc):
            return out_ref.at[4 * px + 2 * py + pc]

        def copy(k, block, to, src=None):
            return pltpu.make_async_remote_copy(
                src_ref=slot(*block) if src is None else src, dst_ref=slot(*block),
                send_sem=send_sems.at[k], recv_sem=recv_sems.at[k], device_id=to, device_id_type=_MESH)

        mine = pltpu.make_async_copy(x_ref, slot(*me), local_sem)
        first = [copy(0, me, sibling, src=x_ref)]
        first += [copy(1 + j, me, (*chip, ic), src=x_ref) for j, chip in enumerate(chips)]
        return ic, me, sibling, chips, copy, mine, first

    def start(self, ins, outs, sems):
        *_, mine, first = self._plan(ins, outs, sems)
        mine.start()
        for cp in first:
            cp.start()

    def finish(self, ins, outs, sems):
        ic, me, sibling, chips, copy, mine, first = self._plan(ins, outs, sems)
        passed = [copy(4 + j, (*chip, ic), sibling) for j, chip in enumerate(chips)]
        for j, chip in enumerate(chips):
            copy(1 + j, (*chip, ic), me).wait_recv()
            passed[j].start()
        copy(0, sibling, me).wait_recv()
        for j, chip in enumerate(chips):
            copy(4 + j, (*chip, 1 - ic), me).wait_recv()
        for cp in first + passed:
            cp.wait_send()
        mine.wait()


class ChipExchangeJob:
    def __init__(self, p4):
        self.inputs = [p4]
        self.out_shapes = [jax.ShapeDtypeStruct(p4.shape, p4.dtype)]
        self.sems = [pltpu.SemaphoreType.DMA((3,)), pltpu.SemaphoreType.DMA((3,)), pltpu.SemaphoreType.DMA(())]
        self.results = None

    @staticmethod
    def _plan(ins, outs, sems):
        (p_ref,), (r_ref,), (send_sems, recv_sems, local_sem) = ins, outs, sems
        ix, iy, ic = lax.axis_index("x"), lax.axis_index("y"), lax.axis_index("c")
        my_xy = 2 * ix + iy
        local = pltpu.make_async_copy(p_ref.at[my_xy], r_ref.at[my_xy], local_sem)
        chips = [(1 - ix, iy), (ix, 1 - iy), (1 - ix, 1 - iy)]
        copies = [
            pltpu.make_async_remote_copy(
                src_ref=p_ref.at[2 * px + py], dst_ref=r_ref.at[my_xy],
                send_sem=send_sems.at[k], recv_sem=recv_sems.at[k],
                device_id=(px, py, ic), device_id_type=_MESH)
            for k, (px, py) in enumerate(chips)
        ]
        return local, copies

    def start(self, ins, outs, sems):
        local, copies = self._plan(ins, outs, sems)
        local.start()
        for cp in copies:
            cp.start()

    def finish(self, ins, outs, sems):
        local, copies = self._plan(ins, outs, sems)
        for cp in copies:
            cp.wait()
        local.wait()


def _each_job(jobs, method, ins, outs, sems):
    i = o = s = 0
    for job in jobs:
        ni, no, ns = len(job.inputs), len(job.out_shapes), len(job.sems)
        getattr(job, method)(ins[i:i + ni], outs[o:o + no], sems[s:s + ns])
        i, o, s = i + ni, o + no, s + ns


def run_jobs(jobs, name):
    j_in = [a for job in jobs for a in job.inputs]
    j_out = [sh for job in jobs for sh in job.out_shapes]
    j_sem = [sm for job in jobs for sm in job.sems]

    def body(*refs):
        ins, outs, sems = refs[:len(j_in)], refs[len(j_in):len(j_in) + len(j_out)], refs[len(j_in) + len(j_out):]
        _each_job(jobs, "start", ins, outs, sems)
        _each_job(jobs, "finish", ins, outs, sems)

    res = pl.pallas_call(body, name=name, out_shape=j_out, in_specs=[_ANY] * len(j_in), out_specs=[_ANY] * len(j_out),
                         scratch_shapes=j_sem)(*j_in)
    _hand_out(jobs, res)


def _hand_out(jobs, res):
    o = 0
    for job in jobs:
        job.results = list(res[o:o + len(job.out_shapes)])
        o += len(job.out_shapes)


def hosted_call(body, args, *, name, grid, in_specs, out_specs, out_shape, scratch_shapes=(), sem=None, jobs=()):
    outs_l, specs_l = list(out_shape), list(out_specs)
    if not jobs:
        return pl.pallas_call(body, name=name, grid=grid, in_specs=list(in_specs), out_specs=specs_l, out_shape=outs_l,
                              scratch_shapes=list(scratch_shapes), compiler_params=_cp(*sem))(*args)
    j_in = [a for job in jobs for a in job.inputs]
    j_out = [sh for job in jobs for sh in job.out_shapes]
    j_sem = [sm for job in jobs for sm in job.sems]
    n_in, n_out, n_scr = len(in_specs), len(outs_l), len(scratch_shapes)

    def wrapped(*refs):
        pos = [0]

        def take(n):
            pos[0] += n
            return refs[pos[0] - n:pos[0]]

        ins, jin, outs, jout, scr, jsem = take(n_in), take(len(j_in)), take(n_out), take(len(j_out)), take(n_scr), take(len(j_sem))
        ids = [pl.program_id(a) for a in range(len(grid))]
        first = functools.reduce(lambda p, q: p & q, [i == 0 for i in ids])
        last = functools.reduce(lambda p, q: p & q, [i == g - 1 for i, g in zip(ids, grid)])

        @pl.when(first)
        def _():
            _each_job(jobs, "start", jin, jout, jsem)

        body(*ins, *outs, *scr)

        @pl.when(last)
        def _():
            _each_job(jobs, "finish", jin, jout, jsem)

    res = pl.pallas_call(
        wrapped, name=name, grid=grid, in_specs=list(in_specs) + [_ANY] * len(j_in),
        out_specs=specs_l + [_ANY] * len(j_out), out_shape=outs_l + j_out,
        scratch_shapes=list(scratch_shapes) + j_sem, compiler_params=_cp(*["arbitrary"] * len(grid)),
    )(*args, *j_in)
    _hand_out(jobs, res[n_out:])
    return list(res[:n_out])


def rs_pair_exchange(g8, name):
    def body(g_ref, r_ref, send_sems, recv_sems):
        ix, iy, ic = lax.axis_index("x"), lax.axis_index("y"), lax.axis_index("c")
        copies = [
            pltpu.make_async_remote_copy(
                src_ref=g_ref.at[2 * xy + (1 - ic)], dst_ref=r_ref.at[xy],
                send_sem=send_sems.at[xy], recv_sem=recv_sems.at[xy],
                device_id=(ix, iy, 1 - ic), device_id_type=_MESH)
            for xy in range(4)
        ]
        for cp in copies:
            cp.start()
        for cp in copies:
            cp.wait()

    return pl.pallas_call(
        body, name=name, out_shape=jax.ShapeDtypeStruct((4,) + g8.shape[1:], g8.dtype),
        in_specs=[_ANY], out_specs=_ANY,
        scratch_shapes=[pltpu.SemaphoreType.DMA((4,)), pltpu.SemaphoreType.DMA((4,))],
    )(g8)


def pair_sum(g8, r4, c_idx, name):
    _, rows, cols = g8.shape
    tr, tc = _tile_2d(rows, cols)

    def body(c_ref, g_ref, r_ref, o_ref):
        o_ref[...] = (g_ref[...].astype(F32) + r_ref[...].astype(F32)).astype(o_ref.dtype)

    grid_spec = pltpu.PrefetchScalarGridSpec(
        num_scalar_prefetch=1, grid=(4, rows // tr, cols // tc),
        in_specs=[pl.BlockSpec((None, tr, tc), lambda s, i, j, c: (2 * s + c[0], i, j)),
                  pl.BlockSpec((None, tr, tc), lambda s, i, j, c: (s, i, j))],
        out_specs=pl.BlockSpec((None, tr, tc), lambda s, i, j, c: (s, i, j)))
    return pl.pallas_call(
        body, name=name, grid_spec=grid_spec, out_shape=jax.ShapeDtypeStruct((4, rows, cols), g8.dtype),
        compiler_params=_cp("parallel", "parallel", "parallel"),
    )(c_idx, g8, r4)


def _tile_2d(rows, cols):
    budget = 128 * 2048
    tr, tc = rows, cols
    if rows % 16 == 0:
        while tr * cols > budget and tr % 32 == 0:
            tr //= 2
    else:
        while rows * tc > budget and tc % (2 * LANE) == 0:
            tc //= 2
    return tr, tc


def _adam_math(w, g, m, v):
    m2 = ADAM_B1 * m + (1.0 - ADAM_B1) * g
    v2 = ADAM_B2 * v + (1.0 - ADAM_B2) * (g * g)
    m_hat = m2 / (1.0 - ADAM_B1 ** ADAM_STEP)
    v_hat = v2 / (1.0 - ADAM_B2 ** ADAM_STEP)
    delta = -ADAM_LR * (m_hat / (jnp.sqrt(v_hat) + ADAM_EPS) + ADAM_WD * w)
    return delta, m2, v2


def adam_layer(parts, w, m, v, bufs, layer, name):
    n_parts, rows, cols = parts.shape
    tr, tc = _tile_2d(rows, cols)

    def body(p_ref, w_ref, m_ref, v_ref, g_in, d_in, m_in, v_in, g_out, d_out, m_out, v_out):
        g = p_ref[0].astype(F32)
        for k in range(1, n_parts):
            g = g + p_ref[k].astype(F32)
        delta, m2, v2 = _adam_math(w_ref[...], g, m_ref[...], v_ref[...])
        g_out[...] = g
        d_out[...] = delta
        m_out[...] = m2
        v_out[...] = v2

    lay = pl.BlockSpec((None, tr, tc), lambda i, j: (layer, i, j))
    return pl.pallas_call(
        body, name=name, grid=(rows // tr, cols // tc),
        in_specs=[pl.BlockSpec((n_parts, tr, tc), lambda i, j: (0, i, j)), lay, lay, lay, _ANY, _ANY, _ANY, _ANY],
        out_specs=[lay, lay, lay, lay],
        out_shape=[jax.ShapeDtypeStruct(w.shape, F32)] * 4,
        input_output_aliases={4: 0, 5: 1, 6: 2, 7: 3},
        compiler_params=_cp("parallel", "parallel"),
    )(parts, w, m, v, *bufs)


def matmul(a, b, mode, out_dtype, name, res=None, caps=(1024, 1024, 512), b_view=None, out_slabs=None, jobs=()):
    if mode == "tn":
        k_dim, m_dim = a.shape
    else:
        m_dim, k_dim = a.shape
    if b_view is None:
        b_rows, b_cols = b.shape
    else:
        kind = b_view
        shard_r, shard_c = b.shape[1:]
        b_rows, b_cols = (shard_r, N_DEV * shard_c) if kind == "cols" else (N_DEV * shard_r, shard_c)
    n_dim = b_rows if mode == "nt" else b_cols
    cap_n, cap_k = caps[1], caps[2]
    tn = _tile(out_slabs, cap_n) if out_slabs else _tile(n_dim, cap_n)
    tk = _tile(k_dim, cap_k)
    if b_view is not None:
        along_n = (kind == "cols") == (mode == "nn")
        shard_len = shard_c if kind == "cols" else shard_r
        if along_n:
            tn = _tile(shard_len, cap_n)
        else:
            tk = _tile(shard_len, cap_k)
    tm = _tile(m_dim, caps[0])
    nk = k_dim // tk

    def body(*refs):
        if res is None:
            a_ref, b_ref, o_ref, acc = refs
        else:
            a_ref, b_ref, r_ref, o_ref, acc = refs
        k = pl.program_id(2)

        @pl.when(k == 0)
        def _():
            acc[...] = jnp.zeros_like(acc)

        acc[...] += lax.dot_general(a_ref[...], b_ref[...], _DN[mode], preferred_element_type=F32)

        @pl.when(k == nk - 1)
        def _():
            r = acc[...]
            if res is not None:
                r = r + r_ref[...]
            o_ref[...] = r.astype(out_dtype)

    a_spec = pl.BlockSpec((tk, tm), lambda i, j, k: (k, i)) if mode == "tn" else pl.BlockSpec((tm, tk), lambda i, j, k: (i, k))
    b_blk = (tn, tk) if mode == "nt" else (tk, tn)
    b_pos = (lambda i, j, k: (j, k)) if mode == "nt" else (lambda i, j, k: (k, j))
    if b_view is None:
        b_spec = pl.BlockSpec(b_blk, b_pos)
    elif kind == "cols":
        per = shard_c // b_blk[1]
        b_spec = pl.BlockSpec((None,) + b_blk, lambda i, j, k: (b_pos(i, j, k)[1] // per, b_pos(i, j, k)[0], b_pos(i, j, k)[1] % per))
    else:
        per = shard_r // b_blk[0]
        b_spec = pl.BlockSpec((None,) + b_blk, lambda i, j, k: (b_pos(i, j, k)[0] // per, b_pos(i, j, k)[0] % per, b_pos(i, j, k)[1]))
    if out_slabs:
        per_o = out_slabs // tn
        o_spec = pl.BlockSpec((None, tm, tn), lambda i, j, k: (j // per_o, i, j % per_o))
        out_shape = jax.ShapeDtypeStruct((n_dim // out_slabs, m_dim, out_slabs), out_dtype)
    else:
        o_spec = pl.BlockSpec((tm, tn), lambda i, j, k: (i, j))
        out_shape = jax.ShapeDtypeStruct((m_dim, n_dim), out_dtype)
    in_specs, args = [a_spec, b_spec], [a, b]
    if res is not None:
        in_specs.append(o_spec)
        args.append(res)
    return hosted_call(
        body, args, name=name, grid=(m_dim // tm, n_dim // tn, nk), in_specs=in_specs, out_specs=[o_spec],
        out_shape=[out_shape], scratch_shapes=[pltpu.VMEM((tm, tn), F32)],
        sem=("parallel", "parallel", "arbitrary"), jobs=jobs)[0]


def rms_fwd(x, gain, name):
    t, d = x.shape
    tt = _tile(t, 256)

    def body(x_ref, g_ref, o_ref):
        o_ref[...] = _rms(x_ref[...], g_ref[...]).astype(o_ref.dtype)

    return pl.pallas_call(
        body, name=name, grid=(t // tt,),
        in_specs=[pl.BlockSpec((tt, d), lambda i: (i, 0)), pl.BlockSpec((1, d), lambda i: (0, 0))],
        out_specs=pl.BlockSpec((tt, d), lambda i: (i, 0)),
        out_shape=jax.ShapeDtypeStruct((t, d), MXU), compiler_params=_cp("parallel"),
    )(x, gain)


def rms_bwd(x, gain, dh, dres, name):
    t, d = x.shape
    tt = _tile(t, 256)

    def body(x_ref, g_ref, dh_ref, dr_ref, dx_ref, dxb_ref, dg_ref):
        _, vjp = jax.vjp(_rms, x_ref[...], g_ref[...])
        dx, dg = vjp(dh_ref[...])
        dx = dx + dr_ref[...]
        dx_ref[...] = dx
        dxb_ref[...] = dx.astype(dxb_ref.dtype)

        @pl.when(pl.program_id(0) == 0)
        def _():
            dg_ref[...] = jnp.zeros_like(dg_ref)

        dg_ref[...] += dg

    row = pl.BlockSpec((tt, d), lambda i: (i, 0))
    vec = pl.BlockSpec((1, d), lambda i: (0, 0))
    return pl.pallas_call(
        body, name=name, grid=(t // tt,), in_specs=[row, vec, row, row], out_specs=[row, row, vec],
        out_shape=[jax.ShapeDtypeStruct((t, d), F32), jax.ShapeDtypeStruct((t, d), MXU), jax.ShapeDtypeStruct((1, d), F32)],
        compiler_params=_cp("arbitrary"),
    )(x, gain, dh, dres)


def sqrelu_fwd(f, name):
    t, n = f.shape
    tt, tn = _tile(t, 256), _tile(n, 2048)

    def body(f_ref, o_ref):
        r = jnp.maximum(f_ref[...], 0.0)
        o_ref[...] = (r * r).astype(o_ref.dtype)

    blk = pl.BlockSpec((tt, tn), lambda i, j: (i, j))
    return pl.pallas_call(body, name=name, grid=(t // tt, n // tn), in_specs=[blk], out_specs=blk,
                          out_shape=jax.ShapeDtypeStruct((t, n), MXU), compiler_params=_cp("parallel", "parallel"))(f)


def sqrelu_bwd(f, da, name):
    t, n = f.shape
    tt, tn = _tile(t, 256), _tile(n, 2048)

    def body(f_ref, da_ref, o_ref):
        o_ref[...] = (da_ref[...] * (2.0 * jnp.maximum(f_ref[...], 0.0))).astype(o_ref.dtype)

    blk = pl.BlockSpec((tt, tn), lambda i, j: (i, j))
    return pl.pallas_call(body, name=name, grid=(t // tt, n // tn), in_specs=[blk, blk], out_specs=blk,
                          out_shape=jax.ShapeDtypeStruct((t, n), MXU), compiler_params=_cp("parallel", "parallel"))(f, da)


def loss_head(x, target, name):
    t, d = x.shape
    tt = _tile(t, 256)

    def body(x_ref, t_ref, dx_ref, dxb_ref, l_ref):
        e = x_ref[...] - t_ref[...]
        dx = e * (1.0 / d)
        dx_ref[...] = dx
        dxb_ref[...] = dx.astype(dxb_ref.dtype)

        @pl.when(pl.program_id(0) == 0)
        def _():
            l_ref[...] = jnp.zeros_like(l_ref)

        part = jnp.sum(jnp.sum(e * e, axis=-1, keepdims=True) * (1.0 / d), axis=0, keepdims=True)
        l_ref[...] += 0.5 * part

    row = pl.BlockSpec((tt, d), lambda i: (i, 0))
    return pl.pallas_call(
        body, name=name, grid=(t // tt,), in_specs=[row, row],
        out_specs=[row, row, pl.BlockSpec((8, LANE), lambda i: (0, 0))],
        out_shape=[jax.ShapeDtypeStruct((t, d), F32), jax.ShapeDtypeStruct((t, d), MXU), jax.ShapeDtypeStruct((8, LANE), F32)],
        compiler_params=_cp("arbitrary"),
    )(x, target)


def _merge_f(g0, g1, g2, ba, bb, bc):
    return _sigmoid(g0) * ba + _sigmoid(g1) * bb + _sigmoid(g2) * bc


def merge_fwd(z, off, ba, bb, bc, name):
    t, d = ba.shape
    tt, td = _tile(t, 256), _tile(d // 2, 1024)
    nd, ob = d // td, off // td

    def body(g0, g1, g2, a, b, c, o_ref):
        o_ref[...] = _merge_f(g0[...], g1[...], g2[...], a[...], b[...], c[...]).astype(o_ref.dtype)

    gates = [pl.BlockSpec((tt, td), functools.partial(lambda i, j, s: (i, ob + s * nd + j), s=s)) for s in range(3)]
    blk = pl.BlockSpec((tt, td), lambda i, j: (i, j))
    return pl.pallas_call(body, name=name, grid=(t // tt, nd), in_specs=gates + [blk] * 3, out_specs=blk,
                          out_shape=jax.ShapeDtypeStruct((t, d), MXU), compiler_params=_cp("parallel", "parallel"))(z, z, z, ba, bb, bc)


def merge_bwd(z, off, ba, bb, bc, dy, name):
    t, d = ba.shape
    tt, td = _tile(t, 256), _tile(d // 2, 1024)
    nd, ob = d // td, off // td

    def body(g0, g1, g2, a, b, c, dy_ref, dgl, da, db, dc):
        _, vjp = jax.vjp(_merge_f, g0[...], g1[...], g2[...], a[...], b[...], c[...])
        d0, d1, d2, xa, xb, xc = vjp(dy_ref[...])
        for s, dv in enumerate((d0, d1, d2)):
            dgl[s] = dv.astype(dgl.dtype)
        da[...] = xa.astype(da.dtype)
        db[...] = xb.astype(db.dtype)
        dc[...] = xc.astype(dc.dtype)

    gates = [pl.BlockSpec((tt, td), functools.partial(lambda i, j, s: (i, ob + s * nd + j), s=s)) for s in range(3)]
    blk = pl.BlockSpec((tt, td), lambda i, j: (i, j))
    dgl, da, db, dc = pl.pallas_call(
        body, name=name, grid=(t // tt, nd), in_specs=gates + [blk] * 4,
        out_specs=[pl.BlockSpec((3, tt, td), lambda i, j: (0, i, j)), blk, blk, blk],
        out_shape=[jax.ShapeDtypeStruct((3, t, d), MXU)] + [jax.ShapeDtypeStruct((t, d), MXU)] * 3,
        compiler_params=_cp("parallel", "parallel"),
    )(z, z, z, ba, bb, bc, dy)
    return dgl, da, db, dc


def _gdn_pre_f(qp, kp, vp, ab, cq, ck, cv, alog, dtb, h, n_heads):
    def conv(xp, cw):
        acc = xp * cw[CONV_K - 1]
        for j in range(1, CONV_K):
            acc = acc + shift(xp, j) * cw[CONV_K - 1 - j]
        return _silu(acc)

    q, k, v = conv(qp, cq), conv(kp, ck), conv(vp, cv)
    q = q * lax.rsqrt(jnp.sum(q * q, axis=-1, keepdims=True) + EPS) * (HD ** -0.5)
    k = k * lax.rsqrt(jnp.sum(k * k, axis=-1, keepdims=True) + EPS)
    lane = _iota2(ab.shape, 1)
    a_col = jnp.sum(jnp.where(lane == h, ab, 0.0), axis=-1, keepdims=True)
    b_col = jnp.sum(jnp.where(lane == n_heads + h, ab, 0.0), axis=-1, keepdims=True)
    lane1 = _iota2(alog.shape, 1)
    al = jnp.sum(jnp.where(lane1 == h, alog, 0.0), axis=-1, keepdims=True)
    dt = jnp.sum(jnp.where(lane1 == h, dtb, 0.0), axis=-1, keepdims=True)
    g = -jnp.exp(al) * _softplus(a_col + dt)
    return q, k, v, g, _sigmoid(b_col)


def _gdn_pre_specs(t, n_heads, ab_blk):
    zq = [pl.BlockSpec((t, HD), functools.partial(lambda h, s: (0, s * n_heads + h), s=s)) for s in range(3)]
    ab = pl.BlockSpec((t, LANE), lambda h: (0, ab_blk))
    cw = [pl.BlockSpec((CONV_K, HD), functools.partial(lambda h, s: (0, s * n_heads + h), s=s)) for s in range(3)]
    vec = pl.BlockSpec((1, LANE), lambda h: (0, 0))
    return zq, ab, cw, vec


def gdn_pre_fwd(z, ab_blk, conv_w, alog, dtb, n_heads, name):
    t = z.shape[0]
    zq, ab, cw, vec = _gdn_pre_specs(t, n_heads, ab_blk)

    def body(qp, kp, vp, ab_ref, cq, ck, cv, al, dt, q_o, k_o, v_o, g_o, b_o):
        rows = lambda r: tuple(r[j:j + 1, :] for j in range(CONV_K))
        outs = _gdn_pre_f(qp[...], kp[...], vp[...], ab_ref[...], rows(cq), rows(ck), rows(cv), al[...], dt[...],
                          pl.program_id(0), n_heads)
        for o_ref, val in zip((q_o, k_o, v_o, g_o, b_o), outs):
            o_ref[...] = val

    head = pl.BlockSpec((None, t, HD), lambda h: (h, 0, 0))
    col = pl.BlockSpec((None, t, 1), lambda h: (h, 0, 0))
    return pl.pallas_call(
        body, name=name, grid=(n_heads,), in_specs=zq + [ab] + cw + [vec, vec], out_specs=[head] * 3 + [col] * 2,
        out_shape=[jax.ShapeDtypeStruct((n_heads, t, HD), F32)] * 3 + [jax.ShapeDtypeStruct((n_heads, t, 1), F32)] * 2,
        compiler_params=_cp("parallel"),
    )(z, z, z, z, conv_w, conv_w, conv_w, alog, dtb)


def gdn_pre_bwd(z, ab_blk, conv_w, alog, dtb, n_heads, dq, dk, dv, dg, db, name):
    t = z.shape[0]
    gw = n_heads * HD
    zq, ab, cw, vec = _gdn_pre_specs(t, n_heads, ab_blk)

    def body(qp, kp, vp, ab_ref, cq, ck, cv, al, dt, dq_r, dk_r, dv_r, dg_r, db_r,
             dqp, dkp, dvp, dab, dcq, dck, dcv, dal, ddt):
        h = pl.program_id(0)
        rows = lambda r: tuple(r[j:j + 1, :] for j in range(CONV_K))
        f = functools.partial(_gdn_pre_f, h=h, n_heads=n_heads)
        _, vjp = jax.vjp(f, qp[...], kp[...], vp[...], ab_ref[...], rows(cq), rows(ck), rows(cv), al[...], dt[...])
        gq, gk, gv, gab, gcq, gck, gcv, gal, gdt = vjp((dq_r[...], dk_r[...], dv_r[...], dg_r[...], db_r[...]))
        dqp[...] = gq.astype(dqp.dtype)
        dkp[...] = gk.astype(dkp.dtype)
        dvp[...] = gv.astype(dvp.dtype)
        for ref, gr in ((dcq, gcq), (dck, gck), (dcv, gcv)):
            for j in range(CONV_K):
                ref[j:j + 1, :] = gr[j]

        @pl.when(h == 0)
        def _():
            dab[...] = jnp.zeros_like(dab)
            dal[...] = jnp.zeros_like(dal)
            ddt[...] = jnp.zeros_like(ddt)

        dab[...] += gab
        dal[...] += gal
        ddt[...] += gdt

    head = pl.BlockSpec((None, t, HD), lambda h: (h, 0, 0))
    col = pl.BlockSpec((None, t, 1), lambda h: (h, 0, 0))
    seg = pl.BlockSpec((t, HD), lambda h: (0, h))
    cseg = pl.BlockSpec((CONV_K, HD), lambda h: (0, h))
    return pl.pallas_call(
        body, name=name, grid=(n_heads,),
        in_specs=zq + [ab] + cw + [vec, vec] + [head] * 3 + [col] * 2,
        out_specs=[seg] * 3 + [pl.BlockSpec((t, LANE), lambda h: (0, 0))] + [cseg] * 3 + [vec, vec],
        out_shape=[jax.ShapeDtypeStruct((t, gw), MXU)] * 3 + [jax.ShapeDtypeStruct((t, LANE), F32)]
        + [jax.ShapeDtypeStruct((CONV_K, gw), F32)] * 3 + [jax.ShapeDtypeStruct((1, LANE), F32)] * 2,
        compiler_params=_cp("arbitrary"),
    )(z, z, z, z, conv_w, conv_w, conv_w, alog, dtb, dq, dk, dv, dg, db)


def _gdn_chunk_f(q, k, v, g, b):
    c = CHUNK
    r, s = _iota2((c, c), 0), _iota2((c, c), 1)
    tril = (s <= r).astype(F32)
    gc_w = mm(tril, jnp.broadcast_to(g, (c, HD)), "nn", "la")
    gc_i = mm(tril, jnp.broadcast_to(g, (c, c)), "nn", "la")
    gc_j = mm(jnp.ones((c, c), F32), jnp.where(r == s, gc_i, 0.0), "nn", "la")
    decay = jnp.exp(jnp.where(s <= r, gc_i - gc_j, -1e30))
    kb = k * b
    low = jnp.where(s < r, mm(kb, k, "nt", "x3") * decay, 0.0)
    inv = jnp.where(r == s, 1.0, 0.0) - low
    pw = mm(low, low, "nn", "x3")
    n_sq = 1
    while 2 * n_sq < c:
        inv = inv + mm(inv, pw, "nn", "x3")
        n_sq *= 2
        if 2 * n_sq < c:
            pw = mm(pw, pw, "nn", "x3")
    egc = jnp.exp(gc_w)
    u = mm(inv, v * b, "nn", "x3")
    w = mm(inv, kb * egc, "nn", "x3")
    intra = mm(q, k, "nt", "lo") * decay
    g_last = jnp.sum(g, axis=0, keepdims=True)
    kd = k * jnp.exp(g_last - gc_w)
    egl = jnp.exp(jnp.broadcast_to(g_last, (1, HD)))
    return u, w, intra, q * egc, kd, egl


def _group(n, cap=4):
    return max(g for g in range(1, cap + 1) if n % g == 0)


def _chunk_specs(nb_h, nb_c, n_chunks=None):
    cn = (lambda n: n) if n_chunks is None else (lambda n: n_chunks // nb_c - 1 - n)
    rows = nb_c * CHUNK
    vec = pl.BlockSpec((nb_h, rows, HD), lambda h, n: (h, cn(n), 0))
    col = pl.BlockSpec((nb_h, rows, 1), lambda h, n: (h, cn(n), 0))
    sq = pl.BlockSpec((nb_h, rows, CHUNK), lambda h, n: (h, cn(n), 0))
    one = pl.BlockSpec((nb_h, nb_c, 1, HD), lambda h, n: (h, cn(n), 0, 0))
    st = pl.BlockSpec((nb_h, nb_c, HD, HD), lambda h, n: (h, cn(n), 0, 0))
    return vec, col, sq, one, st


def _chunk_shapes(n_heads, t):
    vec = jax.ShapeDtypeStruct((n_heads, t, HD), F32)
    return [vec, vec, jax.ShapeDtypeStruct((n_heads, t, CHUNK), F32), vec, vec,
            jax.ShapeDtypeStruct((n_heads, t // CHUNK, 1, HD), F32)]


def _chunk_rows(ci):
    return slice(ci * CHUNK, (ci + 1) * CHUNK)


def gdn_chunk_fwd(q, k, v, g, b, name, jobs=()):
    n_heads, t, _ = q.shape
    nb_c = _group(t // CHUNK)
    vec, col, sq, one, _ = _chunk_specs(1, nb_c)

    def body(q_r, k_r, v_r, g_r, b_r, *outs):
        for ci in range(nb_c):
            rows = _chunk_rows(ci)
            vals = _gdn_chunk_f(*(r[0, rows, :] for r in (q_r, k_r, v_r, g_r, b_r)))
            for o_ref, val in zip(outs[:5], vals[:5]):
                o_ref[0, rows, :] = val
            outs[5][0, ci] = vals[5]

    return hosted_call(
        body, (q, k, v, g, b), name=name, grid=(n_heads, t // CHUNK // nb_c), in_specs=[vec] * 3 + [col] * 2,
        out_specs=[vec, vec, sq, vec, vec, one], out_shape=_chunk_shapes(n_heads, t),
        sem=("parallel", "parallel"), jobs=jobs)


def gdn_chunk_bwd(q, k, v, g, b, cts, name, jobs=()):
    n_heads, t, _ = q.shape
    nb_c = _group(t // CHUNK)
    vec, col, sq, one, _ = _chunk_specs(1, nb_c)

    def body(q_r, k_r, v_r, g_r, b_r, du, dw, di, dqd, dkd, degl, dq, dk, dv, dg, db):
        for ci in range(nb_c):
            rows = _chunk_rows(ci)
            _, vjp = jax.vjp(_gdn_chunk_f, *(r[0, rows, :] for r in (q_r, k_r, v_r, g_r, b_r)))
            grads = vjp(tuple(r[0, rows, :] for r in (du, dw, di, dqd, dkd)) + (degl[0, ci],))
            for o_ref, val in zip((dq, dk, dv, dg, db), grads):
                o_ref[0, rows, :] = val

    col_shape = jax.ShapeDtypeStruct((n_heads, t, 1), F32)
    return hosted_call(
        body, (q, k, v, g, b, *cts), name=name, grid=(n_heads, t // CHUNK // nb_c),
        in_specs=[vec] * 3 + [col] * 2 + [vec, vec, sq, vec, vec, one],
        out_specs=[vec] * 3 + [col] * 2,
        out_shape=[jax.ShapeDtypeStruct((n_heads, t, HD), F32)] * 3 + [col_shape] * 2,
        sem=("parallel", "parallel"), jobs=jobs)


def _scan_f(s, u, w, a, qd, kd, egl):
    vn = u - mm(w, s, "nn", "lo")
    o = mm(qd, s, "nn", "lo") + mm(a, vn, "nn", "lo")
    return o, s * egl + mm(kd, vn, "tn", "lo")


def gdn_scan_fwd(chunks, name):
    u = chunks[0]
    n_heads, t, _ = u.shape
    nc = t // CHUNK
    nb_h = _group(n_heads)
    vec, _, sq, one, st = _chunk_specs(nb_h, 1)

    def body(u_r, w_r, a_r, qd_r, kd_r, e_r, o_ref, s_ref, state):
        @pl.when(pl.program_id(1) == 0)
        def _():
            state[...] = jnp.zeros_like(state)

        for hh in range(nb_h):
            s = state[hh]
            s_ref[hh, 0] = s
            o, s2 = _scan_f(s, u_r[hh], w_r[hh], a_r[hh], qd_r[hh], kd_r[hh], e_r[hh, 0])
            o_ref[hh] = o
            state[hh] = s2

    return pl.pallas_call(
        body, name=name, grid=(n_heads // nb_h, nc), in_specs=[vec, vec, sq, vec, vec, one], out_specs=[vec, st],
        out_shape=[jax.ShapeDtypeStruct((n_heads, t, HD), F32), jax.ShapeDtypeStruct((n_heads, nc, HD, HD), F32)],
        scratch_shapes=[pltpu.VMEM((nb_h, HD, HD), F32)], compiler_params=_cp("parallel", "arbitrary"),
    )(*chunks)


def gdn_scan_bwd(chunks, states, do, name):
    n_heads, t, _ = do.shape
    nc = t // CHUNK
    nb_h = _group(n_heads)
    vec, _, sq, one, st = _chunk_specs(nb_h, 1, n_chunks=nc)

    def body(u_r, w_r, a_r, qd_r, kd_r, e_r, s_r, do_r, du, dw, da, dqd, dkd, de, dstate):
        @pl.when(pl.program_id(1) == 0)
        def _():
            dstate[...] = jnp.zeros_like(dstate)

        for hh in range(nb_h):
            _, vjp = jax.vjp(_scan_f, s_r[hh, 0], u_r[hh], w_r[hh], a_r[hh], qd_r[hh], kd_r[hh], e_r[hh, 0])
            grads = vjp((do_r[hh], dstate[hh]))
            dstate[hh] = grads[0]
            for o_ref, val in zip((du, dw, da, dqd, dkd), grads[1:6]):
                o_ref[hh] = val
            de[hh, 0] = grads[6]

    return pl.pallas_call(
        body, name=name, grid=(n_heads // nb_h, nc), in_specs=[vec, vec, sq, vec, vec, one, st, vec],
        out_specs=[vec, vec, sq, vec, vec, one], out_shape=_chunk_shapes(n_heads, t),
        scratch_shapes=[pltpu.VMEM((nb_h, HD, HD), F32)], compiler_params=_cp("parallel", "arbitrary"),
    )(*chunks, states, do)


def _post_f(o, gate, gain):
    return _rms(o, gain) * _silu(gate)


def gdn_post_fwd(o, z, gate_blk, gain, name):
    n_heads, t, _ = o.shape
    tt = _tile(t, 512)

    def body(o_r, gt_r, gn_r, out):
        out[...] = _post_f(o_r[...], gt_r[...], gn_r[...]).astype(out.dtype)

    return pl.pallas_call(
        body, name=name, grid=(n_heads, t // tt),
        in_specs=[pl.BlockSpec((None, tt, HD), lambda h, i: (h, i, 0)), pl.BlockSpec((tt, HD), lambda h, i: (i, gate_blk + h)),
                  pl.BlockSpec((1, HD), lambda h, i: (0, 0))],
        out_specs=pl.BlockSpec((tt, HD), lambda h, i: (i, h)),
        out_shape=jax.ShapeDtypeStruct((t, n_heads * HD), MXU), compiler_params=_cp("parallel", "parallel"),
    )(o, z, gain)


def gdn_post_bwd(o, z, gate_blk, gain, doa, name):
    n_heads, t, _ = o.shape
    tt = _tile(t, 512)

    def body(o_r, gt_r, gn_r, d_r, do_ref, dgt_ref, dgn_ref):
        _, vjp = jax.vjp(_post_f, o_r[...], gt_r[...], gn_r[...])
        go, ggt, ggn = vjp(d_r[...])
        do_ref[...] = go
        dgt_ref[...] = ggt.astype(dgt_ref.dtype)

        @pl.when((pl.program_id(0) == 0) & (pl.program_id(1) == 0))
        def _():
            dgn_ref[...] = jnp.zeros_like(dgn_ref)

        dgn_ref[...] += ggn

    tok = pl.BlockSpec((tt, HD), lambda h, i: (i, h))
    vec = pl.BlockSpec((1, HD), lambda h, i: (0, 0))
    head = pl.BlockSpec((None, tt, HD), lambda h, i: (h, i, 0))
    return pl.pallas_call(
        body, name=name, grid=(n_heads, t // tt),
        in_specs=[head, pl.BlockSpec((tt, HD), lambda h, i: (i, gate_blk + h)), vec, tok],
        out_specs=[head, tok, vec],
        out_shape=[jax.ShapeDtypeStruct((n_heads, t, HD), F32), jax.ShapeDtypeStruct((t, n_heads * HD), MXU),
                   jax.ShapeDtypeStruct((1, HD), F32)],
        compiler_params=_cp("arbitrary", "arbitrary"),
    )(o, z, gain, doa)


def _gmlp_f(ups, vps, lngs, wss, bcols):
    n_groups = len(ups)
    width = HD * n_groups
    us = [_gelu(a) for a in ups]
    vs = [_gelu(a) for a in vps]
    mu = sum(jnp.sum(a, axis=-1, keepdims=True) for a in vs) * (1.0 / width)
    xcs = [a - mu for a in vs]
    var = sum(jnp.sum(a * a, axis=-1, keepdims=True) for a in xcs) * (1.0 / width)
    rstd = lax.rsqrt(var + EPS)
    r, s = _iota2((HD, HD), 0), _iota2((HD, HD), 1)
    causal = (s // CHUNK) <= (r // CHUNK)
    outs = []
    for gi in range(n_groups):
        vb = xcs[gi] * rstd * lngs[gi]
        sp = mm(jnp.where(causal, wss[gi], 0.0), vb, "nn", "lo") + bcols[gi]
        outs.append(us[gi] * sp)
    return tuple(outs)


def _gmlp_load(uv_u, uv_v, lng, ws, bt, n_groups):
    seg = lambda ref, gi: ref[:, gi * HD:(gi + 1) * HD]
    return ([seg(uv_u, gi) for gi in range(n_groups)], [seg(uv_v, gi) for gi in range(n_groups)],
            [seg(lng, gi) for gi in range(n_groups)], [ws[gi] for gi in range(n_groups)],
            [bt[:, gi:gi + 1] for gi in range(n_groups)])


def _gmlp_specs(width, u_blk, n_groups):
    u = pl.BlockSpec((HD, width), lambda i: (i, u_blk))
    v = pl.BlockSpec((HD, width), lambda i: (i, u_blk + 1))
    lng = pl.BlockSpec((1, width), lambda i: (0, 0))
    ws = pl.BlockSpec((n_groups, HD, HD), lambda i: (0, 0, 0))
    bt = pl.BlockSpec((HD, LANE), lambda i: (0, 0))
    return u, v, lng, ws, bt


def gmlp_fwd(z, uv_off, width, lng, ws, bt, name):
    t = z.shape[0]
    n_groups = width // HD
    specs = _gmlp_specs(width, uv_off // width, n_groups)

    def body(u_r, v_r, l_r, w_r, b_r, out):
        outs = _gmlp_f(*_gmlp_load(u_r, v_r, l_r, w_r, b_r, n_groups))
        for gi in range(n_groups):
            out[:, gi * HD:(gi + 1) * HD] = outs[gi].astype(out.dtype)

    return pl.pallas_call(
        body, name=name, grid=(t // HD,), in_specs=list(specs), out_specs=pl.BlockSpec((HD, width), lambda i: (i, 0)),
        out_shape=jax.ShapeDtypeStruct((t, width), MXU), compiler_params=_cp("parallel"),
    )(z, z, lng, ws, bt)


def gmlp_bwd(z, uv_off, width, lng, ws, bt, dob, name):
    t = z.shape[0]
    n_groups = width // HD
    specs = _gmlp_specs(width, uv_off // width, n_groups)

    def body(u_r, v_r, l_r, w_r, b_r, d_r, duv, dl, dws, dbt):
        _, vjp = jax.vjp(_gmlp_f, *_gmlp_load(u_r, v_r, l_r, w_r, b_r, n_groups))
        gu, gv, gl, gw, gb = vjp(tuple(d_r[:, gi * HD:(gi + 1) * HD] for gi in range(n_groups)))

        @pl.when(pl.program_id(0) == 0)
        def _():
            dl[...] = jnp.zeros_like(dl)
            dws[...] = jnp.zeros_like(dws)
            dbt[...] = jnp.zeros_like(dbt)

        for gi in range(n_groups):
            duv[:, gi * HD:(gi + 1) * HD] = gu[gi].astype(duv.dtype)
            duv[:, width + gi * HD:width + (gi + 1) * HD] = gv[gi].astype(duv.dtype)
            dl[:, gi * HD:(gi + 1) * HD] += gl[gi]
            dws[gi] += gw[gi]
            dbt[:, gi:gi + 1] += gb[gi]

    return pl.pallas_call(
        body, name=name, grid=(t // HD,), in_specs=list(specs) + [pl.BlockSpec((HD, width), lambda i: (i, 0))],
        out_specs=[pl.BlockSpec((HD, 2 * width), lambda i: (i, 0)), specs[2], specs[3], specs[4]],
        out_shape=[jax.ShapeDtypeStruct((t, 2 * width), MXU), jax.ShapeDtypeStruct((1, width), F32),
                   jax.ShapeDtypeStruct((n_groups, HD, HD), F32), jax.ShapeDtypeStruct((HD, LANE), F32)],
        compiler_params=_cp("arbitrary"),
    )(z, z, lng, ws, bt, dob)


def _sba_pre_f(qp, kp, qg, kg):
    return _rms(qp, qg), _rms(kp, kg)


def sba_pre_fwd(z, c_blk, n_heads, qg, kg, name):
    t = z.shape[0]
    tt = _tile(t, 512)
    zs = [pl.BlockSpec((tt, HD), functools.partial(lambda h, i, s: (i, c_blk + s * n_heads + h), s=s)) for s in range(3)]
    vec = pl.BlockSpec((1, HD), lambda h, i: (0, 0))
    head = pl.BlockSpec((None, tt, HD), lambda h, i: (h, i, 0))

    def body(qp, kp, vp, qg_r, kg_r, q_o, k_o, v_o):
        q, k = _sba_pre_f(qp[...], kp[...], qg_r[...], kg_r[...])
        q_o[...] = q.astype(q_o.dtype)
        k_o[...] = k.astype(k_o.dtype)
        v_o[...] = vp[...].astype(v_o.dtype)

    return pl.pallas_call(
        body, name=name, grid=(n_heads, t // tt), in_specs=zs + [vec, vec], out_specs=[head] * 3,
        out_shape=[jax.ShapeDtypeStruct((n_heads, t, HD), MXU)] * 3, compiler_params=_cp("parallel", "parallel"),
    )(z, z, z, qg, kg)


def sba_pre_bwd(z, c_blk, n_heads, qg, kg, dq, dk, dv, name):
    t = z.shape[0]
    tt = _tile(t, 512)
    zs = [pl.BlockSpec((tt, HD), functools.partial(lambda h, i, s: (i, c_blk + s * n_heads + h), s=s)) for s in range(2)]
    vec = pl.BlockSpec((1, HD), lambda h, i: (0, 0))
    head = pl.BlockSpec((None, tt, HD), lambda h, i: (h, i, 0))
    tok = pl.BlockSpec((tt, HD), lambda h, i: (i, h))

    def body(qp, kp, qg_r, kg_r, dq_r, dk_r, dv_r, dqp, dkp, dvp, dqg, dkg):
        _, vjp = jax.vjp(_sba_pre_f, qp[...], kp[...], qg_r[...], kg_r[...])
        gq, gk, gqg, gkg = vjp((dq_r[...], dk_r[...]))
        dqp[...] = gq.astype(dqp.dtype)
        dkp[...] = gk.astype(dkp.dtype)
        dvp[...] = dv_r[...].astype(dvp.dtype)

        @pl.when((pl.program_id(0) == 0) & (pl.program_id(1) == 0))
        def _():
            dqg[...] = jnp.zeros_like(dqg)
            dkg[...] = jnp.zeros_like(dkg)

        dqg[...] += gqg
        dkg[...] += gkg

    return pl.pallas_call(
        body, name=name, grid=(n_heads, t // tt), in_specs=zs + [vec, vec] + [head] * 3,
        out_specs=[tok] * 3 + [vec, vec],
        out_shape=[jax.ShapeDtypeStruct((t, n_heads * HD), MXU)] * 3 + [jax.ShapeDtypeStruct((1, HD), F32)] * 2,
        compiler_params=_cp("arbitrary", "arbitrary"),
    )(z, z, qg, kg, dq, dk, dv)


def _sba_block(q, kj, i, j):
    zz = lax.dot_general(q, kj, _DN["nt"], preferred_element_type=F32) * (HD ** -0.5)
    ls = _logsig(zz)
    strict = (j * HD + _iota2((HD, HD), 1)) < (i * HD + _iota2((HD, HD), 0))
    return zz, ls, jnp.where(strict, ls - zz, 0.0), strict


def sba_fwd(q, k, v, name, jobs=()):
    n_heads, t, _ = q.shape
    nb_h = _group(n_heads)

    def body(q_r, k_r, v_r, o_ref, tot_ref):
        i = pl.program_id(1)
        after = (_iota2((HD, HD), 0) > _iota2((HD, HD), 1)).astype(F32)

        def step(it, carry):
            j = i - it
            rows = pl.ds(pl.multiple_of(j * HD, HD), HD)
            new = []
            for hh in range(nb_h):
                acc, cs = carry[hh]
                _, ls, lk, strict = _sba_block(q_r[hh], k_r[hh, rows, :], i, j)
                suffix = _mm_raw(lk, after, "nn", "rb") + cs
                att = jnp.where(strict, jnp.exp(ls + suffix), 0.0)
                acc = acc + _mm_raw(att, v_r[hh, rows, :], "nn", "lo")
                new.append((acc, cs + jnp.sum(lk, axis=-1, keepdims=True)))
            return tuple(new)

        init = tuple((jnp.zeros((HD, HD), F32), jnp.zeros((HD, 1), F32)) for _ in range(nb_h))
        final = lax.fori_loop(0, i + 1, step, init)
        for hh in range(nb_h):
            o_ref[:, hh * HD:(hh + 1) * HD] = final[hh][0].astype(o_ref.dtype)
            tot_ref[hh] = final[hh][1]

    full = pl.BlockSpec((nb_h, t, HD), lambda h, i: (h, 0, 0))
    return hosted_call(
        body, (q, k, v), name=name, grid=(n_heads // nb_h, t // HD),
        in_specs=[pl.BlockSpec((nb_h, HD, HD), lambda h, i: (h, i, 0)), full, full],
        out_specs=[pl.BlockSpec((HD, nb_h * HD), lambda h, i: (i, h)), pl.BlockSpec((nb_h, HD, 1), lambda h, i: (h, i, 0))],
        out_shape=[jax.ShapeDtypeStruct((t, n_heads * HD), MXU), jax.ShapeDtypeStruct((n_heads, t, 1), F32)],
        sem=("parallel", "parallel"), jobs=jobs)


def sba_bwd(q, k, v, tot, do, name, jobs=()):
    n_heads, t, _ = q.shape
    nb_h = _group(n_heads)

    def body(q_r, k_r, v_r, tot_r, do_r, dq_ref, dk_ref, dv_ref):
        i = pl.program_id(1)

        @pl.when(i == 0)
        def _():
            dk_ref[...] = jnp.zeros_like(dk_ref)
            dv_ref[...] = jnp.zeros_like(dv_ref)

        r, s = _iota2((HD, HD), 0), _iota2((HD, HD), 1)
        upto = (r <= s).astype(F32)
        before = (r < s).astype(F32)

        def step(j, carry):
            rows = pl.ds(pl.multiple_of(j * HD, HD), HD)
            new = []
            for hh in range(nb_h):
                dq, cp, cd = carry[hh]
                qb, dob = q_r[hh], do_r[:, hh * HD:(hh + 1) * HD]
                kj, vj = k_r[hh, rows, :], v_r[hh, rows, :]
                _, ls, lk, strict = _sba_block(qb, kj, i, j)
                sig = jnp.exp(ls)
                suffix = tot_r[hh] - (cp + _mm_raw(lk, upto, "nn", "rb"))
                att = jnp.where(strict, jnp.exp(ls + suffix), 0.0)
                dp = _mm_raw(dob, vj, "nt", "lo") * att
                dlk = cd + _mm_raw(dp, before, "nn", "rb")
                dz = jnp.where(strict, dp * (1.0 - sig) - dlk * sig, 0.0) * (HD ** -0.5)
                dk_ref[hh, rows, :] += _mm_raw(dz, qb, "tn", "lo")
                dv_ref[hh, rows, :] += _mm_raw(att, dob, "tn", "lo")
                new.append((dq + _mm_raw(dz, kj, "nn", "lo"), cp + jnp.sum(lk, axis=-1, keepdims=True),
                            cd + jnp.sum(dp, axis=-1, keepdims=True)))
            return tuple(new)

        zero_col = jnp.zeros((HD, 1), F32)
        final = lax.fori_loop(0, i + 1, step, tuple((jnp.zeros((HD, HD), F32), zero_col, zero_col) for _ in range(nb_h)))
        for hh in range(nb_h):
            dq_ref[hh] = final[hh][0]

    full = pl.BlockSpec((nb_h, t, HD), lambda h, i: (h, 0, 0))
    blk = pl.BlockSpec((nb_h, HD, HD), lambda h, i: (h, i, 0))
    return hosted_call(
        body, (q, k, v, tot, do), name=name, grid=(n_heads // nb_h, t // HD),
        in_specs=[blk, full, full, pl.BlockSpec((nb_h, HD, 1), lambda h, i: (h, i, 0)),
                  pl.BlockSpec((HD, nb_h * HD), lambda h, i: (i, h))],
        out_specs=[blk, full, full], out_shape=[jax.ShapeDtypeStruct((n_heads, t, HD), F32)] * 3,
        sem=("parallel", "arbitrary"), jobs=jobs)


def small_adam(parts, w, m, v, name):
    n_parts, rows, _ = parts.shape
    tr = _tile(rows, 512) if rows % LANE == 0 else rows

    def body(p_ref, w_ref, m_ref, v_ref, g_out, d_out, m_out, v_out):
        g = p_ref[0]
        for k in range(1, n_parts):
            g = g + p_ref[k]
        delta, m2, v2 = _adam_math(w_ref[...], g, m_ref[...], v_ref[...])
        g_out[...] = g
        d_out[...] = delta
        m_out[...] = m2
        v_out[...] = v2

    blk = pl.BlockSpec((tr, LANE), lambda i: (i, 0))
    return pl.pallas_call(
        body, name=name, grid=(rows // tr,),
        in_specs=[pl.BlockSpec((n_parts, tr, LANE), lambda i: (0, i, 0)), blk, blk, blk], out_specs=[blk] * 4,
        out_shape=[jax.ShapeDtypeStruct((rows, LANE), F32)] * 4, compiler_params=_cp("parallel"),
    )(parts, w, m, v)


def _pack(arrays):
    flat = jnp.concatenate([a.reshape(-1).astype(F32) for a in arrays])
    pad = (-flat.shape[0]) % (8 * LANE)
    return jnp.pad(flat, (0, pad)).reshape(-1, LANE)


def _unpack(packed, shapes):
    flat, outs, pos = packed.reshape(-1), [], 0
    for shp in shapes:
        n = 1
        for s in shp:
            n *= s
        outs.append(flat[pos:pos + n].reshape(shp))
        pos += n
    return outs


def _pad_lanes(a):
    return jnp.pad(a, ((0, 0), (0, LANE - a.shape[1])))


def kernel(x, w_in, conv_w, a_log, dt_bias, gdn_norm_g, gmlp_ln_g, w_spatial, b_spatial, sba_q_g, sba_k_g, w_out_a, w_out_b, w_out_c, w_out, norm_mix_g, norm_mlp_g, w_ff1, w_ff2, loss_target, m_w_in, m_conv_w, m_a_log, m_dt_bias, m_gdn_norm_g, m_gmlp_ln_g, m_w_spatial, m_b_spatial, m_sba_q_g, m_sba_k_g, m_w_out_a, m_w_out_b, m_w_out_c, m_w_out, m_norm_mix_g, m_norm_mlp_g, m_w_ff1, m_w_ff2, v_w_in, v_conv_w, v_a_log, v_dt_bias, v_gdn_norm_g, v_gmlp_ln_g, v_w_spatial, v_b_spatial, v_sba_q_g, v_sba_k_g, v_w_out_a, v_w_out_b, v_w_out_c, v_w_out, v_norm_mix_g, v_norm_mlp_g, v_w_ff1, v_w_ff2):
    depth = w_in.shape[0]
    _, t, d = x.shape
    n_heads = d // 256
    gw = n_heads * HD
    width = d // 2
    n_groups = width // HD
    off_gate, off_uv, off_c, off_gl = 3 * gw, 4 * gw, 4 * gw + 2 * width, 7 * gw + 2 * width
    off_ab = off_gl + 3 * d
    n_packed = off_ab + LANE
    n_in = off_ab + 2 * n_heads
    assert w_in.shape[2] * N_DEV == n_in and t % LANE == 0 and d % 256 == 0

    ix, iy, ic = lax.axis_index("x"), lax.axis_index("y"), lax.axis_index("c")
    dev = 4 * ix + 2 * iy + ic
    c_idx = jnp.reshape(ic, (1,)).astype(jnp.int32)
    xs, target = x[0], loss_target[0]

    w_in_t, m_in_t, v_in_t = (jnp.swapaxes(a, 1, 2) for a in (w_in, m_w_in, v_w_in))
    shard_of = dict(w_in=w_in_t, w_out_a=w_out_a, w_out_b=w_out_b, w_out_c=w_out_c, w_out=w_out, w_ff1=w_ff1, w_ff2=w_ff2)
    ag_jobs = lambda l: {nm: GatherJob(w[l].astype(MXU)) for nm, w in shard_of.items()}
    layer0 = ag_jobs(0)
    conv_job = GatherJob(conv_w)
    run_jobs(list(layer0.values()) + [conv_job], "ag_layer0")
    gathered_w = [{nm: job.results[0] for nm, job in layer0.items()}]
    conv_full = jnp.transpose(conv_job.results[0], (1, 2, 0, 3)).reshape(depth, CONV_K, 3 * gw)

    def pack_in(g):
        w = g.reshape(n_in, d)
        return jnp.concatenate([w[:3 * gw], w[3 * gw + 2 * n_heads:], w[3 * gw:3 * gw + 2 * n_heads],
                                jnp.zeros((LANE - 2 * n_heads, d), w.dtype)], axis=0)

    def unpack_in(wp):
        w = jnp.concatenate([wp[:3 * gw], wp[off_ab:off_ab + 2 * n_heads], wp[3 * gw:off_ab]], axis=0)
        return w.reshape(N_DEV, n_in // N_DEV, d)

    alog_p, dtb_p = _pad_lanes(a_log), _pad_lanes(dt_bias)
    bt_all = jnp.pad(jnp.transpose(b_spatial, (0, 2, 1)), ((0, 0), (0, 0), (0, LANE - n_groups)))

    saved = []
    cur = xs
    for l in range(depth):
        gl_w = gathered_w[l]
        lw = dict(w_in=pack_in(gl_w["w_in"]), conv=conv_full[l], alog=alog_p[l:l + 1], dtb=dtb_p[l:l + 1],
                  gng=gdn_norm_g[l:l + 1], lng=gmlp_ln_g[l:l + 1], ws=w_spatial[l], bt=bt_all[l],
                  qg=sba_q_g[l:l + 1], kg=sba_k_g[l:l + 1], gmix=norm_mix_g[l:l + 1], gmlp=norm_mlp_g[l:l + 1])
        nxt = ag_jobs(l + 1) if l + 1 < depth else {}
        ride = lambda *names: [nxt[nm] for nm in names if nm in nxt]
        s = dict(lw=lw, x=cur, gw=gl_w)
        s["h1"] = rms_fwd(cur, lw["gmix"], "rms_mix")
        z = s["z"] = matmul(s["h1"], lw["w_in"], "nt", F32, "mm_in", caps=(1024, 1408, 512), jobs=ride("w_ff2"))
        s["pre"] = gdn_pre_fwd(z, off_ab // LANE, lw["conv"], lw["alog"], lw["dtb"], n_heads, "gdn_pre")
        s["chunks"] = gdn_chunk_fwd(*s["pre"], "gdn_chunk", jobs=ride("w_in"))
        s["o"], s["states"] = gdn_scan_fwd(s["chunks"], "gdn_scan")
        s["oa"] = gdn_post_fwd(s["o"], z, off_gate // HD, lw["gng"], "gdn_post")
        s["ob"] = gmlp_fwd(z, off_uv, width, lw["lng"], lw["ws"], lw["bt"], "gmlp")
        s["qkv_c"] = sba_pre_fwd(z, off_c // HD, n_heads, lw["qg"], lw["kg"], "sba_pre")
        s["oc"], s["tot"] = sba_fwd(*s["qkv_c"], "sba", jobs=ride("w_ff1"))
        s["ba"] = matmul(s["oa"], gl_w["w_out_a"], "nn", F32, "mm_oa", b_view="cols")
        s["bb"] = matmul(s["ob"], gl_w["w_out_b"], "nn", F32, "mm_ob", b_view="cols")
        s["bc"] = matmul(s["oc"], gl_w["w_out_c"], "nn", F32, "mm_oc", b_view="cols")
        s["y"] = merge_fwd(z, off_gl, s["ba"], s["bb"], s["bc"], "merge")
        s["x1"] = matmul(s["y"], gl_w["w_out"], "nn", F32, "mm_out", res=cur, b_view="rows")
        s["h2"] = rms_fwd(s["x1"], lw["gmlp"], "rms_mlp")
        s["f"] = matmul(s["h2"], gl_w["w_ff1"], "nn", F32, "mm_ff1", b_view="cols", jobs=ride("w_out", "w_out_a"))
        s["a"] = sqrelu_fwd(s["f"], "sqrelu")
        cur = matmul(s["a"], gl_w["w_ff2"], "nn", F32, "mm_ff2", res=s["x1"], b_view="rows",
                     jobs=ride("w_out_b", "w_out_c"))
        if nxt:
            gathered_w.append({nm: job.results[0] for nm, job in nxt.items()})
        saved.append(s)

    dx, dxb, loss_tile = loss_head(cur, target, "loss_head")
    loss = lax.psum(loss_tile[0, 0], AXES)

    big = dict(w_in=(w_in_t, m_in_t, v_in_t), w_out_a=(w_out_a, m_w_out_a, v_w_out_a), w_out_b=(w_out_b, m_w_out_b, v_w_out_b),
               w_out_c=(w_out_c, m_w_out_c, v_w_out_c), w_out=(w_out, m_w_out, v_w_out), w_ff1=(w_ff1, m_w_ff1, v_w_ff1),
               w_ff2=(w_ff2, m_w_ff2, v_w_ff2))
    bufs = {nm: tuple(lax.empty(w.shape, F32) for _ in range(4)) for nm, (w, _, _) in big.items()}
    waiting = []

    def pair_reduce(nm, g8, l):
        w = big[nm][0]
        g8 = g8.reshape(N_DEV, w.shape[1], w.shape[2])
        r4 = rs_pair_exchange(g8, "rs_pair_" + nm)
        waiting.append((nm, l, ChipExchangeJob(pair_sum(g8, r4, c_idx, "rs_pair_sum_" + nm))))

    def take(*names):
        picked = [e for e in waiting if e[0] in names]
        for e in picked:
            waiting.remove(e)
        return picked

    def update(picked):
        for nm, l, job in picked:
            w, m, v = big[nm]
            bufs[nm] = tuple(adam_layer(job.results[0], w, m, v, bufs[nm], l, "adam_" + nm))

    small_grads = []
    for l in reversed(range(depth)):
        s = saved[l]
        lw, z, gl_w = s["lw"], s["z"], s["gw"]
        da = matmul(dxb, gl_w["w_ff2"], "nt", F32, "mm_ff2_dx", b_view="rows")
        pair_reduce("w_ff2", matmul(s["a"], dxb, "tn", MXU, "mm_ff2_dw"), l)
        df = sqrelu_bwd(s["f"], da, "sqrelu_bwd")
        dh2 = matmul(df, gl_w["w_ff1"], "nt", F32, "mm_ff1_dx", b_view="cols")
        pair_reduce("w_ff1", matmul(s["h2"], df, "tn", MXU, "mm_ff1_dw", out_slabs=w_ff1.shape[2]), l)
        dx1, dx1b, d_gmlp = rms_bwd(s["x1"], lw["gmlp"], dh2, dx, "rms_mlp_bwd")
        dy = matmul(dx1b, gl_w["w_out"], "nt", F32, "mm_out_dx", b_view="rows")
        pair_reduce("w_out", matmul(s["y"], dx1b, "tn", MXU, "mm_out_dw"), l)
        dgl, dba, dbb, dbc = merge_bwd(z, off_gl, s["ba"], s["bb"], s["bc"], dy, "merge_bwd")
        doa = matmul(dba, gl_w["w_out_a"], "nt", F32, "mm_oa_dx", b_view="cols")
        dob = matmul(dbb, gl_w["w_out_b"], "nt", F32, "mm_ob_dx", b_view="cols")
        doc = matmul(dbc, gl_w["w_out_c"], "nt", F32, "mm_oc_dx", b_view="cols")
        slab = w_out_a.shape[2]
        pair_reduce("w_out_a", matmul(s["oa"], dba, "tn", MXU, "mm_oa_dw", out_slabs=slab), l)
        pair_reduce("w_out_b", matmul(s["ob"], dbb, "tn", MXU, "mm_ob_dw", out_slabs=slab), l)
        pair_reduce("w_out_c", matmul(s["oc"], dbc, "tn", MXU, "mm_oc_dw", out_slabs=slab), l)
        riding = take("w_ff2", "w_ff1", "w_out")
        dqc, dkc, dvc = sba_bwd(*s["qkv_c"], s["tot"], doc, "sba_bwd", jobs=[e[2] for e in riding])
        update(riding)
        dz_qc, dz_kc, dz_vc, d_qg, d_kg = sba_pre_bwd(z, off_c // HD, n_heads, lw["qg"], lw["kg"], dqc, dkc, dvc, "sba_pre_bwd")
        dz_uv, d_lng, d_ws, d_bt = gmlp_bwd(z, off_uv, width, lw["lng"], lw["ws"], lw["bt"], dob, "gmlp_bwd")
        do, dz_gate, d_gng = gdn_post_bwd(s["o"], z, off_gate // HD, lw["gng"], doa, "gdn_post_bwd")
        chunk_cts = gdn_scan_bwd(s["chunks"], s["states"], do, "gdn_scan_bwd")
        riding = take("w_in", "w_out_a", "w_out_b", "w_out_c")
        dqa, dka, dva, dga, dba_ = gdn_chunk_bwd(*s["pre"], chunk_cts, "gdn_chunk_bwd", jobs=[e[2] for e in riding])
        update(riding)
        dz_q, dz_k, dz_v, d_ab, d_cq, d_ck, d_cv, d_alog, d_dtb = gdn_pre_bwd(
            z, off_ab // LANE, lw["conv"], lw["alog"], lw["dtb"], n_heads, dqa, dka, dva, dga, dba_, "gdn_pre_bwd")
        dz = jnp.concatenate([dz_q, dz_k, dz_v, dz_gate, dz_uv, dz_qc, dz_kc, dz_vc, dgl[0], dgl[1], dgl[2],
                              d_ab.astype(MXU)], axis=1)
        dh1 = matmul(dz, lw["w_in"], "nn", F32, "mm_in_dx", caps=(1024, 1024, 1408))
        pair_reduce("w_in", unpack_in(matmul(dz, s["h1"], "tn", MXU, "mm_in_dw", caps=(1408, 1024, 512))), l)
        dx, dxb, d_gmix = rms_bwd(s["x"], lw["gmix"], dh1, dx1, "rms_mix_bwd")
        small_grads.append(dict(
            conv_w=jnp.concatenate([d_cq, d_ck, d_cv], axis=1), a_log=d_alog[0, :n_heads], dt_bias=d_dtb[0, :n_heads],
            gdn_norm_g=d_gng[0], gmlp_ln_g=d_lng[0], w_spatial=d_ws, b_spatial=jnp.transpose(d_bt[:, :n_groups]),
            sba_q_g=d_qg[0], sba_k_g=d_kg[0], norm_mix_g=d_gmix[0], norm_mlp_g=d_gmlp[0]))
    small_grads = small_grads[::-1]
    rest = take("w_in")
    run_jobs([e[2] for e in rest], "rs_chip_last")
    update(rest)

    rep_names = ["a_log", "dt_bias", "gdn_norm_g", "gmlp_ln_g", "w_spatial", "b_spatial", "sba_q_g", "sba_k_g",
                 "norm_mix_g", "norm_mlp_g"]
    rep = dict(a_log=(a_log, m_a_log, v_a_log), dt_bias=(dt_bias, m_dt_bias, v_dt_bias),
               gdn_norm_g=(gdn_norm_g, m_gdn_norm_g, v_gdn_norm_g), gmlp_ln_g=(gmlp_ln_g, m_gmlp_ln_g, v_gmlp_ln_g),
               w_spatial=(w_spatial, m_w_spatial, v_w_spatial), b_spatial=(b_spatial, m_b_spatial, v_b_spatial),
               sba_q_g=(sba_q_g, m_sba_q_g, v_sba_q_g), sba_k_g=(sba_k_g, m_sba_k_g, v_sba_k_g),
               norm_mix_g=(norm_mix_g, m_norm_mix_g, v_norm_mix_g), norm_mlp_g=(norm_mlp_g, m_norm_mlp_g, v_norm_mlp_g))
    stack = lambda nm: jnp.stack([sg[nm] for sg in small_grads])
    conv_cols = conv_w.shape[2]
    conv_pad = jnp.zeros((depth, CONV_K, 3 * gw - conv_cols), F32)
    widen = lambda a: jnp.concatenate([a, conv_pad], axis=2)
    grads_packed = _pack([stack(nm) for nm in rep_names] + [stack("conv_w")])
    small_job = GatherJob(grads_packed)
    run_jobs([small_job], "ag_small_grads")
    gathered = small_job.results[0]
    packed = [_pack([rep[nm][k] for nm in rep_names] + [widen((conv_w, m_conv_w, v_conv_w)[k])]) for k in range(3)]
    shapes = [rep[nm][0].shape for nm in rep_names] + [(depth, CONV_K, 3 * gw)]
    flat = gathered.reshape(N_DEV, -1)
    n_rep = sum(int(rep[nm][0].size) for nm in rep_names)
    conv_part = flat[:, n_rep:n_rep + depth * CONV_K * 3 * gw].reshape(N_DEV, depth, CONV_K, 3 * gw)
    conv_mine = lax.dynamic_slice_in_dim(conv_part, dev * conv_cols, conv_cols, axis=3)
    conv_mine = jnp.concatenate([conv_mine, jnp.zeros((N_DEV, depth, CONV_K, 3 * gw - conv_cols), F32)], axis=3)
    tail = flat[:, n_rep + depth * CONV_K * 3 * gw:]
    parts = jnp.concatenate([flat[:, :n_rep], conv_mine.reshape(N_DEV, -1), tail], axis=1).reshape(gathered.shape)
    outs_small = small_adam(parts, packed[0], packed[1], packed[2], "adam_small")
    small = [dict(zip(rep_names + ["conv_w"], _unpack(o, shapes))) for o in outs_small]
    for sm in small:
        sm["conv_w"] = sm["conv_w"][:, :, :conv_cols]

    order = ["w_in", "conv_w", "a_log", "dt_bias", "gdn_norm_g", "gmlp_ln_g", "w_spatial", "b_spatial", "sba_q_g",
             "sba_k_g", "w_out_a", "w_out_b", "w_out_c", "w_out", "norm_mix_g", "norm_mlp_g", "w_ff1", "w_ff2"]
    result = [loss, dx[None]]
    for kind in range(4):
        for nm in order:
            if nm == "w_in":
                result.append(jnp.swapaxes(bufs[nm][kind], 1, 2))
            else:
                result.append(bufs[nm][kind] if nm in bufs else small[kind][nm])
    return tuple(result)
```

```python
import functools

import jax
import jax.numpy as jnp
from jax import lax
from jax.experimental import pallas as pl
from jax.experimental.pallas import tpu as pltpu

F32 = jnp.float32
MXU = jnp.bfloat16
N_DEV = 8
AXES = ("x", "y", "c")
CHUNK = 64
HD = 128
CONV_K = 4
EPS = 1e-6
LANE = 128
VMEM_LIMIT = 56 * 1024 * 1024
ADAM_LR, ADAM_B1, ADAM_B2, ADAM_EPS, ADAM_WD, ADAM_STEP = 0.001, 0.9, 0.999, 1e-08, 0.01, 10

_ANY = pl.BlockSpec(memory_space=pl.ANY)
_MESH = pl.DeviceIdType.MESH
_DN = {"nn": (((1,), (0,)), ((), ())), "nt": (((1,), (1,)), ((), ())), "tn": (((0,), (0,)), ((), ()))}


def _cp(*sem):
    return pltpu.CompilerParams(dimension_semantics=sem, vmem_limit_bytes=VMEM_LIMIT)


def _tile(n, cap):
    if n <= cap:
        return n
    best = LANE
    for t in range(LANE, cap + 1, LANE):
        if n % t == 0:
            best = t
    assert n % best == 0, (n, cap)
    return best


def _split(x, n):
    parts, rest = [], x.astype(F32)
    for _ in range(n):
        p = rest.astype(MXU)
        parts.append(p)
        rest = rest - p.astype(F32)
    return parts


def _mm_raw(a, b, mode, prec):
    dot = lambda p, q: lax.dot_general(p, q, _DN[mode], preferred_element_type=F32)
    if prec == "lo" or MXU == F32:
        return dot(a.astype(MXU), b.astype(MXU))
    if prec == "x3":
        (ah, al), (bh, bl) = _split(a, 2), _split(b, 2)
        return dot(ah, bh) + (dot(al, bh) + dot(ah, bl))
    if prec == "la":
        ae, (bh, bl) = a.astype(MXU), _split(b, 2)
        return dot(ae, bh) + dot(ae, bl)
    assert prec == "rb"
    (ah, al), be = _split(a, 2), b.astype(MXU)
    return dot(ah, be) + dot(al, be)


@functools.partial(jax.custom_vjp, nondiff_argnums=(2, 3))
def mm(a, b, mode, prec):
    return _mm_raw(a, b, mode, prec)


def _mm_fwd(a, b, mode, prec):
    return _mm_raw(a, b, mode, prec), (a, b)


def _mm_bwd(mode, prec, res, ct):
    a, b = res
    pa = {"la": None, "rb": "rb"}.get(prec, prec)
    pb = {"la": "la", "rb": None}.get(prec, prec)
    if mode == "nn":
        da = _mm_raw(ct, b, "nt", pa) if pa else None
        db = _mm_raw(a, ct, "tn", pb) if pb else None
    elif mode == "nt":
        da = _mm_raw(ct, b, "nn", pa) if pa else None
        db = _mm_raw(ct, a, "tn", {"la": "rb"}.get(pb, pb)) if pb else None
    else:
        da = _mm_raw(b, ct, "nt", {"rb": "la"}.get(pa, pa)) if pa else None
        db = _mm_raw(a, ct, "nn", pb) if pb else None
    da = jnp.zeros_like(a) if da is None else da.astype(a.dtype)
    db = jnp.zeros_like(b) if db is None else db.astype(b.dtype)
    return da, db


mm.defvjp(_mm_fwd, _mm_bwd)


def _shift_rows(x, j):
    n = x.shape[0]
    row = lax.broadcasted_iota(jnp.int32, x.shape, 0)
    if j > 0:
        return jnp.where(row >= j, pltpu.roll(x, j, 0), 0.0)
    return jnp.where(row < n + j, pltpu.roll(x, n + j, 0), 0.0)


@functools.partial(jax.custom_vjp, nondiff_argnums=(1,))
def shift(x, j):
    return _shift_rows(x, j)


shift.defvjp(lambda x, j: (_shift_rows(x, j), None), lambda j, _, ct: (_shift_rows(ct, -j),))


def _sigmoid(x):
    return 1.0 / (1.0 + jnp.exp(-x))


def _silu(x):
    return x * _sigmoid(x)


def _softplus(x):
    return jnp.maximum(x, 0.0) + jnp.log(1.0 + jnp.exp(-jnp.abs(x)))


def _logsig(x):
    return jnp.minimum(x, 0.0) - jnp.log(1.0 + jnp.exp(-jnp.abs(x)))


def _gelu(x):
    return 0.5 * x * (1.0 + lax.erf(x * (2.0 ** -0.5)))


def _rms(x, g):
    return x * lax.rsqrt(jnp.mean(x * x, axis=-1, keepdims=True) + EPS) * g


def _iota2(shape, dim):
    return lax.broadcasted_iota(jnp.int32, shape, dim)


class GatherJob:
    def __init__(self, x):
        self.inputs = [x]
        self.out_shapes = [jax.ShapeDtypeStruct((N_DEV,) + x.shape, x.dtype)]
        self.sems = [pltpu.SemaphoreType.DMA((7,)), pltpu.SemaphoreType.DMA((7,)), pltpu.SemaphoreType.DMA(())]
        self.results = None

    @staticmethod
    def _plan(ins, outs, sems):
        (x_ref,), (out_ref,), (send_sems, recv_sems, local_sem) = ins, outs, sems
        ix, iy, ic = lax.axis_index("x"), lax.axis_index("y"), lax.axis_index("c")
        me, sibling = (ix, iy, ic), (ix, iy, 1 - ic)
        chips = [(1 - ix, iy), (ix, 1 - iy), (1 - ix, 1 - iy)]

        def slot(px, py, pc):
            return out_ref.at[4 * px + 2 * py + pc]

        def copy(k, block, to, src=None):
            return pltpu.make_async_remote_copy(
                src_ref=slot(*block) if src is None else src, dst_ref=slot(*block),
                send_sem=send_sems.at[k], recv_sem=recv_sems.at[k], device_id=to, device_id_type=_MESH)

        mine = pltpu.make_async_copy(x_ref, slot(*me), local_sem)
        first = [copy(0, me, sibling, src=x_ref)]
        first += [copy(1 + j, me, (*chip, ic), src=x_ref) for j, chip in enumerate(chips)]
        return ic, me, sibling, chips, copy, mine, first

    def start(self, ins, outs, sems):
        *_, mine, first = self._plan(ins, outs, sems)
        mine.start()
        for cp in first:
            cp.start()

    def finish(self, ins, outs, sems):
        ic, me, sibling, chips, copy, mine, first = self._plan(ins, outs, sems)
        passed = [copy(4 + j, (*chip, ic), sibling) for j, chip in enumerate(chips)]
        for j, chip in enumerate(chips):
            copy(1 + j, (*chip, ic), me).wait_recv()
            passed[j].start()
        copy(0, sibling, me).wait_recv()
        for j, chip in enumerate(chips):
            copy(4 + j, (*chip, 1 - ic), me).wait_recv()
        for cp in first + passed:
            cp.wait_send()
        mine.wait()


class ChipExchangeJob:
    def __init__(self, p4):
        self.inputs = [p4]
        self.out_shapes = [jax.ShapeDtypeStruct(p4.shape, p4.dtype)]
        self.sems = [pltpu.SemaphoreType.DMA((3,)), pltpu.SemaphoreType.DMA((3,)), pltpu.SemaphoreType.DMA(())]
        self.results = None

    @staticmethod
    def _plan(ins, outs, sems):
        (p_ref,), (r_ref,), (send_sems, recv_sems, local_sem) = ins, outs, sems
        ix, iy, ic = lax.axis_index("x"), lax.axis_index("y"), lax.axis_index("c")
        my_xy = 2 * ix + iy
        local = pltpu.make_async_copy(p_ref.at[my_xy], r_ref.at[my_xy], local_sem)
        chips = [(1 - ix, iy), (ix, 1 - iy), (1 - ix, 1 - iy)]
        copies = [
            pltpu.make_async_remote_copy(
                src_ref=p_ref.at[2 * px + py], dst_ref=r_ref.at[my_xy],
                send_sem=send_sems.at[k], recv_sem=recv_sems.at[k],
                device_id=(px, py, ic), device_id_type=_MESH)
            for k, (px, py) in enumerate(chips)
        ]
        return local, copies

    def start(self, ins, outs, sems):
        local, copies = self._plan(ins, outs, sems)
        local.start()
        for cp in copies:
            cp.start()

    def finish(self, ins, outs, sems):
        local, copies = self._plan(ins, outs, sems)
        for cp in copies:
            cp.wait()
        local.wait()


def _each_job(jobs, method, ins, outs, sems):
    i = o = s = 0
    for job in jobs:
        ni, no, ns = len(job.inputs), len(job.out_shapes), len(job.sems)
        getattr(job, method)(ins[i:i + ni], outs[o:o + no], sems[s:s + ns])
        i, o, s = i + ni, o + no, s + ns


def run_jobs(jobs, name):
    j_in = [a for job in jobs for a in job.inputs]
    j_out = [sh for job in jobs for sh in job.out_shapes]
    j_sem = [sm for job in jobs for sm in job.sems]

    def body(*refs):
        ins, outs, sems = refs[:len(j_in)], refs[len(j_in):len(j_in) + len(j_out)], refs[len(j_in) + len(j_out):]
        _each_job(jobs, "start", ins, outs, sems)
        _each_job(jobs, "finish", ins, outs, sems)

    res = pl.pallas_call(body, name=name, out_shape=j_out, in_specs=[_ANY] * len(j_in), out_specs=[_ANY] * len(j_out),
                         scratch_shapes=j_sem)(*j_in)
    _hand_out(jobs, res)


def _hand_out(jobs, res):
    o = 0
    for job in jobs:
        job.results = list(res[o:o + len(job.out_shapes)])
        o += len(job.out_shapes)


def hosted_call(body, args, *, name, grid, in_specs, out_specs, out_shape, scratch_shapes=(), sem=None, jobs=()):
    outs_l, specs_l = list(out_shape), list(out_specs)
    if not jobs:
        return pl.pallas_call(body, name=name, grid=grid, in_specs=list(in_specs), out_specs=specs_l, out_shape=outs_l,
                              scratch_shapes=list(scratch_shapes), compiler_params=_cp(*sem))(*args)
    j_in = [a for job in jobs for a in job.inputs]
    j_out = [sh for job in jobs for sh in job.out_shapes]
    j_sem = [sm for job in jobs for sm in job.sems]
    n_in, n_out, n_scr = len(in_specs), len(outs_l), len(scratch_shapes)

    def wrapped(*refs):
        pos = [0]

        def take(n):
            pos[0] += n
            return refs[pos[0] - n:pos[0]]

        ins, jin, outs, jout, scr, jsem = take(n_in), take(len(j_in)), take(n_out), take(len(j_out)), take(n_scr), take(len(j_sem))
        ids = [pl.program_id(a) for a in range(len(grid))]
        first = functools.reduce(lambda p, q: p & q, [i == 0 for i in ids])
        last = functools.reduce(lambda p, q: p & q, [i == g - 1 for i, g in zip(ids, grid)])

        @pl.when(first)
        def _():
            _each_job(jobs, "start", jin, jout, jsem)

        body(*ins, *outs, *scr)

        @pl.when(last)
        def _():
            _each_job(jobs, "finish", jin, jout, jsem)

    res = pl.pallas_call(
        wrapped, name=name, grid=grid, in_specs=list(in_specs) + [_ANY] * len(j_in),
        out_specs=specs_l + [_ANY] * len(j_out), out_shape=outs_l + j_out,
        scratch_shapes=list(scratch_shapes) + j_sem, compiler_params=_cp(*["arbitrary"] * len(grid)),
    )(*args, *j_in)
    _hand_out(jobs, res[n_out:])
    return list(res[:n_out])


def rs_pair_exchange(g8, name):
    def body(g_ref, r_ref, send_sems, recv_sems):
        ix, iy, ic = lax.axis_index("x"), lax.axis_index("y"), lax.axis_index("c")
        copies = [
            pltpu.make_async_remote_copy(
                src_ref=g_ref.at[2 * xy + (1 - ic)], dst_ref=r_ref.at[xy],
                send_sem=send_sems.at[xy], recv_sem=recv_sems.at[xy],
                device_id=(ix, iy, 1 - ic), device_id_type=_MESH)
            for xy in range(4)
        ]
        for cp in copies:
            cp.start()
        for cp in copies:
            cp.wait()

    return pl.pallas_call(
        body, name=name, out_shape=jax.ShapeDtypeStruct((4,) + g8.shape[1:], g8.dtype),
        in_specs=[_ANY], out_specs=_ANY,
        scratch_shapes=[pltpu.SemaphoreType.DMA((4,)), pltpu.SemaphoreType.DMA((4,))],
    )(g8)


def pair_sum(g8, r4, c_idx, name):
    _, rows, cols = g8.shape
    tr, tc = _tile_2d(rows, cols)

    def body(c_ref, g_ref, r_ref, o_ref):
        o_ref[...] = (g_ref[...].astype(F32) + r_ref[...].astype(F32)).astype(o_ref.dtype)

    grid_spec = pltpu.PrefetchScalarGridSpec(
        num_scalar_prefetch=1, grid=(4, rows // tr, cols // tc),
        in_specs=[pl.BlockSpec((None, tr, tc), lambda s, i, j, c: (2 * s + c[0], i, j)),
                  pl.BlockSpec((None, tr, tc), lambda s, i, j, c: (s, i, j))],
        out_specs=pl.BlockSpec((None, tr, tc), lambda s, i, j, c: (s, i, j)))
    return pl.pallas_call(
        body, name=name, grid_spec=grid_spec, out_shape=jax.ShapeDtypeStruct((4, rows, cols), g8.dtype),
        compiler_params=_cp("parallel", "parallel", "parallel"),
    )(c_idx, g8, r4)


def _tile_2d(rows, cols):
    budget = 128 * 2048
    tr, tc = rows, cols
    if rows % 16 == 0:
        while tr * cols > budget and tr % 32 == 0:
            tr //= 2
    else:
        while rows * tc > budget and tc % (2 * LANE) == 0:
            tc //= 2
    return tr, tc


def _adam_math(w, g, m, v):
    m2 = ADAM_B1 * m + (1.0 - ADAM_B1) * g
    v2 = ADAM_B2 * v + (1.0 - ADAM_B2) * (g * g)
    m_hat = m2 / (1.0 - ADAM_B1 ** ADAM_STEP)
    v_hat = v2 / (1.0 - ADAM_B2 ** ADAM_STEP)
    delta = -ADAM_LR * (m_hat / (jnp.sqrt(v_hat) + ADAM_EPS) + ADAM_WD * w)
    return delta, m2, v2


def adam_layer(parts, w, m, v, bufs, layer, name):
    n_parts, rows, cols = parts.shape
    tr, tc = _tile_2d(rows, cols)

    def body(p_ref, w_ref, m_ref, v_ref, g_in, d_in, m_in, v_in, g_out, d_out, m_out, v_out):
        g = p_ref[0].astype(F32)
        for k in range(1, n_parts):
            g = g + p_ref[k].astype(F32)
        delta, m2, v2 = _adam_math(w_ref[...], g, m_ref[...], v_ref[...])
        g_out[...] = g
        d_out[...] = delta
        m_out[...] = m2
        v_out[...] = v2

    lay = pl.BlockSpec((None, tr, tc), lambda i, j: (layer, i, j))
    return pl.pallas_call(
        body, name=name, grid=(rows // tr, cols // tc),
        in_specs=[pl.BlockSpec((n_parts, tr, tc), lambda i, j: (0, i, j)), lay, lay, lay, _ANY, _ANY, _ANY, _ANY],
        out_specs=[lay, lay, lay, lay],
        out_shape=[jax.ShapeDtypeStruct(w.shape, F32)] * 4,
        input_output_aliases={4: 0, 5: 1, 6: 2, 7: 3},
        compiler_params=_cp("parallel", "parallel"),
    )(parts, w, m, v, *bufs)


def matmul(a, b, mode, out_dtype, name, res=None, caps=(1024, 1024, 2048), b_view=None, out_slabs=None, jobs=()):
    if mode == "tn":
        k_dim, m_dim = a.shape
    else:
        m_dim, k_dim = a.shape
    if b_view is None:
        b_rows, b_cols = b.shape
    else:
        kind = b_view
        shard_r, shard_c = b.shape[1:]
        b_rows, b_cols = (shard_r, N_DEV * shard_c) if kind == "cols" else (N_DEV * shard_r, shard_c)
    n_dim = b_rows if mode == "nt" else b_cols
    cap_n, cap_k = caps[1], caps[2]
    tn = _tile(out_slabs, cap_n) if out_slabs else _tile(n_dim, cap_n)
    tk = _tile(k_dim, cap_k)
    stacked = 0
    if b_view is not None:
        along_n = (kind == "cols") == (mode == "nn")
        shard_len = shard_c if kind == "cols" else shard_r
        if along_n:
            tn = _tile(shard_len, cap_n)
        elif kind == "rows" and tk >= shard_r and shard_r % 16 == 0:
            stacked = tk // shard_r
        else:
            tk = _tile(shard_len, cap_k)
    tm = _tile(m_dim, caps[0])
    nk = k_dim // tk

    def body(*refs):
        a_ref, b_ref = refs[:2]
        r_ref = refs[2] if res is not None else None
        o_ref = refs[3 if res is not None else 2]

        def emit(r):
            if res is not None:
                r = r + r_ref[...]
            o_ref[...] = r.astype(out_dtype)

        b_val = b_ref[...].reshape(tk, tn) if stacked else b_ref[...]
        part = lax.dot_general(a_ref[...], b_val, _DN[mode], preferred_element_type=F32)
        if nk == 1:
            emit(part)
            return
        acc = refs[-1]
        k = pl.program_id(2)

        @pl.when(k == 0)
        def _():
            acc[...] = part

        @pl.when(k > 0)
        def _():
            acc[...] += part

        @pl.when(k == nk - 1)
        def _():
            emit(acc[...])

    a_spec = pl.BlockSpec((tk, tm), lambda i, j, k: (k, i)) if mode == "tn" else pl.BlockSpec((tm, tk), lambda i, j, k: (i, k))
    b_blk = (tn, tk) if mode == "nt" else (tk, tn)
    b_pos = (lambda i, j, k: (j, k)) if mode == "nt" else (lambda i, j, k: (k, j))
    if b_view is None:
        b_spec = pl.BlockSpec(b_blk, b_pos)
    elif stacked:
        b_spec = pl.BlockSpec((stacked, shard_r, tn), lambda i, j, k: (k, 0, j))
    elif kind == "cols":
        per = shard_c // b_blk[1]
        b_spec = pl.BlockSpec((None,) + b_blk, lambda i, j, k: (b_pos(i, j, k)[1] // per, b_pos(i, j, k)[0], b_pos(i, j, k)[1] % per))
    else:
        per = shard_r // b_blk[0]
        b_spec = pl.BlockSpec((None,) + b_blk, lambda i, j, k: (b_pos(i, j, k)[0] // per, b_pos(i, j, k)[0] % per, b_pos(i, j, k)[1]))
    if out_slabs:
        per_o = out_slabs // tn
        o_spec = pl.BlockSpec((None, tm, tn), lambda i, j, k: (j // per_o, i, j % per_o))
        out_shape = jax.ShapeDtypeStruct((n_dim // out_slabs, m_dim, out_slabs), out_dtype)
    else:
        o_spec = pl.BlockSpec((tm, tn), lambda i, j, k: (i, j))
        out_shape = jax.ShapeDtypeStruct((m_dim, n_dim), out_dtype)
    in_specs, args = [a_spec, b_spec], [a, b]
    if res is not None:
        in_specs.append(o_spec)
        args.append(res)
    return hosted_call(
        body, args, name=name, grid=(m_dim // tm, n_dim // tn, nk), in_specs=in_specs, out_specs=[o_spec],
        out_shape=[out_shape], scratch_shapes=[pltpu.VMEM((tm, tn), F32)] if nk > 1 else [],
        sem=("parallel", "parallel", "arbitrary"), jobs=jobs)[0]


def rms_fwd(x, gain, name):
    t, d = x.shape
    tt = _tile(t, 256)

    def body(x_ref, g_ref, o_ref):
        o_ref[...] = _rms(x_ref[...], g_ref[...]).astype(o_ref.dtype)

    return pl.pallas_call(
        body, name=name, grid=(t // tt,),
        in_specs=[pl.BlockSpec((tt, d), lambda i: (i, 0)), pl.BlockSpec((1, d), lambda i: (0, 0))],
        out_specs=pl.BlockSpec((tt, d), lambda i: (i, 0)),
        out_shape=jax.ShapeDtypeStruct((t, d), MXU), compiler_params=_cp("parallel"),
    )(x, gain)


def rms_bwd(x, gain, dh, dres, name):
    t, d = x.shape
    tt = _tile(t, 256)

    def body(x_ref, g_ref, dh_ref, dr_ref, dx_ref, dxb_ref, dg_ref):
        _, vjp = jax.vjp(_rms, x_ref[...], g_ref[...])
        dx, dg = vjp(dh_ref[...])
        dx = dx + dr_ref[...]
        dx_ref[...] = dx
        dxb_ref[...] = dx.astype(dxb_ref.dtype)

        @pl.when(pl.program_id(0) == 0)
        def _():
            dg_ref[...] = jnp.zeros_like(dg_ref)

        dg_ref[...] += dg

    row = pl.BlockSpec((tt, d), lambda i: (i, 0))
    vec = pl.BlockSpec((1, d), lambda i: (0, 0))
    return pl.pallas_call(
        body, name=name, grid=(t // tt,), in_specs=[row, vec, row, row], out_specs=[row, row, vec],
        out_shape=[jax.ShapeDtypeStruct((t, d), F32), jax.ShapeDtypeStruct((t, d), MXU), jax.ShapeDtypeStruct((1, d), F32)],
        compiler_params=_cp("arbitrary"),
    )(x, gain, dh, dres)


def sqrelu_fwd(f, name):
    t, n = f.shape
    tt, tn = _tile(t, 256), _tile(n, 2048)

    def body(f_ref, o_ref):
        r = jnp.maximum(f_ref[...], 0.0)
        o_ref[...] = (r * r).astype(o_ref.dtype)

    blk = pl.BlockSpec((tt, tn), lambda i, j: (i, j))
    return pl.pallas_call(body, name=name, grid=(t // tt, n // tn), in_specs=[blk], out_specs=blk,
                          out_shape=jax.ShapeDtypeStruct((t, n), MXU), compiler_params=_cp("parallel", "parallel"))(f)


def sqrelu_bwd(f, da, name):
    t, n = f.shape
    tt, tn = _tile(t, 256), _tile(n, 2048)

    def body(f_ref, da_ref, o_ref):
        o_ref[...] = (da_ref[...] * (2.0 * jnp.maximum(f_ref[...], 0.0))).astype(o_ref.dtype)

    blk = pl.BlockSpec((tt, tn), lambda i, j: (i, j))
    return pl.pallas_call(body, name=name, grid=(t // tt, n // tn), in_specs=[blk, blk], out_specs=blk,
                          out_shape=jax.ShapeDtypeStruct((t, n), MXU), compiler_params=_cp("parallel", "parallel"))(f, da)


def loss_head(x, target, name):
    t, d = x.shape
    tt = _tile(t, 256)

    def body(x_ref, t_ref, dx_ref, dxb_ref, l_ref):
        e = x_ref[...] - t_ref[...]
        dx = e * (1.0 / d)
        dx_ref[...] = dx
        dxb_ref[...] = dx.astype(dxb_ref.dtype)

        @pl.when(pl.program_id(0) == 0)
        def _():
            l_ref[...] = jnp.zeros_like(l_ref)

        part = jnp.sum(jnp.sum(e * e, axis=-1, keepdims=True) * (1.0 / d), axis=0, keepdims=True)
        l_ref[...] += 0.5 * part

    row = pl.BlockSpec((tt, d), lambda i: (i, 0))
    return pl.pallas_call(
        body, name=name, grid=(t // tt,), in_specs=[row, row],
        out_specs=[row, row, pl.BlockSpec((8, LANE), lambda i: (0, 0))],
        out_shape=[jax.ShapeDtypeStruct((t, d), F32), jax.ShapeDtypeStruct((t, d), MXU), jax.ShapeDtypeStruct((8, LANE), F32)],
        compiler_params=_cp("arbitrary"),
    )(x, target)


def _merge_f(g0, g1, g2, ba, bb, bc):
    return _sigmoid(g0) * ba + _sigmoid(g1) * bb + _sigmoid(g2) * bc


def merge_fwd(z, off, ba, bb, bc, name):
    t, d = ba.shape
    tt, td = _tile(t, 256), _tile(d // 2, 1024)
    nd, ob = d // td, off // td

    def body(g0, g1, g2, a, b, c, o_ref):
        o_ref[...] = _merge_f(g0[...], g1[...], g2[...], a[...], b[...], c[...]).astype(o_ref.dtype)

    gates = [pl.BlockSpec((tt, td), functools.partial(lambda i, j, s: (i, ob + s * nd + j), s=s)) for s in range(3)]
    blk = pl.BlockSpec((tt, td), lambda i, j: (i, j))
    return pl.pallas_call(body, name=name, grid=(t // tt, nd), in_specs=gates + [blk] * 3, out_specs=blk,
                          out_shape=jax.ShapeDtypeStruct((t, d), MXU), compiler_params=_cp("parallel", "parallel"))(z, z, z, ba, bb, bc)


def merge_bwd(z, off, ba, bb, bc, dy, name):
    t, d = ba.shape
    tt, td = _tile(t, 256), _tile(d // 2, 1024)
    nd, ob = d // td, off // td

    def body(g0, g1, g2, a, b, c, dy_ref, dgl, da, db, dc):
        _, vjp = jax.vjp(_merge_f, g0[...], g1[...], g2[...], a[...], b[...], c[...])
        d0, d1, d2, xa, xb, xc = vjp(dy_ref[...])
        for s, dv in enumerate((d0, d1, d2)):
            dgl[s] = dv.astype(dgl.dtype)
        da[...] = xa.astype(da.dtype)
        db[...] = xb.astype(db.dtype)
        dc[...] = xc.astype(dc.dtype)

    gates = [pl.BlockSpec((tt, td), functools.partial(lambda i, j, s: (i, ob + s * nd + j), s=s)) for s in range(3)]
    blk = pl.BlockSpec((tt, td), lambda i, j: (i, j))
    dgl, da, db, dc = pl.pallas_call(
        body, name=name, grid=(t // tt, nd), in_specs=gates + [blk] * 4,
        out_specs=[pl.BlockSpec((3, tt, td), lambda i, j: (0, i, j)), blk, blk, blk],
        out_shape=[jax.ShapeDtypeStruct((3, t, d), MXU)] + [jax.ShapeDtypeStruct((t, d), MXU)] * 3,
        compiler_params=_cp("parallel", "parallel"),
    )(z, z, z, ba, bb, bc, dy)
    return dgl, da, db, dc


def _gdn_pre_f(qp, kp, vp, ab, cq, ck, cv, alog, dtb, h, n_heads):
    def conv(xp, cw):
        acc = xp * cw[CONV_K - 1]
        for j in range(1, CONV_K):
            acc = acc + shift(xp, j) * cw[CONV_K - 1 - j]
        return _silu(acc)

    q, k, v = conv(qp, cq), conv(kp, ck), conv(vp, cv)
    q = q * lax.rsqrt(jnp.sum(q * q, axis=-1, keepdims=True) + EPS) * (HD ** -0.5)
    k = k * lax.rsqrt(jnp.sum(k * k, axis=-1, keepdims=True) + EPS)
    lane = _iota2(ab.shape, 1)
    a_col = jnp.sum(jnp.where(lane == h, ab, 0.0), axis=-1, keepdims=True)
    b_col = jnp.sum(jnp.where(lane == n_heads + h, ab, 0.0), axis=-1, keepdims=True)
    lane1 = _iota2(alog.shape, 1)
    al = jnp.sum(jnp.where(lane1 == h, alog, 0.0), axis=-1, keepdims=True)
    dt = jnp.sum(jnp.where(lane1 == h, dtb, 0.0), axis=-1, keepdims=True)
    g = -jnp.exp(al) * _softplus(a_col + dt)
    return q, k, v, g, _sigmoid(b_col)


def _gdn_pre_specs(t, n_heads, ab_blk):
    zq = [pl.BlockSpec((t, HD), functools.partial(lambda h, s: (0, s * n_heads + h), s=s)) for s in range(3)]
    ab = pl.BlockSpec((t, LANE), lambda h: (0, ab_blk))
    cw = [pl.BlockSpec((CONV_K, HD), functools.partial(lambda h, s: (0, s * n_heads + h), s=s)) for s in range(3)]
    vec = pl.BlockSpec((1, LANE), lambda h: (0, 0))
    return zq, ab, cw, vec


def gdn_pre_fwd(z, ab_blk, conv_w, alog, dtb, n_heads, name):
    t = z.shape[0]
    zq, ab, cw, vec = _gdn_pre_specs(t, n_heads, ab_blk)

    def body(qp, kp, vp, ab_ref, cq, ck, cv, al, dt, q_o, k_o, v_o, g_o, b_o):
        rows = lambda r: tuple(r[j:j + 1, :] for j in range(CONV_K))
        outs = _gdn_pre_f(qp[...], kp[...], vp[...], ab_ref[...], rows(cq), rows(ck), rows(cv), al[...], dt[...],
                          pl.program_id(0), n_heads)
        for o_ref, val in zip((q_o, k_o, v_o, g_o, b_o), outs):
            o_ref[...] = val

    head = pl.BlockSpec((None, t, HD), lambda h: (h, 0, 0))
    col = pl.BlockSpec((None, t, 1), lambda h: (h, 0, 0))
    return pl.pallas_call(
        body, name=name, grid=(n_heads,), in_specs=zq + [ab] + cw + [vec, vec], out_specs=[head] * 3 + [col] * 2,
        out_shape=[jax.ShapeDtypeStruct((n_heads, t, HD), F32)] * 3 + [jax.ShapeDtypeStruct((n_heads, t, 1), F32)] * 2,
        compiler_params=_cp("parallel"),
    )(z, z, z, z, conv_w, conv_w, conv_w, alog, dtb)


def gdn_pre_bwd(z, ab_blk, conv_w, alog, dtb, n_heads, dq, dk, dv, dg, db, name):
    t = z.shape[0]
    gw = n_heads * HD
    zq, ab, cw, vec = _gdn_pre_specs(t, n_heads, ab_blk)

    def body(qp, kp, vp, ab_ref, cq, ck, cv, al, dt, dq_r, dk_r, dv_r, dg_r, db_r,
             dqp, dkp, dvp, dab, dcq, dck, dcv, dal, ddt):
        h = pl.program_id(0)
        rows = lambda r: tuple(r[j:j + 1, :] for j in range(CONV_K))
        f = functools.partial(_gdn_pre_f, h=h, n_heads=n_heads)
        _, vjp = jax.vjp(f, qp[...], kp[...], vp[...], ab_ref[...], rows(cq), rows(ck), rows(cv), al[...], dt[...])
        gq, gk, gv, gab, gcq, gck, gcv, gal, gdt = vjp((dq_r[...], dk_r[...], dv_r[...], dg_r[...], db_r[...]))
        dqp[...] = gq.astype(dqp.dtype)
        dkp[...] = gk.astype(dkp.dtype)
        dvp[...] = gv.astype(dvp.dtype)
        for ref, gr in ((dcq, gcq), (dck, gck), (dcv, gcv)):
            for j in range(CONV_K):
                ref[j:j + 1, :] = gr[j]

        @pl.when(h == 0)
        def _():
            dab[...] = jnp.zeros_like(dab)
            dal[...] = jnp.zeros_like(dal)
            ddt[...] = jnp.zeros_like(ddt)

        dab[...] += gab
        dal[...] += gal
        ddt[...] += gdt

    head = pl.BlockSpec((None, t, HD), lambda h: (h, 0, 0))
    col = pl.BlockSpec((None, t, 1), lambda h: (h, 0, 0))
    seg = pl.BlockSpec((t, HD), lambda h: (0, h))
    cseg = pl.BlockSpec((CONV_K, HD), lambda h: (0, h))
    return pl.pallas_call(
        body, name=name, grid=(n_heads,),
        in_specs=zq + [ab] + cw + [vec, vec] + [head] * 3 + [col] * 2,
        out_specs=[seg] * 3 + [pl.BlockSpec((t, LANE), lambda h: (0, 0))] + [cseg] * 3 + [vec, vec],
        out_shape=[jax.ShapeDtypeStruct((t, gw), MXU)] * 3 + [jax.ShapeDtypeStruct((t, LANE), F32)]
        + [jax.ShapeDtypeStruct((CONV_K, gw), F32)] * 3 + [jax.ShapeDtypeStruct((1, LANE), F32)] * 2,
        compiler_params=_cp("arbitrary"),
    )(z, z, z, z, conv_w, conv_w, conv_w, alog, dtb, dq, dk, dv, dg, db)


def _gdn_chunk_f(q, k, v, g, b):
    c = CHUNK
    r, s = _iota2((c, c), 0), _iota2((c, c), 1)
    tril = (s <= r).astype(F32)
    gc_w = mm(tril, jnp.broadcast_to(g, (c, HD)), "nn", "la")
    gc_i = mm(tril, jnp.broadcast_to(g, (c, c)), "nn", "la")
    gc_j = mm(jnp.ones((c, c), F32), jnp.where(r == s, gc_i, 0.0), "nn", "la")
    decay = jnp.exp(jnp.where(s <= r, gc_i - gc_j, -1e30))
    kb = k * b
    low = jnp.where(s < r, mm(kb, k, "nt", "x3") * decay, 0.0)
    inv = jnp.where(r == s, 1.0, 0.0) - low
    pw = mm(low, low, "nn", "x3")
    n_sq = 1
    while 2 * n_sq < c:
        inv = inv + mm(inv, pw, "nn", "x3")
        n_sq *= 2
        if 2 * n_sq < c:
            pw = mm(pw, pw, "nn", "x3")
    egc = jnp.exp(gc_w)
    u = mm(inv, v * b, "nn", "x3")
    w = mm(inv, kb * egc, "nn", "x3")
    intra = mm(q, k, "nt", "lo") * decay
    g_last = jnp.sum(g, axis=0, keepdims=True)
    kd = k * jnp.exp(g_last - gc_w)
    egl = jnp.exp(jnp.broadcast_to(g_last, (1, HD)))
    return u, w, intra, q * egc, kd, egl


def _group(n, cap=4):
    return max(g for g in range(1, cap + 1) if n % g == 0)


def _chunk_specs(nb_h, nb_c, n_chunks=None):
    cn = (lambda n: n) if n_chunks is None else (lambda n: n_chunks // nb_c - 1 - n)
    rows = nb_c * CHUNK
    vec = pl.BlockSpec((nb_h, rows, HD), lambda h, n: (h, cn(n), 0))
    col = pl.BlockSpec((nb_h, rows, 1), lambda h, n: (h, cn(n), 0))
    sq = pl.BlockSpec((nb_h, rows, CHUNK), lambda h, n: (h, cn(n), 0))
    one = pl.BlockSpec((nb_h, nb_c, 1, HD), lambda h, n: (h, cn(n), 0, 0))
    st = pl.BlockSpec((nb_h, nb_c, HD, HD), lambda h, n: (h, cn(n), 0, 0))
    return vec, col, sq, one, st


def _chunk_shapes(n_heads, t):
    vec = jax.ShapeDtypeStruct((n_heads, t, HD), F32)
    return [vec, vec, jax.ShapeDtypeStruct((n_heads, t, CHUNK), F32), vec, vec,
            jax.ShapeDtypeStruct((n_heads, t // CHUNK, 1, HD), F32)]


def _chunk_rows(ci):
    return slice(ci * CHUNK, (ci + 1) * CHUNK)


def gdn_chunk_fwd(q, k, v, g, b, name, jobs=()):
    n_heads, t, _ = q.shape
    nb_c = _group(t // CHUNK)
    vec, col, sq, one, _ = _chunk_specs(1, nb_c)

    def body(q_r, k_r, v_r, g_r, b_r, *outs):
        for ci in range(nb_c):
            rows = _chunk_rows(ci)
            vals = _gdn_chunk_f(*(r[0, rows, :] for r in (q_r, k_r, v_r, g_r, b_r)))
            for o_ref, val in zip(outs[:5], vals[:5]):
                o_ref[0, rows, :] = val
            outs[5][0, ci] = vals[5]

    return hosted_call(
        body, (q, k, v, g, b), name=name, grid=(n_heads, t // CHUNK // nb_c), in_specs=[vec] * 3 + [col] * 2,
        out_specs=[vec, vec, sq, vec, vec, one], out_shape=_chunk_shapes(n_heads, t),
        sem=("parallel", "parallel"), jobs=jobs)


def gdn_chunk_bwd(q, k, v, g, b, cts, name, jobs=()):
    n_heads, t, _ = q.shape
    nb_c = _group(t // CHUNK)
    vec, col, sq, one, _ = _chunk_specs(1, nb_c)

    def body(q_r, k_r, v_r, g_r, b_r, du, dw, di, dqd, dkd, degl, dq, dk, dv, dg, db):
        for ci in range(nb_c):
            rows = _chunk_rows(ci)
            _, vjp = jax.vjp(_gdn_chunk_f, *(r[0, rows, :] for r in (q_r, k_r, v_r, g_r, b_r)))
            grads = vjp(tuple(r[0, rows, :] for r in (du, dw, di, dqd, dkd)) + (degl[0, ci],))
            for o_ref, val in zip((dq, dk, dv, dg, db), grads):
                o_ref[0, rows, :] = val

    col_shape = jax.ShapeDtypeStruct((n_heads, t, 1), F32)
    return hosted_call(
        body, (q, k, v, g, b, *cts), name=name, grid=(n_heads, t // CHUNK // nb_c),
        in_specs=[vec] * 3 + [col] * 2 + [vec, vec, sq, vec, vec, one],
        out_specs=[vec] * 3 + [col] * 2,
        out_shape=[jax.ShapeDtypeStruct((n_heads, t, HD), F32)] * 3 + [col_shape] * 2,
        sem=("parallel", "parallel"), jobs=jobs)


def _scan_f(s, u, w, a, qd, kd, egl):
    vn = u - mm(w, s, "nn", "lo")
    o = mm(qd, s, "nn", "lo") + mm(a, vn, "nn", "lo")
    return o, s * egl + mm(kd, vn, "tn", "lo")


def gdn_scan_fwd(chunks, name):
    u = chunks[0]
    n_heads, t, _ = u.shape
    nc = t // CHUNK
    nb_h = _group(n_heads)
    vec, _, sq, one, st = _chunk_specs(nb_h, 1)

    def body(u_r, w_r, a_r, qd_r, kd_r, e_r, o_ref, s_ref, state):
        @pl.when(pl.program_id(1) == 0)
        def _():
            state[...] = jnp.zeros_like(state)

        for hh in range(nb_h):
            s = state[hh]
            s_ref[hh, 0] = s
            o, s2 = _scan_f(s, u_r[hh], w_r[hh], a_r[hh], qd_r[hh], kd_r[hh], e_r[hh, 0])
            o_ref[hh] = o
            state[hh] = s2

    return pl.pallas_call(
        body, name=name, grid=(n_heads // nb_h, nc), in_specs=[vec, vec, sq, vec, vec, one], out_specs=[vec, st],
        out_shape=[jax.ShapeDtypeStruct((n_heads, t, HD), F32), jax.ShapeDtypeStruct((n_heads, nc, HD, HD), F32)],
        scratch_shapes=[pltpu.VMEM((nb_h, HD, HD), F32)], compiler_params=_cp("parallel", "arbitrary"),
    )(*chunks)


def gdn_scan_bwd(chunks, states, do, name):
    n_heads, t, _ = do.shape
    nc = t // CHUNK
    nb_h = _group(n_heads)
    vec, _, sq, one, st = _chunk_specs(nb_h, 1, n_chunks=nc)

    def body(u_r, w_r, a_r, qd_r, kd_r, e_r, s_r, do_r, du, dw, da, dqd, dkd, de, dstate):
        @pl.when(pl.program_id(1) == 0)
        def _():
            dstate[...] = jnp.zeros_like(dstate)

        for hh in range(nb_h):
            _, vjp = jax.vjp(_scan_f, s_r[hh, 0], u_r[hh], w_r[hh], a_r[hh], qd_r[hh], kd_r[hh], e_r[hh, 0])
            grads = vjp((do_r[hh], dstate[hh]))
            dstate[hh] = grads[0]
            for o_ref, val in zip((du, dw, da, dqd, dkd), grads[1:6]):
                o_ref[hh] = val
            de[hh, 0] = grads[6]

    return pl.pallas_call(
        body, name=name, grid=(n_heads // nb_h, nc), in_specs=[vec, vec, sq, vec, vec, one, st, vec],
        out_specs=[vec, vec, sq, vec, vec, one], out_shape=_chunk_shapes(n_heads, t),
        scratch_shapes=[pltpu.VMEM((nb_h, HD, HD), F32)], compiler_params=_cp("parallel", "arbitrary"),
    )(*chunks, states, do)


def _post_f(o, gate, gain):
    return _rms(o, gain) * _silu(gate)


def gdn_post_fwd(o, z, gate_blk, gain, name):
    n_heads, t, _ = o.shape
    tt = _tile(t, 512)

    def body(o_r, gt_r, gn_r, out):
        out[...] = _post_f(o_r[...], gt_r[...], gn_r[...]).astype(out.dtype)

    return pl.pallas_call(
        body, name=name, grid=(n_heads, t // tt),
        in_specs=[pl.BlockSpec((None, tt, HD), lambda h, i: (h, i, 0)), pl.BlockSpec((tt, HD), lambda h, i: (i, gate_blk + h)),
                  pl.BlockSpec((1, HD), lambda h, i: (0, 0))],
        out_specs=pl.BlockSpec((tt, HD), lambda h, i: (i, h)),
        out_shape=jax.ShapeDtypeStruct((t, n_heads * HD), MXU), compiler_params=_cp("parallel", "parallel"),
    )(o, z, gain)


def gdn_post_bwd(o, z, gate_blk, gain, doa, name):
    n_heads, t, _ = o.shape
    tt = _tile(t, 512)

    def body(o_r, gt_r, gn_r, d_r, do_ref, dgt_ref, dgn_ref):
        _, vjp = jax.vjp(_post_f, o_r[...], gt_r[...], gn_r[...])
        go, ggt, ggn = vjp(d_r[...])
        do_ref[...] = go
        dgt_ref[...] = ggt.astype(dgt_ref.dtype)

        @pl.when((pl.program_id(0) == 0) & (pl.program_id(1) == 0))
        def _():
            dgn_ref[...] = jnp.zeros_like(dgn_ref)

        dgn_ref[...] += ggn

    tok = pl.BlockSpec((tt, HD), lambda h, i: (i, h))
    vec = pl.BlockSpec((1, HD), lambda h, i: (0, 0))
    head = pl.BlockSpec((None, tt, HD), lambda h, i: (h, i, 0))
    return pl.pallas_call(
        body, name=name, grid=(n_heads, t // tt),
        in_specs=[head, pl.BlockSpec((tt, HD), lambda h, i: (i, gate_blk + h)), vec, tok],
        out_specs=[head, tok, vec],
        out_shape=[jax.ShapeDtypeStruct((n_heads, t, HD), F32), jax.ShapeDtypeStruct((t, n_heads * HD), MXU),
                   jax.ShapeDtypeStruct((1, HD), F32)],
        compiler_params=_cp("arbitrary", "arbitrary"),
    )(o, z, gain, doa)


def _gmlp_f(ups, vps, lngs, wss, bcols):
    n_groups = len(ups)
    width = HD * n_groups
    us = [_gelu(a) for a in ups]
    vs = [_gelu(a) for a in vps]
    mu = sum(jnp.sum(a, axis=-1, keepdims=True) for a in vs) * (1.0 / width)
    xcs = [a - mu for a in vs]
    var = sum(jnp.sum(a * a, axis=-1, keepdims=True) for a in xcs) * (1.0 / width)
    rstd = lax.rsqrt(var + EPS)
    r, s = _iota2((HD, HD), 0), _iota2((HD, HD), 1)
    causal = (s // CHUNK) <= (r // CHUNK)
    outs = []
    for gi in range(n_groups):
        vb = xcs[gi] * rstd * lngs[gi]
        sp = mm(jnp.where(causal, wss[gi], 0.0), vb, "nn", "lo") + bcols[gi]
        outs.append(us[gi] * sp)
    return tuple(outs)


def _gmlp_load(uv_u, uv_v, lng, ws, bt, n_groups):
    seg = lambda ref, gi: ref[:, gi * HD:(gi + 1) * HD]
    return ([seg(uv_u, gi) for gi in range(n_groups)], [seg(uv_v, gi) for gi in range(n_groups)],
            [seg(lng, gi) for gi in range(n_groups)], [ws[gi] for gi in range(n_groups)],
            [bt[:, gi:gi + 1] for gi in range(n_groups)])


def _gmlp_specs(width, u_blk, n_groups):
    u = pl.BlockSpec((HD, width), lambda i: (i, u_blk))
    v = pl.BlockSpec((HD, width), lambda i: (i, u_blk + 1))
    lng = pl.BlockSpec((1, width), lambda i: (0, 0))
    ws = pl.BlockSpec((n_groups, HD, HD), lambda i: (0, 0, 0))
    bt = pl.BlockSpec((HD, LANE), lambda i: (0, 0))
    return u, v, lng, ws, bt


def gmlp_fwd(z, uv_off, width, lng, ws, bt, name):
    t = z.shape[0]
    n_groups = width // HD
    specs = _gmlp_specs(width, uv_off // width, n_groups)

    def body(u_r, v_r, l_r, w_r, b_r, out):
        outs = _gmlp_f(*_gmlp_load(u_r, v_r, l_r, w_r, b_r, n_groups))
        for gi in range(n_groups):
            out[:, gi * HD:(gi + 1) * HD] = outs[gi].astype(out.dtype)

    return pl.pallas_call(
        body, name=name, grid=(t // HD,), in_specs=list(specs), out_specs=pl.BlockSpec((HD, width), lambda i: (i, 0)),
        out_shape=jax.ShapeDtypeStruct((t, width), MXU), compiler_params=_cp("parallel"),
    )(z, z, lng, ws, bt)


def gmlp_bwd(z, uv_off, width, lng, ws, bt, dob, name):
    t = z.shape[0]
    n_groups = width // HD
    specs = _gmlp_specs(width, uv_off // width, n_groups)

    def body(u_r, v_r, l_r, w_r, b_r, d_r, duv, dl, dws, dbt):
        _, vjp = jax.vjp(_gmlp_f, *_gmlp_load(u_r, v_r, l_r, w_r, b_r, n_groups))
        gu, gv, gl, gw, gb = vjp(tuple(d_r[:, gi * HD:(gi + 1) * HD] for gi in range(n_groups)))

        @pl.when(pl.program_id(0) == 0)
        def _():
            dl[...] = jnp.zeros_like(dl)
            dws[...] = jnp.zeros_like(dws)
            dbt[...] = jnp.zeros_like(dbt)

        for gi in range(n_groups):
            duv[:, gi * HD:(gi + 1) * HD] = gu[gi].astype(duv.dtype)
            duv[:, width + gi * HD:width + (gi + 1) * HD] = gv[gi].astype(duv.dtype)
            dl[:, gi * HD:(gi + 1) * HD] += gl[gi]
            dws[gi] += gw[gi]
            dbt[:, gi:gi + 1] += gb[gi]

    return pl.pallas_call(
        body, name=name, grid=(t // HD,), in_specs=list(specs) + [pl.BlockSpec((HD, width), lambda i: (i, 0))],
        out_specs=[pl.BlockSpec((HD, 2 * width), lambda i: (i, 0)), specs[2], specs[3], specs[4]],
        out_shape=[jax.ShapeDtypeStruct((t, 2 * width), MXU), jax.ShapeDtypeStruct((1, width), F32),
                   jax.ShapeDtypeStruct((n_groups, HD, HD), F32), jax.ShapeDtypeStruct((HD, LANE), F32)],
        compiler_params=_cp("arbitrary"),
    )(z, z, lng, ws, bt, dob)


def _sba_pre_f(qp, kp, qg, kg):
    return _rms(qp, qg), _rms(kp, kg)


def sba_pre_fwd(z, c_blk, n_heads, qg, kg, name):
    t = z.shape[0]
    tt = _tile(t, 512)
    zs = [pl.BlockSpec((tt, HD), functools.partial(lambda h, i, s: (i, c_blk + s * n_heads + h), s=s)) for s in range(3)]
    vec = pl.BlockSpec((1, HD), lambda h, i: (0, 0))
    head = pl.BlockSpec((None, tt, HD), lambda h, i: (h, i, 0))

    def body(qp, kp, vp, qg_r, kg_r, q_o, k_o, v_o):
        q, k = _sba_pre_f(qp[...], kp[...], qg_r[...], kg_r[...])
        q_o[...] = q.astype(q_o.dtype)
        k_o[...] = k.astype(k_o.dtype)
        v_o[...] = vp[...].astype(v_o.dtype)

    return pl.pallas_call(
        body, name=name, grid=(n_heads, t // tt), in_specs=zs + [vec, vec], out_specs=[head] * 3,
        out_shape=[jax.ShapeDtypeStruct((n_heads, t, HD), MXU)] * 3, compiler_params=_cp("parallel", "parallel"),
    )(z, z, z, qg, kg)


def sba_pre_bwd(z, c_blk, n_heads, qg, kg, dq, dk, dv, name):
    t = z.shape[0]
    tt = _tile(t, 512)
    zs = [pl.BlockSpec((tt, HD), functools.partial(lambda h, i, s: (i, c_blk + s * n_heads + h), s=s)) for s in range(2)]
    vec = pl.BlockSpec((1, HD), lambda h, i: (0, 0))
    head = pl.BlockSpec((None, tt, HD), lambda h, i: (h, i, 0))
    tok = pl.BlockSpec((tt, HD), lambda h, i: (i, h))

    def body(qp, kp, qg_r, kg_r, dq_r, dk_r, dv_r, dqp, dkp, dvp, dqg, dkg):
        _, vjp = jax.vjp(_sba_pre_f, qp[...], kp[...], qg_r[...], kg_r[...])
        gq, gk, gqg, gkg = vjp((dq_r[...], dk_r[...]))
        dqp[...] = gq.astype(dqp.dtype)
        dkp[...] = gk.astype(dkp.dtype)
        dvp[...] = dv_r[...].astype(dvp.dtype)

        @pl.when((pl.program_id(0) == 0) & (pl.program_id(1) == 0))
        def _():
            dqg[...] = jnp.zeros_like(dqg)
            dkg[...] = jnp.zeros_like(dkg)

        dqg[...] += gqg
        dkg[...] += gkg

    return pl.pallas_call(
        body, name=name, grid=(n_heads, t // tt), in_specs=zs + [vec, vec] + [head] * 3,
        out_specs=[tok] * 3 + [vec, vec],
        out_shape=[jax.ShapeDtypeStruct((t, n_heads * HD), MXU)] * 3 + [jax.ShapeDtypeStruct((1, HD), F32)] * 2,
        compiler_params=_cp("arbitrary", "arbitrary"),
    )(z, z, qg, kg, dq, dk, dv)


def _sba_block(q, kj, i, j):
    zz = lax.dot_general(q, kj, _DN["nt"], preferred_element_type=F32) * (HD ** -0.5)
    ls = _logsig(zz)
    strict = (j * HD + _iota2((HD, HD), 1)) < (i * HD + _iota2((HD, HD), 0))
    return zz, ls, jnp.where(strict, ls - zz, 0.0), strict


def sba_fwd(q, k, v, name, jobs=()):
    n_heads, t, _ = q.shape
    nb_h = _group(n_heads)

    def body(q_r, k_r, v_r, o_ref, tot_ref):
        i = pl.program_id(1)
        after = (_iota2((HD, HD), 0) > _iota2((HD, HD), 1)).astype(F32)

        def step(it, carry):
            j = i - it
            rows = pl.ds(pl.multiple_of(j * HD, HD), HD)
            new = []
            for hh in range(nb_h):
                acc, cs = carry[hh]
                _, ls, lk, strict = _sba_block(q_r[hh], k_r[hh, rows, :], i, j)
                suffix = _mm_raw(lk, after, "nn", "rb") + cs
                att = jnp.where(strict, jnp.exp(ls + suffix), 0.0)
                acc = acc + _mm_raw(att, v_r[hh, rows, :], "nn", "lo")
                new.append((acc, cs + jnp.sum(lk, axis=-1, keepdims=True)))
            return tuple(new)

        init = tuple((jnp.zeros((HD, HD), F32), jnp.zeros((HD, 1), F32)) for _ in range(nb_h))
        final = lax.fori_loop(0, i + 1, step, init)
        for hh in range(nb_h):
            o_ref[:, hh * HD:(hh + 1) * HD] = final[hh][0].astype(o_ref.dtype)
            tot_ref[hh] = final[hh][1]

    full = pl.BlockSpec((nb_h, t, HD), lambda h, i: (h, 0, 0))
    return hosted_call(
        body, (q, k, v), name=name, grid=(n_heads // nb_h, t // HD),
        in_specs=[pl.BlockSpec((nb_h, HD, HD), lambda h, i: (h, i, 0)), full, full],
        out_specs=[pl.BlockSpec((HD, nb_h * HD), lambda h, i: (i, h)), pl.BlockSpec((nb_h, HD, 1), lambda h, i: (h, i, 0))],
        out_shape=[jax.ShapeDtypeStruct((t, n_heads * HD), MXU), jax.ShapeDtypeStruct((n_heads, t, 1), F32)],
        sem=("parallel", "parallel"), jobs=jobs)


def sba_bwd(q, k, v, tot, do, name, jobs=()):
    n_heads, t, _ = q.shape
    nb_h = _group(n_heads)

    def body(q_r, k_r, v_r, tot_r, do_r, dq_ref, dk_ref, dv_ref):
        i = pl.program_id(1)

        @pl.when(i == 0)
        def _():
            dk_ref[...] = jnp.zeros_like(dk_ref)
            dv_ref[...] = jnp.zeros_like(dv_ref)

        r, s = _iota2((HD, HD), 0), _iota2((HD, HD), 1)
        upto = (r <= s).astype(F32)
        before = (r < s).astype(F32)

        def step(j, carry):
            rows = pl.ds(pl.multiple_of(j * HD, HD), HD)
            new = []
            for hh in range(nb_h):
                dq, cp, cd = carry[hh]
                qb, dob = q_r[hh], do_r[:, hh * HD:(hh + 1) * HD]
                kj, vj = k_r[hh, rows, :], v_r[hh, rows, :]
                _, ls, lk, strict = _sba_block(qb, kj, i, j)
                sig = jnp.exp(ls)
                suffix = tot_r[hh] - (cp + _mm_raw(lk, upto, "nn", "rb"))
                att = jnp.where(strict, jnp.exp(ls + suffix), 0.0)
                dp = _mm_raw(dob, vj, "nt", "lo") * att
                dlk = cd + _mm_raw(dp, before, "nn", "rb")
                dz = jnp.where(strict, dp * (1.0 - sig) - dlk * sig, 0.0) * (HD ** -0.5)
                dk_ref[hh, rows, :] += _mm_raw(dz, qb, "tn", "lo")
                dv_ref[hh, rows, :] += _mm_raw(att, dob, "tn", "lo")
                new.append((dq + _mm_raw(dz, kj, "nn", "lo"), cp + jnp.sum(lk, axis=-1, keepdims=True),
                            cd + jnp.sum(dp, axis=-1, keepdims=True)))
            return tuple(new)

        zero_col = jnp.zeros((HD, 1), F32)
        final = lax.fori_loop(0, i + 1, step, tuple((jnp.zeros((HD, HD), F32), zero_col, zero_col) for _ in range(nb_h)))
        for hh in range(nb_h):
            dq_ref[hh] = final[hh][0]

    full = pl.BlockSpec((nb_h, t, HD), lambda h, i: (h, 0, 0))
    blk = pl.BlockSpec((nb_h, HD, HD), lambda h, i: (h, i, 0))
    return hosted_call(
        body, (q, k, v, tot, do), name=name, grid=(n_heads // nb_h, t // HD),
        in_specs=[blk, full, full, pl.BlockSpec((nb_h, HD, 1), lambda h, i: (h, i, 0)),
                  pl.BlockSpec((HD, nb_h * HD), lambda h, i: (i, h))],
        out_specs=[blk, full, full], out_shape=[jax.ShapeDtypeStruct((n_heads, t, HD), F32)] * 3,
        sem=("parallel", "arbitrary"), jobs=jobs)


def small_adam(parts, w, m, v, name):
    n_parts, rows, _ = parts.shape
    tr = _tile(rows, 512) if rows % LANE == 0 else rows

    def body(p_ref, w_ref, m_ref, v_ref, g_out, d_out, m_out, v_out):
        g = p_ref[0]
        for k in range(1, n_parts):
            g = g + p_ref[k]
        delta, m2, v2 = _adam_math(w_ref[...], g, m_ref[...], v_ref[...])
        g_out[...] = g
        d_out[...] = delta
        m_out[...] = m2
        v_out[...] = v2

    blk = pl.BlockSpec((tr, LANE), lambda i: (i, 0))
    return pl.pallas_call(
        body, name=name, grid=(rows // tr,),
        in_specs=[pl.BlockSpec((n_parts, tr, LANE), lambda i: (0, i, 0)), blk, blk, blk], out_specs=[blk] * 4,
        out_shape=[jax.ShapeDtypeStruct((rows, LANE), F32)] * 4, compiler_params=_cp("parallel"),
    )(parts, w, m, v)


def _pack(arrays):
    flat = jnp.concatenate([a.reshape(-1).astype(F32) for a in arrays])
    pad = (-flat.shape[0]) % (8 * LANE)
    return jnp.pad(flat, (0, pad)).reshape(-1, LANE)


def _unpack(packed, shapes):
    flat, outs, pos = packed.reshape(-1), [], 0
    for shp in shapes:
        n = 1
        for s in shp:
            n *= s
        outs.append(flat[pos:pos + n].reshape(shp))
        pos += n
    return outs


def _pad_lanes(a):
    return jnp.pad(a, ((0, 0), (0, LANE - a.shape[1])))


def kernel(x, w_in, conv_w, a_log, dt_bias, gdn_norm_g, gmlp_ln_g, w_spatial, b_spatial, sba_q_g, sba_k_g, w_out_a, w_out_b, w_out_c, w_out, norm_mix_g, norm_mlp_g, w_ff1, w_ff2, loss_target, m_w_in, m_conv_w, m_a_log, m_dt_bias, m_gdn_norm_g, m_gmlp_ln_g, m_w_spatial, m_b_spatial, m_sba_q_g, m_sba_k_g, m_w_out_a, m_w_out_b, m_w_out_c, m_w_out, m_norm_mix_g, m_norm_mlp_g, m_w_ff1, m_w_ff2, v_w_in, v_conv_w, v_a_log, v_dt_bias, v_gdn_norm_g, v_gmlp_ln_g, v_w_spatial, v_b_spatial, v_sba_q_g, v_sba_k_g, v_w_out_a, v_w_out_b, v_w_out_c, v_w_out, v_norm_mix_g, v_norm_mlp_g, v_w_ff1, v_w_ff2):
    depth = w_in.shape[0]
    _, t, d = x.shape
    n_heads = d // 256
    gw = n_heads * HD
    width = d // 2
    n_groups = width // HD
    off_gate, off_uv, off_c, off_gl = 3 * gw, 4 * gw, 4 * gw + 2 * width, 7 * gw + 2 * width
    off_ab = off_gl + 3 * d
    n_packed = off_ab + LANE
    n_in = off_ab + 2 * n_heads
    assert w_in.shape[2] * N_DEV == n_in and t % LANE == 0 and d % 256 == 0

    ix, iy, ic = lax.axis_index("x"), lax.axis_index("y"), lax.axis_index("c")
    dev = 4 * ix + 2 * iy + ic
    c_idx = jnp.reshape(ic, (1,)).astype(jnp.int32)
    xs, target = x[0], loss_target[0]

    w_in_t, m_in_t, v_in_t = (jnp.swapaxes(a, 1, 2) for a in (w_in, m_w_in, v_w_in))
    shard_of = dict(w_in=w_in_t, w_out_a=w_out_a, w_out_b=w_out_b, w_out_c=w_out_c, w_out=w_out, w_ff1=w_ff1, w_ff2=w_ff2)
    ag_jobs = lambda l: {nm: GatherJob(w[l].astype(MXU)) for nm, w in shard_of.items()}
    layer0 = ag_jobs(0)
    conv_job = GatherJob(conv_w)
    run_jobs(list(layer0.values()) + [conv_job], "ag_layer0")
    gathered_w = [{nm: job.results[0] for nm, job in layer0.items()}]
    conv_full = jnp.transpose(conv_job.results[0], (1, 2, 0, 3)).reshape(depth, CONV_K, 3 * gw)

    per_dev = n_in // N_DEV

    def pack_in(g):
        pieces = []
        for lo, hi in ((0, 3 * gw), (3 * gw + 2 * n_heads, n_in), (3 * gw, 3 * gw + 2 * n_heads)):
            while lo < hi:
                p = lo // per_dev
                end = min(hi, (p + 1) * per_dev)
                pieces.append(g[p, lo - p * per_dev:end - p * per_dev])
                lo = end
        return jnp.concatenate(pieces + [jnp.zeros((LANE - 2 * n_heads, d), g.dtype)], axis=0)

    def unpack_in(wp):
        w = jnp.concatenate([wp[:3 * gw], wp[off_ab:off_ab + 2 * n_heads], wp[3 * gw:off_ab]], axis=0)
        return w.reshape(N_DEV, per_dev, d)

    alog_p, dtb_p = _pad_lanes(a_log), _pad_lanes(dt_bias)
    bt_all = jnp.pad(jnp.transpose(b_spatial, (0, 2, 1)), ((0, 0), (0, 0), (0, LANE - n_groups)))

    saved = []
    cur = xs
    for l in range(depth):
        gl_w = gathered_w[l]
        lw = dict(w_in=pack_in(gl_w["w_in"]), conv=conv_full[l], alog=alog_p[l:l + 1], dtb=dtb_p[l:l + 1],
                  gng=gdn_norm_g[l:l + 1], lng=gmlp_ln_g[l:l + 1], ws=w_spatial[l], bt=bt_all[l],
                  qg=sba_q_g[l:l + 1], kg=sba_k_g[l:l + 1], gmix=norm_mix_g[l:l + 1], gmlp=norm_mlp_g[l:l + 1])
        nxt = ag_jobs(l + 1) if l + 1 < depth else {}
        ride = lambda *names: [nxt[nm] for nm in names if nm in nxt]
        s = dict(lw=lw, x=cur, gw=gl_w)
        s["h1"] = rms_fwd(cur, lw["gmix"], "rms_mix")
        z = s["z"] = matmul(s["h1"], lw["w_in"], "nt", F32, "mm_in", caps=(1024, 1408, 2048), jobs=ride("w_ff2"))
        s["pre"] = gdn_pre_fwd(z, off_ab // LANE, lw["conv"], lw["alog"], lw["dtb"], n_heads, "gdn_pre")
        s["chunks"] = gdn_chunk_fwd(*s["pre"], "gdn_chunk", jobs=ride("w_in"))
        s["o"], s["states"] = gdn_scan_fwd(s["chunks"], "gdn_scan")
        s["oa"] = gdn_post_fwd(s["o"], z, off_gate // HD, lw["gng"], "gdn_post")
        s["ob"] = gmlp_fwd(z, off_uv, width, lw["lng"], lw["ws"], lw["bt"], "gmlp")
        s["qkv_c"] = sba_pre_fwd(z, off_c // HD, n_heads, lw["qg"], lw["kg"], "sba_pre")
        s["oc"], s["tot"] = sba_fwd(*s["qkv_c"], "sba", jobs=ride("w_ff1"))
        s["ba"] = matmul(s["oa"], gl_w["w_out_a"], "nn", F32, "mm_oa", b_view="cols")
        s["bb"] = matmul(s["ob"], gl_w["w_out_b"], "nn", F32, "mm_ob", b_view="cols")
        s["bc"] = matmul(s["oc"], gl_w["w_out_c"], "nn", F32, "mm_oc", b_view="cols")
        s["y"] = merge_fwd(z, off_gl, s["ba"], s["bb"], s["bc"], "merge")
        s["x1"] = matmul(s["y"], gl_w["w_out"], "nn", F32, "mm_out", res=cur, b_view="rows")
        s["h2"] = rms_fwd(s["x1"], lw["gmlp"], "rms_mlp")
        s["f"] = matmul(s["h2"], gl_w["w_ff1"], "nn", F32, "mm_ff1", b_view="cols", jobs=ride("w_out", "w_out_a"))
        s["a"] = sqrelu_fwd(s["f"], "sqrelu")
        cur = matmul(s["a"], gl_w["w_ff2"], "nn", F32, "mm_ff2", res=s["x1"], b_view="rows",
                     jobs=ride("w_out_b", "w_out_c"))
        if nxt:
            gathered_w.append({nm: job.results[0] for nm, job in nxt.items()})
        saved.append(s)

    dx, dxb, loss_tile = loss_head(cur, target, "loss_head")
    loss = lax.psum(loss_tile[0, 0], AXES)

    big = dict(w_in=(w_in_t, m_in_t, v_in_t), w_out_a=(w_out_a, m_w_out_a, v_w_out_a), w_out_b=(w_out_b, m_w_out_b, v_w_out_b),
               w_out_c=(w_out_c, m_w_out_c, v_w_out_c), w_out=(w_out, m_w_out, v_w_out), w_ff1=(w_ff1, m_w_ff1, v_w_ff1),
               w_ff2=(w_ff2, m_w_ff2, v_w_ff2))
    bufs = {nm: tuple(lax.empty(w.shape, F32) for _ in range(4)) for nm, (w, _, _) in big.items()}
    waiting = []

    def pair_reduce(nm, g8, l):
        w = big[nm][0]
        g8 = g8.reshape(N_DEV, w.shape[1], w.shape[2])
        r4 = rs_pair_exchange(g8, "rs_pair_" + nm)
        waiting.append((nm, l, ChipExchangeJob(pair_sum(g8, r4, c_idx, "rs_pair_sum_" + nm))))

    def take(*names):
        picked = [e for e in waiting if e[0] in names]
        for e in picked:
            waiting.remove(e)
        return picked

    def update(picked):
        for nm, l, job in picked:
            w, m, v = big[nm]
            bufs[nm] = tuple(adam_layer(job.results[0], w, m, v, bufs[nm], l, "adam_" + nm))

    small_grads = []
    for l in reversed(range(depth)):
        s = saved[l]
        lw, z, gl_w = s["lw"], s["z"], s["gw"]
        da = matmul(dxb, gl_w["w_ff2"], "nt", F32, "mm_ff2_dx", b_view="rows")
        pair_reduce("w_ff2", matmul(s["a"], dxb, "tn", MXU, "mm_ff2_dw"), l)
        df = sqrelu_bwd(s["f"], da, "sqrelu_bwd")
        dh2 = matmul(df, gl_w["w_ff1"], "nt", F32, "mm_ff1_dx", b_view="cols")
        pair_reduce("w_ff1", matmul(s["h2"], df, "tn", MXU, "mm_ff1_dw", out_slabs=w_ff1.shape[2]), l)
        dx1, dx1b, d_gmlp = rms_bwd(s["x1"], lw["gmlp"], dh2, dx, "rms_mlp_bwd")
        dy = matmul(dx1b, gl_w["w_out"], "nt", F32, "mm_out_dx", b_view="rows")
        pair_reduce("w_out", matmul(s["y"], dx1b, "tn", MXU, "mm_out_dw"), l)
        dgl, dba, dbb, dbc = merge_bwd(z, off_gl, s["ba"], s["bb"], s["bc"], dy, "merge_bwd")
        doa = matmul(dba, gl_w["w_out_a"], "nt", F32, "mm_oa_dx", b_view="cols")
        dob = matmul(dbb, gl_w["w_out_b"], "nt", F32, "mm_ob_dx", b_view="cols")
        doc = matmul(dbc, gl_w["w_out_c"], "nt", F32, "mm_oc_dx", b_view="cols")
        slab = w_out_a.shape[2]
        pair_reduce("w_out_a", matmul(s["oa"], dba, "tn", MXU, "mm_oa_dw", out_slabs=slab), l)
        pair_reduce("w_out_b", matmul(s["ob"], dbb, "tn", MXU, "mm_ob_dw", out_slabs=slab), l)
        pair_reduce("w_out_c", matmul(s["oc"], dbc, "tn", MXU, "mm_oc_dw", out_slabs=slab), l)
        riding = take("w_ff2", "w_ff1", "w_out")
        dqc, dkc, dvc = sba_bwd(*s["qkv_c"], s["tot"], doc, "sba_bwd", jobs=[e[2] for e in riding])
        update(riding)
        dz_qc, dz_kc, dz_vc, d_qg, d_kg = sba_pre_bwd(z, off_c // HD, n_heads, lw["qg"], lw["kg"], dqc, dkc, dvc, "sba_pre_bwd")
        dz_uv, d_lng, d_ws, d_bt = gmlp_bwd(z, off_uv, width, lw["lng"], lw["ws"], lw["bt"], dob, "gmlp_bwd")
        do, dz_gate, d_gng = gdn_post_bwd(s["o"], z, off_gate // HD, lw["gng"], doa, "gdn_post_bwd")
        chunk_cts = gdn_scan_bwd(s["chunks"], s["states"], do, "gdn_scan_bwd")
        riding = take("w_in", "w_out_a", "w_out_b", "w_out_c")
        dqa, dka, dva, dga, dba_ = gdn_chunk_bwd(*s["pre"], chunk_cts, "gdn_chunk_bwd", jobs=[e[2] for e in riding])
        update(riding)
        dz_q, dz_k, dz_v, d_ab, d_cq, d_ck, d_cv, d_alog, d_dtb = gdn_pre_bwd(
            z, off_ab // LANE, lw["conv"], lw["alog"], lw["dtb"], n_heads, dqa, dka, dva, dga, dba_, "gdn_pre_bwd")
        dz = jnp.concatenate([dz_q, dz_k, dz_v, dz_gate, dz_uv, dz_qc, dz_kc, dz_vc, dgl[0], dgl[1], dgl[2],
                              d_ab.astype(MXU)], axis=1)
        dh1 = matmul(dz, lw["w_in"], "nn", F32, "mm_in_dx", caps=(1024, 1024, 1408))
        pair_reduce("w_in", unpack_in(matmul(dz, s["h1"], "tn", MXU, "mm_in_dw", caps=(1408, 1024, 2048))), l)
        dx, dxb, d_gmix = rms_bwd(s["x"], lw["gmix"], dh1, dx1, "rms_mix_bwd")
        small_grads.append(dict(
            conv_w=jnp.concatenate([d_cq, d_ck, d_cv], axis=1), a_log=d_alog[0, :n_heads], dt_bias=d_dtb[0, :n_heads],
            gdn_norm_g=d_gng[0], gmlp_ln_g=d_lng[0], w_spatial=d_ws, b_spatial=jnp.transpose(d_bt[:, :n_groups]),
            sba_q_g=d_qg[0], sba_k_g=d_kg[0], norm_mix_g=d_gmix[0], norm_mlp_g=d_gmlp[0]))
    small_grads = small_grads[::-1]
    rest = take("w_in")
    run_jobs([e[2] for e in rest], "rs_chip_last")
    update(rest)

    rep_names = ["a_log", "dt_bias", "gdn_norm_g", "gmlp_ln_g", "w_spatial", "b_spatial", "sba_q_g", "sba_k_g",
                 "norm_mix_g", "norm_mlp_g"]
    rep = dict(a_log=(a_log, m_a_log, v_a_log), dt_bias=(dt_bias, m_dt_bias, v_dt_bias),
               gdn_norm_g=(gdn_norm_g, m_gdn_norm_g, v_gdn_norm_g), gmlp_ln_g=(gmlp_ln_g, m_gmlp_ln_g, v_gmlp_ln_g),
               w_spatial=(w_spatial, m_w_spatial, v_w_spatial), b_spatial=(b_spatial, m_b_spatial, v_b_spatial),
               sba_q_g=(sba_q_g, m_sba_q_g, v_sba_q_g), sba_k_g=(sba_k_g, m_sba_k_g, v_sba_k_g),
               norm_mix_g=(norm_mix_g, m_norm_mix_g, v_norm_mix_g), norm_mlp_g=(norm_mlp_g, m_norm_mlp_g, v_norm_mlp_g))
    stack = lambda nm: jnp.stack([sg[nm] for sg in small_grads])
    conv_cols = conv_w.shape[2]
    conv_pad = jnp.zeros((depth, CONV_K, 3 * gw - conv_cols), F32)
    widen = lambda a: jnp.concatenate([a, conv_pad], axis=2)
    grads_packed = _pack([stack(nm) for nm in rep_names] + [stack("conv_w")])
    small_job = GatherJob(grads_packed)
    run_jobs([small_job], "ag_small_grads")
    gathered = small_job.results[0]
    packed = [_pack([rep[nm][k] for nm in rep_names] + [widen((conv_w, m_conv_w, v_conv_w)[k])]) for k in range(3)]
    shapes = [rep[nm][0].shape for nm in rep_names] + [(depth, CONV_K, 3 * gw)]
    flat = gathered.reshape(N_DEV, -1)
    n_rep = sum(int(rep[nm][0].size) for nm in rep_names)
    conv_part = flat[:, n_rep:n_rep + depth * CONV_K * 3 * gw].reshape(N_DEV, depth, CONV_K, 3 * gw)
    conv_mine = lax.dynamic_slice_in_dim(conv_part, dev * conv_cols, conv_cols, axis=3)
    conv_mine = jnp.concatenate([conv_mine, jnp.zeros((N_DEV, depth, CONV_K, 3 * gw - conv_cols), F32)], axis=3)
    tail = flat[:, n_rep + depth * CONV_K * 3 * gw:]
    parts = jnp.concatenate([flat[:, :n_rep], conv_mine.reshape(N_DEV, -1), tail], axis=1).reshape(gathered.shape)
    outs_small = small_adam(parts, packed[0], packed[1], packed[2], "adam_small")
    small = [dict(zip(rep_names + ["conv_w"], _unpack(o, shapes))) for o in outs_small]
    for sm in small:
        sm["conv_w"] = sm["conv_w"][:, :, :conv_cols]

    order = ["w_in", "conv_w", "a_log", "dt_bias", "gdn_norm_g", "gmlp_ln_g", "w_spatial", "b_spatial", "sba_q_g",
             "sba_k_g", "w_out_a", "w_out_b", "w_out_c", "w_out", "norm_mix_g", "norm_mlp_g", "w_ff1", "w_ff2"]
    result = [loss, dx[None]]
    for kind in range(4):
        for nm in order:
            if nm == "w_in":
                result.append(jnp.swapaxes(bufs[nm][kind], 1, 2))
            else:
                result.append(bufs[nm][kind] if nm in bufs else small[kind][nm])
    return tuple(result)
```

```python
import functools

import jax
import jax.numpy as jnp
from jax import lax
from jax.experimental import pallas as pl
from jax.experimental.pallas import tpu as pltpu

F32 = jnp.float32
MXU = jnp.bfloat16
N_DEV = 8
AXES = ("x", "y", "c")
CHUNK = 64
HD = 128
CONV_K = 4
EPS = 1e-6
LANE = 128
VMEM_LIMIT = 56 * 1024 * 1024
ADAM_LR, ADAM_B1, ADAM_B2, ADAM_EPS, ADAM_WD, ADAM_STEP = 0.001, 0.9, 0.999, 1e-08, 0.01, 10

_ANY = pl.BlockSpec(memory_space=pl.ANY)
_MESH = pl.DeviceIdType.MESH
_DN = {"nn": (((1,), (0,)), ((), ())), "nt": (((1,), (1,)), ((), ())), "tn": (((0,), (0,)), ((), ()))}


def _cp(*sem):
    return pltpu.CompilerParams(dimension_semantics=sem, vmem_limit_bytes=VMEM_LIMIT)


def _tile(n, cap):
    if n <= cap:
        return n
    best = LANE
    for t in range(LANE, cap + 1, LANE):
        if n % t == 0:
            best = t
    assert n % best == 0, (n, cap)
    return best


def _split(x, n):
    parts, rest = [], x.astype(F32)
    for _ in range(n):
        p = rest.astype(MXU)
        parts.append(p)
        rest = rest - p.astype(F32)
    return parts


def _mm_raw(a, b, mode, prec):
    dot = lambda p, q: lax.dot_general(p, q, _DN[mode], preferred_element_type=F32)
    if prec == "lo" or MXU == F32:
        return dot(a.astype(MXU), b.astype(MXU))
    if prec == "x3":
        (ah, al), (bh, bl) = _split(a, 2), _split(b, 2)
        return dot(ah, bh) + (dot(al, bh) + dot(ah, bl))
    if prec == "la":
        ae, (bh, bl) = a.astype(MXU), _split(b, 2)
        return dot(ae, bh) + dot(ae, bl)
    assert prec == "rb"
    (ah, al), be = _split(a, 2), b.astype(MXU)
    return dot(ah, be) + dot(al, be)


@functools.partial(jax.custom_vjp, nondiff_argnums=(2, 3))
def mm(a, b, mode, prec):
    return _mm_raw(a, b, mode, prec)


def _mm_fwd(a, b, mode, prec):
    return _mm_raw(a, b, mode, prec), (a, b)


def _mm_bwd(mode, prec, res, ct):
    a, b = res
    pa = {"la": None, "rb": "rb"}.get(prec, prec)
    pb = {"la": "la", "rb": None}.get(prec, prec)
    if mode == "nn":
        da = _mm_raw(ct, b, "nt", pa) if pa else None
        db = _mm_raw(a, ct, "tn", pb) if pb else None
    elif mode == "nt":
        da = _mm_raw(ct, b, "nn", pa) if pa else None
        db = _mm_raw(ct, a, "tn", {"la": "rb"}.get(pb, pb)) if pb else None
    else:
        da = _mm_raw(b, ct, "nt", {"rb": "la"}.get(pa, pa)) if pa else None
        db = _mm_raw(a, ct, "nn", pb) if pb else None
    da = jnp.zeros_like(a) if da is None else da.astype(a.dtype)
    db = jnp.zeros_like(b) if db is None else db.astype(b.dtype)
    return da, db


mm.defvjp(_mm_fwd, _mm_bwd)


def _shift_rows(x, j):
    n = x.shape[0]
    row = lax.broadcasted_iota(jnp.int32, x.shape, 0)
    if j > 0:
        return jnp.where(row >= j, pltpu.roll(x, j, 0), 0.0)
    return jnp.where(row < n + j, pltpu.roll(x, n + j, 0), 0.0)


@functools.partial(jax.custom_vjp, nondiff_argnums=(1,))
def shift(x, j):
    return _shift_rows(x, j)


shift.defvjp(lambda x, j: (_shift_rows(x, j), None), lambda j, _, ct: (_shift_rows(ct, -j),))


def _sigmoid(x):
    return 1.0 / (1.0 + jnp.exp(-x))


def _silu(x):
    return x * _sigmoid(x)


def _softplus(x):
    return jnp.maximum(x, 0.0) + jnp.log(1.0 + jnp.exp(-jnp.abs(x)))


def _logsig(x):
    return jnp.minimum(x, 0.0) - jnp.log(1.0 + jnp.exp(-jnp.abs(x)))


def _gelu(x):
    return 0.5 * x * (1.0 + lax.erf(x * (2.0 ** -0.5)))


def _rms(x, g):
    return x * lax.rsqrt(jnp.mean(x * x, axis=-1, keepdims=True) + EPS) * g


def _iota2(shape, dim):
    return lax.broadcasted_iota(jnp.int32, shape, dim)


class GatherJob:
    def __init__(self, x):
        self.inputs = [x]
        self.out_shapes = [jax.ShapeDtypeStruct((N_DEV,) + x.shape, x.dtype)]
        self.sems = [pltpu.SemaphoreType.DMA((7,)), pltpu.SemaphoreType.DMA((7,)), pltpu.SemaphoreType.DMA(())]
        self.results = None

    @staticmethod
    def _plan(ins, outs, sems):
        (x_ref,), (out_ref,), (send_sems, recv_sems, local_sem) = ins, outs, sems
        ix, iy, ic = lax.axis_index("x"), lax.axis_index("y"), lax.axis_index("c")
        me, sibling = (ix, iy, ic), (ix, iy, 1 - ic)
        chips = [(1 - ix, iy), (ix, 1 - iy), (1 - ix, 1 - iy)]

        def slot(px, py, pc):
            return out_ref.at[4 * px + 2 * py + pc]

        def copy(k, block, to, src=None):
            return pltpu.make_async_remote_copy(
                src_ref=slot(*block) if src is None else src, dst_ref=slot(*block),
                send_sem=send_sems.at[k], recv_sem=recv_sems.at[k], device_id=to, device_id_type=_MESH)

        mine = pltpu.make_async_copy(x_ref, slot(*me), local_sem)
        first = [copy(0, me, sibling, src=x_ref)]
        first += [copy(1 + j, me, (*chip, ic), src=x_ref) for j, chip in enumerate(chips)]
        return ic, me, sibling, chips, copy, mine, first

    def start(self, ins, outs, sems):
        *_, mine, first = self._plan(ins, outs, sems)
        mine.start()
        for cp in first:
            cp.start()

    def finish(self, ins, outs, sems):
        ic, me, sibling, chips, copy, mine, first = self._plan(ins, outs, sems)
        passed = [copy(4 + j, (*chip, ic), sibling) for j, chip in enumerate(chips)]
        for j, chip in enumerate(chips):
            copy(1 + j, (*chip, ic), me).wait_recv()
            passed[j].start()
        copy(0, sibling, me).wait_recv()
        for j, chip in enumerate(chips):
            copy(4 + j, (*chip, 1 - ic), me).wait_recv()
        for cp in first + passed:
            cp.wait_send()
        mine.wait()


class ChipExchangeJob:
    def __init__(self, p4):
        self.inputs = [p4]
        self.out_shapes = [jax.ShapeDtypeStruct(p4.shape, p4.dtype)]
        self.sems = [pltpu.SemaphoreType.DMA((3,)), pltpu.SemaphoreType.DMA((3,)), pltpu.SemaphoreType.DMA(())]
        self.results = None

    @staticmethod
    def _plan(ins, outs, sems):
        (p_ref,), (r_ref,), (send_sems, recv_sems, local_sem) = ins, outs, sems
        ix, iy, ic = lax.axis_index("x"), lax.axis_index("y"), lax.axis_index("c")
        my_xy = 2 * ix + iy
        local = pltpu.make_async_copy(p_ref.at[my_xy], r_ref.at[my_xy], local_sem)
        chips = [(1 - ix, iy), (ix, 1 - iy), (1 - ix, 1 - iy)]
        copies = [
            pltpu.make_async_remote_copy(
                src_ref=p_ref.at[2 * px + py], dst_ref=r_ref.at[my_xy],
                send_sem=send_sems.at[k], recv_sem=recv_sems.at[k],
                device_id=(px, py, ic), device_id_type=_MESH)
            for k, (px, py) in enumerate(chips)
        ]
        return local, copies

    def start(self, ins, outs, sems):
        local, copies = self._plan(ins, outs, sems)
        local.start()
        for cp in copies:
            cp.start()

    def finish(self, ins, outs, sems):
        local, copies = self._plan(ins, outs, sems)
        for cp in copies:
            cp.wait()
        local.wait()


def _each_job(jobs, method, ins, outs, sems):
    i = o = s = 0
    for job in jobs:
        ni, no, ns = len(job.inputs), len(job.out_shapes), len(job.sems)
        getattr(job, method)(ins[i:i + ni], outs[o:o + no], sems[s:s + ns])
        i, o, s = i + ni, o + no, s + ns


def run_jobs(jobs, name):
    j_in = [a for job in jobs for a in job.inputs]
    j_out = [sh for job in jobs for sh in job.out_shapes]
    j_sem = [sm for job in jobs for sm in job.sems]

    def body(*refs):
        ins, outs, sems = refs[:len(j_in)], refs[len(j_in):len(j_in) + len(j_out)], refs[len(j_in) + len(j_out):]
        _each_job(jobs, "start", ins, outs, sems)
        _each_job(jobs, "finish", ins, outs, sems)

    res = pl.pallas_call(body, name=name, out_shape=j_out, in_specs=[_ANY] * len(j_in), out_specs=[_ANY] * len(j_out),
                         scratch_shapes=j_sem)(*j_in)
    _hand_out(jobs, res)


def _hand_out(jobs, res):
    o = 0
    for job in jobs:
        job.results = list(res[o:o + len(job.out_shapes)])
        o += len(job.out_shapes)


def hosted_call(body, args, *, name, grid, in_specs, out_specs, out_shape, scratch_shapes=(), sem=None, jobs=()):
    outs_l, specs_l = list(out_shape), list(out_specs)
    if not jobs:
        return pl.pallas_call(body, name=name, grid=grid, in_specs=list(in_specs), out_specs=specs_l, out_shape=outs_l,
                              scratch_shapes=list(scratch_shapes), compiler_params=_cp(*sem))(*args)
    j_in = [a for job in jobs for a in job.inputs]
    j_out = [sh for job in jobs for sh in job.out_shapes]
    j_sem = [sm for job in jobs for sm in job.sems]
    n_in, n_out, n_scr = len(in_specs), len(outs_l), len(scratch_shapes)

    def wrapped(*refs):
        pos = [0]

        def take(n):
            pos[0] += n
            return refs[pos[0] - n:pos[0]]

        ins, jin, outs, jout, scr, jsem = take(n_in), take(len(j_in)), take(n_out), take(len(j_out)), take(n_scr), take(len(j_sem))
        ids = [pl.program_id(a) for a in range(len(grid))]
        first = functools.reduce(lambda p, q: p & q, [i == 0 for i in ids])
        last = functools.reduce(lambda p, q: p & q, [i == g - 1 for i, g in zip(ids, grid)])

        @pl.when(first)
        def _():
            _each_job(jobs, "start", jin, jout, jsem)

        body(*ins, *outs, *scr)

        @pl.when(last)
        def _():
            _each_job(jobs, "finish", jin, jout, jsem)

    res = pl.pallas_call(
        wrapped, name=name, grid=grid, in_specs=list(in_specs) + [_ANY] * len(j_in),
        out_specs=specs_l + [_ANY] * len(j_out), out_shape=outs_l + j_out,
        scratch_shapes=list(scratch_shapes) + j_sem, compiler_params=_cp(*["arbitrary"] * len(grid)),
    )(*args, *j_in)
    _hand_out(jobs, res[n_out:])
    return list(res[:n_out])


def rs_pair_exchange(g8, name):
    def body(g_ref, r_ref, send_sems, recv_sems):
        ix, iy, ic = lax.axis_index("x"), lax.axis_index("y"), lax.axis_index("c")
        copies = [
            pltpu.make_async_remote_copy(
                src_ref=g_ref.at[2 * xy + (1 - ic)], dst_ref=r_ref.at[xy],
                send_sem=send_sems.at[xy], recv_sem=recv_sems.at[xy],
                device_id=(ix, iy, 1 - ic), device_id_type=_MESH)
            for xy in range(4)
        ]
        for cp in copies:
            cp.start()
        for cp in copies:
            cp.wait()

    return pl.pallas_call(
        body, name=name, out_shape=jax.ShapeDtypeStruct((4,) + g8.shape[1:], g8.dtype),
        in_specs=[_ANY], out_specs=_ANY,
        scratch_shapes=[pltpu.SemaphoreType.DMA((4,)), pltpu.SemaphoreType.DMA((4,))],
    )(g8)


def pair_sum(g8, r4, c_idx, name):
    _, rows, cols = g8.shape
    tr, tc = _tile_2d(rows, cols)

    def body(c_ref, g_ref, r_ref, o_ref):
        o_ref[...] = (g_ref[...].astype(F32) + r_ref[...].astype(F32)).astype(o_ref.dtype)

    grid_spec = pltpu.PrefetchScalarGridSpec(
        num_scalar_prefetch=1, grid=(4, rows // tr, cols // tc),
        in_specs=[pl.BlockSpec((None, tr, tc), lambda s, i, j, c: (2 * s + c[0], i, j)),
                  pl.BlockSpec((None, tr, tc), lambda s, i, j, c: (s, i, j))],
        out_specs=pl.BlockSpec((None, tr, tc), lambda s, i, j, c: (s, i, j)))
    return pl.pallas_call(
        body, name=name, grid_spec=grid_spec, out_shape=jax.ShapeDtypeStruct((4, rows, cols), g8.dtype),
        compiler_params=_cp("parallel", "parallel", "parallel"),
    )(c_idx, g8, r4)


def _tile_2d(rows, cols):
    budget = 128 * 2048
    tr, tc = rows, cols
    if rows % 16 == 0:
        while tr * cols > budget and tr % 32 == 0:
            tr //= 2
    else:
        while rows * tc > budget and tc % (2 * LANE) == 0:
            tc //= 2
    return tr, tc


def _adam_math(w, g, m, v):
    m2 = ADAM_B1 * m + (1.0 - ADAM_B1) * g
    v2 = ADAM_B2 * v + (1.0 - ADAM_B2) * (g * g)
    m_hat = m2 / (1.0 - ADAM_B1 ** ADAM_STEP)
    v_hat = v2 / (1.0 - ADAM_B2 ** ADAM_STEP)
    delta = -ADAM_LR * (m_hat / (jnp.sqrt(v_hat) + ADAM_EPS) + ADAM_WD * w)
    return delta, m2, v2


def adam_layer(parts, w, m, v, bufs, layer, name):
    n_parts, rows, cols = parts.shape
    tr, tc = _tile_2d(rows, cols)

    def body(p_ref, w_ref, m_ref, v_ref, g_in, d_in, m_in, v_in, g_out, d_out, m_out, v_out):
        g = p_ref[0].astype(F32)
        for k in range(1, n_parts):
            g = g + p_ref[k].astype(F32)
        delta, m2, v2 = _adam_math(w_ref[...], g, m_ref[...], v_ref[...])
        g_out[...] = g
        d_out[...] = delta
        m_out[...] = m2
        v_out[...] = v2

    lay = pl.BlockSpec((None, tr, tc), lambda i, j: (layer, i, j))
    return pl.pallas_call(
        body, name=name, grid=(rows // tr, cols // tc),
        in_specs=[pl.BlockSpec((n_parts, tr, tc), lambda i, j: (0, i, j)), lay, lay, lay, _ANY, _ANY, _ANY, _ANY],
        out_specs=[lay, lay, lay, lay],
        out_shape=[jax.ShapeDtypeStruct(w.shape, F32)] * 4,
        input_output_aliases={4: 0, 5: 1, 6: 2, 7: 3},
        compiler_params=_cp("parallel", "parallel"),
    )(parts, w, m, v, *bufs)


def matmul(a, b, mode, out_dtype, name, res=None, caps=(1024, 1024, 2048), b_view=None, out_slabs=None, jobs=(), post=None):
    if mode == "tn":
        k_dim, m_dim = a.shape
    else:
        m_dim, k_dim = a.shape
    if b_view is None:
        b_rows, b_cols = b.shape
    else:
        kind = b_view
        shard_r, shard_c = b.shape[1:]
        b_rows, b_cols = (shard_r, N_DEV * shard_c) if kind == "cols" else (N_DEV * shard_r, shard_c)
    n_dim = b_rows if mode == "nt" else b_cols
    cap_n, cap_k = caps[1], caps[2]
    tn = _tile(out_slabs, cap_n) if out_slabs else _tile(n_dim, cap_n)
    tk = _tile(k_dim, cap_k)
    stacked = 0
    if b_view is not None:
        along_n = (kind == "cols") == (mode == "nn")
        shard_len = shard_c if kind == "cols" else shard_r
        if along_n:
            tn = _tile(shard_len, cap_n)
        elif kind == "rows" and tk >= shard_r and shard_r % 16 == 0:
            stacked = tk // shard_r
        else:
            tk = _tile(shard_len, cap_k)
    tm = _tile(m_dim, caps[0])
    nk = k_dim // tk

    def body(*refs):
        a_ref, b_ref = refs[:2]
        r_ref = refs[2] if res is not None else None
        o_idx = 3 if res is not None else 2
        o_ref = refs[o_idx]

        def emit(r):
            if post == "sqrelu_bwd":
                r = r * (2.0 * jnp.maximum(r_ref[...], 0.0))
            elif res is not None:
                r = r + r_ref[...]
            o_ref[...] = r.astype(out_dtype)
            if post == "sqrelu":
                act = jnp.maximum(r, 0.0)
                refs[o_idx + 1][...] = (act * act).astype(MXU)

        b_val = b_ref[...].reshape(tk, tn) if stacked else b_ref[...]
        part = lax.dot_general(a_ref[...], b_val, _DN[mode], preferred_element_type=F32)
        if nk == 1:
            emit(part)
            return
        acc = refs[-1]
        k = pl.program_id(2)

        @pl.when(k == 0)
        def _():
            acc[...] = part

        @pl.when(k > 0)
        def _():
            acc[...] += part

        @pl.when(k == nk - 1)
        def _():
            emit(acc[...])

    a_spec = pl.BlockSpec((tk, tm), lambda i, j, k: (k, i)) if mode == "tn" else pl.BlockSpec((tm, tk), lambda i, j, k: (i, k))
    b_blk = (tn, tk) if mode == "nt" else (tk, tn)
    b_pos = (lambda i, j, k: (j, k)) if mode == "nt" else (lambda i, j, k: (k, j))
    if b_view is None:
        b_spec = pl.BlockSpec(b_blk, b_pos)
    elif stacked:
        b_spec = pl.BlockSpec((stacked, shard_r, tn), lambda i, j, k: (k, 0, j))
    elif kind == "cols":
        per = shard_c // b_blk[1]
        b_spec = pl.BlockSpec((None,) + b_blk, lambda i, j, k: (b_pos(i, j, k)[1] // per, b_pos(i, j, k)[0], b_pos(i, j, k)[1] % per))
    else:
        per = shard_r // b_blk[0]
        b_spec = pl.BlockSpec((None,) + b_blk, lambda i, j, k: (b_pos(i, j, k)[0] // per, b_pos(i, j, k)[0] % per, b_pos(i, j, k)[1]))
    if out_slabs:
        per_o = out_slabs // tn
        o_spec = pl.BlockSpec((None, tm, tn), lambda i, j, k: (j // per_o, i, j % per_o))
        out_shape = jax.ShapeDtypeStruct((n_dim // out_slabs, m_dim, out_slabs), out_dtype)
    else:
        o_spec = pl.BlockSpec((tm, tn), lambda i, j, k: (i, j))
        out_shape = jax.ShapeDtypeStruct((m_dim, n_dim), out_dtype)
    in_specs, args = [a_spec, b_spec], [a, b]
    if res is not None:
        in_specs.append(o_spec)
        args.append(res)
    out_specs, out_shapes = [o_spec], [out_shape]
    if post == "sqrelu":
        out_specs.append(o_spec)
        out_shapes.append(jax.ShapeDtypeStruct(out_shape.shape, MXU))
    outs = hosted_call(
        body, args, name=name, grid=(m_dim // tm, n_dim // tn, nk), in_specs=in_specs, out_specs=out_specs,
        out_shape=out_shapes, scratch_shapes=[pltpu.VMEM((tm, tn), F32)] if nk > 1 else [],
        sem=("parallel", "parallel", "arbitrary"), jobs=jobs)
    return tuple(outs) if post == "sqrelu" else outs[0]


def rms_fwd(x, gain, name):
    t, d = x.shape
    tt = _tile(t, 256)

    def body(x_ref, g_ref, o_ref):
        o_ref[...] = _rms(x_ref[...], g_ref[...]).astype(o_ref.dtype)

    return pl.pallas_call(
        body, name=name, grid=(t // tt,),
        in_specs=[pl.BlockSpec((tt, d), lambda i: (i, 0)), pl.BlockSpec((1, d), lambda i: (0, 0))],
        out_specs=pl.BlockSpec((tt, d), lambda i: (i, 0)),
        out_shape=jax.ShapeDtypeStruct((t, d), MXU), compiler_params=_cp("parallel"),
    )(x, gain)


def rms_bwd(x, gain, dh, dres, name):
    t, d = x.shape
    tt = _tile(t, 256)

    def body(x_ref, g_ref, dh_ref, dr_ref, dx_ref, dxb_ref, dg_ref):
        _, vjp = jax.vjp(_rms, x_ref[...], g_ref[...])
        dx, dg = vjp(dh_ref[...])
        dx = dx + dr_ref[...]
        dx_ref[...] = dx
        dxb_ref[...] = dx.astype(dxb_ref.dtype)

        @pl.when(pl.program_id(0) == 0)
        def _():
            dg_ref[...] = jnp.zeros_like(dg_ref)

        dg_ref[...] += dg

    row = pl.BlockSpec((tt, d), lambda i: (i, 0))
    vec = pl.BlockSpec((1, d), lambda i: (0, 0))
    return pl.pallas_call(
        body, name=name, grid=(t // tt,), in_specs=[row, vec, row, row], out_specs=[row, row, vec],
        out_shape=[jax.ShapeDtypeStruct((t, d), F32), jax.ShapeDtypeStruct((t, d), MXU), jax.ShapeDtypeStruct((1, d), F32)],
        compiler_params=_cp("arbitrary"),
    )(x, gain, dh, dres)


def loss_head(x, target, name):
    t, d = x.shape
    tt = _tile(t, 256)

    def body(x_ref, t_ref, dx_ref, dxb_ref, l_ref):
        e = x_ref[...] - t_ref[...]
        dx = e * (1.0 / d)
        dx_ref[...] = dx
        dxb_ref[...] = dx.astype(dxb_ref.dtype)

        @pl.when(pl.program_id(0) == 0)
        def _():
            l_ref[...] = jnp.zeros_like(l_ref)

        part = jnp.sum(jnp.sum(e * e, axis=-1, keepdims=True) * (1.0 / d), axis=0, keepdims=True)
        l_ref[...] += 0.5 * part

    row = pl.BlockSpec((tt, d), lambda i: (i, 0))
    return pl.pallas_call(
        body, name=name, grid=(t // tt,), in_specs=[row, row],
        out_specs=[row, row, pl.BlockSpec((8, LANE), lambda i: (0, 0))],
        out_shape=[jax.ShapeDtypeStruct((t, d), F32), jax.ShapeDtypeStruct((t, d), MXU), jax.ShapeDtypeStruct((8, LANE), F32)],
        compiler_params=_cp("arbitrary"),
    )(x, target)


def _merge_f(g0, g1, g2, ba, bb, bc):
    return _sigmoid(g0) * ba + _sigmoid(g1) * bb + _sigmoid(g2) * bc


def merge_fwd(z, off, ba, bb, bc, name):
    t, d = ba.shape
    tt, td = _tile(t, 256), _tile(d // 2, 1024)
    nd, ob = d // td, off // td

    def body(g0, g1, g2, a, b, c, o_ref):
        o_ref[...] = _merge_f(g0[...], g1[...], g2[...], a[...], b[...], c[...]).astype(o_ref.dtype)

    gates = [pl.BlockSpec((tt, td), functools.partial(lambda i, j, s: (i, ob + s * nd + j), s=s)) for s in range(3)]
    blk = pl.BlockSpec((tt, td), lambda i, j: (i, j))
    return pl.pallas_call(body, name=name, grid=(t // tt, nd), in_specs=gates + [blk] * 3, out_specs=blk,
                          out_shape=jax.ShapeDtypeStruct((t, d), MXU), compiler_params=_cp("parallel", "parallel"))(z, z, z, ba, bb, bc)


def merge_bwd(z, off, ba, bb, bc, dy, name):
    t, d = ba.shape
    tt, td = _tile(t, 256), _tile(d // 2, 1024)
    nd, ob = d // td, off // td

    def body(g0, g1, g2, a, b, c, dy_ref, dgl, da, db, dc):
        _, vjp = jax.vjp(_merge_f, g0[...], g1[...], g2[...], a[...], b[...], c[...])
        d0, d1, d2, xa, xb, xc = vjp(dy_ref[...])
        for s, dv in enumerate((d0, d1, d2)):
            dgl[s] = dv.astype(dgl.dtype)
        da[...] = xa.astype(da.dtype)
        db[...] = xb.astype(db.dtype)
        dc[...] = xc.astype(dc.dtype)

    gates = [pl.BlockSpec((tt, td), functools.partial(lambda i, j, s: (i, ob + s * nd + j), s=s)) for s in range(3)]
    blk = pl.BlockSpec((tt, td), lambda i, j: (i, j))
    dgl, da, db, dc = pl.pallas_call(
        body, name=name, grid=(t // tt, nd), in_specs=gates + [blk] * 4,
        out_specs=[pl.BlockSpec((3, tt, td), lambda i, j: (0, i, j)), blk, blk, blk],
        out_shape=[jax.ShapeDtypeStruct((3, t, d), MXU)] + [jax.ShapeDtypeStruct((t, d), MXU)] * 3,
        compiler_params=_cp("parallel", "parallel"),
    )(z, z, z, ba, bb, bc, dy)
    return dgl, da, db, dc


def _gdn_pre_f(qp, kp, vp, ab, cq, ck, cv, alog, dtb, h, n_heads):
    def conv(xp, cw):
        acc = xp * cw[CONV_K - 1]
        for j in range(1, CONV_K):
            acc = acc + shift(xp, j) * cw[CONV_K - 1 - j]
        return _silu(acc)

    q, k, v = conv(qp, cq), conv(kp, ck), conv(vp, cv)
    q = q * lax.rsqrt(jnp.sum(q * q, axis=-1, keepdims=True) + EPS) * (HD ** -0.5)
    k = k * lax.rsqrt(jnp.sum(k * k, axis=-1, keepdims=True) + EPS)
    lane = _iota2(ab.shape, 1)
    a_col = jnp.sum(jnp.where(lane == h, ab, 0.0), axis=-1, keepdims=True)
    b_col = jnp.sum(jnp.where(lane == n_heads + h, ab, 0.0), axis=-1, keepdims=True)
    lane1 = _iota2(alog.shape, 1)
    al = jnp.sum(jnp.where(lane1 == h, alog, 0.0), axis=-1, keepdims=True)
    dt = jnp.sum(jnp.where(lane1 == h, dtb, 0.0), axis=-1, keepdims=True)
    g = -jnp.exp(al) * _softplus(a_col + dt)
    return q, k, v, g, _sigmoid(b_col)


def _gdn_pre_specs(t, n_heads, ab_blk):
    zq = [pl.BlockSpec((t, HD), functools.partial(lambda h, s: (0, s * n_heads + h), s=s)) for s in range(3)]
    ab = pl.BlockSpec((t, LANE), lambda h: (0, ab_blk))
    cw = [pl.BlockSpec((CONV_K, HD), functools.partial(lambda h, s: (0, s * n_heads + h), s=s)) for s in range(3)]
    vec = pl.BlockSpec((1, LANE), lambda h: (0, 0))
    return zq, ab, cw, vec


def gdn_pre_fwd(z, ab_blk, conv_w, alog, dtb, n_heads, name):
    t = z.shape[0]
    zq, ab, cw, vec = _gdn_pre_specs(t, n_heads, ab_blk)

    def body(qp, kp, vp, ab_ref, cq, ck, cv, al, dt, q_o, k_o, v_o, g_o, b_o):
        rows = lambda r: tuple(r[j:j + 1, :] for j in range(CONV_K))
        outs = _gdn_pre_f(qp[...], kp[...], vp[...], ab_ref[...], rows(cq), rows(ck), rows(cv), al[...], dt[...],
                          pl.program_id(0), n_heads)
        for o_ref, val in zip((q_o, k_o, v_o, g_o, b_o), outs):
            o_ref[...] = val

    head = pl.BlockSpec((None, t, HD), lambda h: (h, 0, 0))
    col = pl.BlockSpec((None, t, 1), lambda h: (h, 0, 0))
    return pl.pallas_call(
        body, name=name, grid=(n_heads,), in_specs=zq + [ab] + cw + [vec, vec], out_specs=[head] * 3 + [col] * 2,
        out_shape=[jax.ShapeDtypeStruct((n_heads, t, HD), F32)] * 3 + [jax.ShapeDtypeStruct((n_heads, t, 1), F32)] * 2,
        compiler_params=_cp("parallel"),
    )(z, z, z, z, conv_w, conv_w, conv_w, alog, dtb)


def gdn_pre_bwd(z, ab_blk, conv_w, alog, dtb, n_heads, dq, dk, dv, dg, db, name):
    t = z.shape[0]
    gw = n_heads * HD
    zq, ab, cw, vec = _gdn_pre_specs(t, n_heads, ab_blk)

    def body(qp, kp, vp, ab_ref, cq, ck, cv, al, dt, dq_r, dk_r, dv_r, dg_r, db_r,
             dqp, dkp, dvp, dab, dcq, dck, dcv, dal, ddt):
        h = pl.program_id(0)
        rows = lambda r: tuple(r[j:j + 1, :] for j in range(CONV_K))
        f = functools.partial(_gdn_pre_f, h=h, n_heads=n_heads)
        _, vjp = jax.vjp(f, qp[...], kp[...], vp[...], ab_ref[...], rows(cq), rows(ck), rows(cv), al[...], dt[...])
        gq, gk, gv, gab, gcq, gck, gcv, gal, gdt = vjp((dq_r[...], dk_r[...], dv_r[...], dg_r[...], db_r[...]))
        dqp[...] = gq.astype(dqp.dtype)
        dkp[...] = gk.astype(dkp.dtype)
        dvp[...] = gv.astype(dvp.dtype)
        for ref, gr in ((dcq, gcq), (dck, gck), (dcv, gcv)):
            for j in range(CONV_K):
                ref[j:j + 1, :] = gr[j]

        @pl.when(h == 0)
        def _():
            dab[...] = jnp.zeros_like(dab)
            dal[...] = jnp.zeros_like(dal)
            ddt[...] = jnp.zeros_like(ddt)

        dab[...] += gab
        dal[...] += gal
        ddt[...] += gdt

    head = pl.BlockSpec((None, t, HD), lambda h: (h, 0, 0))
    col = pl.BlockSpec((None, t, 1), lambda h: (h, 0, 0))
    seg = pl.BlockSpec((t, HD), lambda h: (0, h))
    cseg = pl.BlockSpec((CONV_K, HD), lambda h: (0, h))
    return pl.pallas_call(
        body, name=name, grid=(n_heads,),
        in_specs=zq + [ab] + cw + [vec, vec] + [head] * 3 + [col] * 2,
        out_specs=[seg] * 3 + [pl.BlockSpec((t, LANE), lambda h: (0, 0))] + [cseg] * 3 + [vec, vec],
        out_shape=[jax.ShapeDtypeStruct((t, gw), MXU)] * 3 + [jax.ShapeDtypeStruct((t, LANE), F32)]
        + [jax.ShapeDtypeStruct((CONV_K, gw), F32)] * 3 + [jax.ShapeDtypeStruct((1, LANE), F32)] * 2,
        compiler_params=_cp("arbitrary"),
    )(z, z, z, z, conv_w, conv_w, conv_w, alog, dtb, dq, dk, dv, dg, db)


def _gdn_chunk_f(q, k, v, g, b):
    c = CHUNK
    r, s = _iota2((c, c), 0), _iota2((c, c), 1)
    tril = (s <= r).astype(F32)
    gc_w = mm(tril, jnp.broadcast_to(g, (c, HD)), "nn", "la")
    gc_i = mm(tril, jnp.broadcast_to(g, (c, c)), "nn", "la")
    gc_j = mm(jnp.ones((c, c), F32), jnp.where(r <= s, jnp.broadcast_to(g, (c, c)), 0.0), "nn", "la")
    decay = jnp.exp(jnp.where(s <= r, gc_i - gc_j, -1e30))
    kb = k * b
    low = jnp.where(s < r, mm(kb, k, "nt", "x3") * decay, 0.0)
    inv = jnp.where(r == s, 1.0, 0.0) - low
    pw = mm(low, low, "nn", "x3")
    n_sq = 1
    while 2 * n_sq < c:
        inv = inv + mm(inv, pw, "nn", "x3")
        n_sq *= 2
        if 2 * n_sq < c:
            pw = mm(pw, pw, "nn", "x3")
    egc = jnp.exp(gc_w)
    u = mm(inv, v * b, "nn", "x3")
    w = mm(inv, kb * egc, "nn", "x3")
    intra = mm(q, k, "nt", "lo") * decay
    g_last = jnp.sum(g, axis=0, keepdims=True)
    kd = k * jnp.exp(g_last - gc_w)
    egl = jnp.exp(jnp.broadcast_to(g_last, (1, HD)))
    return u, w, intra, q * egc, kd, egl


def _group(n, cap=4):
    return max(g for g in range(1, cap + 1) if n % g == 0)


def _chunk_specs(nb_h, nb_c, n_chunks=None):
    cn = (lambda n: n) if n_chunks is None else (lambda n: n_chunks // nb_c - 1 - n)
    rows = nb_c * CHUNK
    vec = pl.BlockSpec((nb_h, rows, HD), lambda h, n: (h, cn(n), 0))
    col = pl.BlockSpec((nb_h, rows, 1), lambda h, n: (h, cn(n), 0))
    sq = pl.BlockSpec((nb_h, rows, CHUNK), lambda h, n: (h, cn(n), 0))
    one = pl.BlockSpec((nb_h, nb_c, 1, HD), lambda h, n: (h, cn(n), 0, 0))
    st = pl.BlockSpec((nb_h, nb_c, HD, HD), lambda h, n: (h, cn(n), 0, 0))
    return vec, col, sq, one, st


def _chunk_shapes(n_heads, t):
    vec = jax.ShapeDtypeStruct((n_heads, t, HD), F32)
    return [vec, vec, jax.ShapeDtypeStruct((n_heads, t, CHUNK), F32), vec, vec,
            jax.ShapeDtypeStruct((n_heads, t // CHUNK, 1, HD), F32)]


def _chunk_rows(ci):
    return slice(ci * CHUNK, (ci + 1) * CHUNK)


def gdn_chunk_fwd(q, k, v, g, b, name, jobs=()):
    n_heads, t, _ = q.shape
    nb_c = _group(t // CHUNK)
    vec, col, sq, one, _ = _chunk_specs(1, nb_c)

    def body(q_r, k_r, v_r, g_r, b_r, *outs):
        for ci in range(nb_c):
            rows = _chunk_rows(ci)
            vals = _gdn_chunk_f(*(r[0, rows, :] for r in (q_r, k_r, v_r, g_r, b_r)))
            for o_ref, val in zip(outs[:5], vals[:5]):
                o_ref[0, rows, :] = val
            outs[5][0, ci] = vals[5]

    return hosted_call(
        body, (q, k, v, g, b), name=name, grid=(n_heads, t // CHUNK // nb_c), in_specs=[vec] * 3 + [col] * 2,
        out_specs=[vec, vec, sq, vec, vec, one], out_shape=_chunk_shapes(n_heads, t),
        sem=("parallel", "parallel"), jobs=jobs)


def gdn_chunk_bwd(q, k, v, g, b, cts, name, jobs=()):
    n_heads, t, _ = q.shape
    nb_c = _group(t // CHUNK)
    vec, col, sq, one, _ = _chunk_specs(1, nb_c)

    def body(q_r, k_r, v_r, g_r, b_r, du, dw, di, dqd, dkd, degl, dq, dk, dv, dg, db):
        for ci in range(nb_c):
            rows = _chunk_rows(ci)
            _, vjp = jax.vjp(_gdn_chunk_f, *(r[0, rows, :] for r in (q_r, k_r, v_r, g_r, b_r)))
            grads = vjp(tuple(r[0, rows, :] for r in (du, dw, di, dqd, dkd)) + (degl[0, ci],))
            for o_ref, val in zip((dq, dk, dv, dg, db), grads):
                o_ref[0, rows, :] = val

    col_shape = jax.ShapeDtypeStruct((n_heads, t, 1), F32)
    return hosted_call(
        body, (q, k, v, g, b, *cts), name=name, grid=(n_heads, t // CHUNK // nb_c),
        in_specs=[vec] * 3 + [col] * 2 + [vec, vec, sq, vec, vec, one],
        out_specs=[vec] * 3 + [col] * 2,
        out_shape=[jax.ShapeDtypeStruct((n_heads, t, HD), F32)] * 3 + [col_shape] * 2,
        sem=("parallel", "parallel"), jobs=jobs)


def _scan_f(s, u, w, a, qd, kd, egl):
    vn = u - mm(w, s, "nn", "lo")
    o = mm(qd, s, "nn", "lo") + mm(a, vn, "nn", "lo")
    return o, s * egl + mm(kd, vn, "tn", "lo")


def gdn_scan_fwd(chunks, name):
    u = chunks[0]
    n_heads, t, _ = u.shape
    nc = t // CHUNK
    nb_h = _group(n_heads)
    vec, _, sq, one, st = _chunk_specs(nb_h, 1)

    def body(u_r, w_r, a_r, qd_r, kd_r, e_r, o_ref, s_ref, state):
        @pl.when(pl.program_id(1) == 0)
        def _():
            state[...] = jnp.zeros_like(state)

        for hh in range(nb_h):
            s = state[hh]
            s_ref[hh, 0] = s
            o, s2 = _scan_f(s, u_r[hh], w_r[hh], a_r[hh], qd_r[hh], kd_r[hh], e_r[hh, 0])
            o_ref[hh] = o
            state[hh] = s2

    return pl.pallas_call(
        body, name=name, grid=(n_heads // nb_h, nc), in_specs=[vec, vec, sq, vec, vec, one], out_specs=[vec, st],
        out_shape=[jax.ShapeDtypeStruct((n_heads, t, HD), F32), jax.ShapeDtypeStruct((n_heads, nc, HD, HD), F32)],
        scratch_shapes=[pltpu.VMEM((nb_h, HD, HD), F32)], compiler_params=_cp("parallel", "arbitrary"),
    )(*chunks)


def gdn_scan_bwd(chunks, states, do, name):
    n_heads, t, _ = do.shape
    nc = t // CHUNK
    nb_h = _group(n_heads)
    vec, _, sq, one, st = _chunk_specs(nb_h, 1, n_chunks=nc)

    def body(u_r, w_r, a_r, qd_r, kd_r, e_r, s_r, do_r, du, dw, da, dqd, dkd, de, dstate):
        @pl.when(pl.program_id(1) == 0)
        def _():
            dstate[...] = jnp.zeros_like(dstate)

        for hh in range(nb_h):
            _, vjp = jax.vjp(_scan_f, s_r[hh, 0], u_r[hh], w_r[hh], a_r[hh], qd_r[hh], kd_r[hh], e_r[hh, 0])
            grads = vjp((do_r[hh], dstate[hh]))
            dstate[hh] = grads[0]
            for o_ref, val in zip((du, dw, da, dqd, dkd), grads[1:6]):
                o_ref[hh] = val
            de[hh, 0] = grads[6]

    return pl.pallas_call(
        body, name=name, grid=(n_heads // nb_h, nc), in_specs=[vec, vec, sq, vec, vec, one, st, vec],
        out_specs=[vec, vec, sq, vec, vec, one], out_shape=_chunk_shapes(n_heads, t),
        scratch_shapes=[pltpu.VMEM((nb_h, HD, HD), F32)], compiler_params=_cp("parallel", "arbitrary"),
    )(*chunks, states, do)


def _post_f(o, gate, gain):
    return _rms(o, gain) * _silu(gate)


def gdn_post_fwd(o, z, gate_blk, gain, name):
    n_heads, t, _ = o.shape
    tt = _tile(t, 512)

    def body(o_r, gt_r, gn_r, out):
        out[...] = _post_f(o_r[...], gt_r[...], gn_r[...]).astype(out.dtype)

    return pl.pallas_call(
        body, name=name, grid=(n_heads, t // tt),
        in_specs=[pl.BlockSpec((None, tt, HD), lambda h, i: (h, i, 0)), pl.BlockSpec((tt, HD), lambda h, i: (i, gate_blk + h)),
                  pl.BlockSpec((1, HD), lambda h, i: (0, 0))],
        out_specs=pl.BlockSpec((tt, HD), lambda h, i: (i, h)),
        out_shape=jax.ShapeDtypeStruct((t, n_heads * HD), MXU), compiler_params=_cp("parallel", "parallel"),
    )(o, z, gain)


def gdn_post_bwd(o, z, gate_blk, gain, doa, name):
    n_heads, t, _ = o.shape
    tt = _tile(t, 512)

    def body(o_r, gt_r, gn_r, d_r, do_ref, dgt_ref, dgn_ref):
        _, vjp = jax.vjp(_post_f, o_r[...], gt_r[...], gn_r[...])
        go, ggt, ggn = vjp(d_r[...])
        do_ref[...] = go
        dgt_ref[...] = ggt.astype(dgt_ref.dtype)

        @pl.when((pl.program_id(0) == 0) & (pl.program_id(1) == 0))
        def _():
            dgn_ref[...] = jnp.zeros_like(dgn_ref)

        dgn_ref[...] += ggn

    tok = pl.BlockSpec((tt, HD), lambda h, i: (i, h))
    vec = pl.BlockSpec((1, HD), lambda h, i: (0, 0))
    head = pl.BlockSpec((None, tt, HD), lambda h, i: (h, i, 0))
    return pl.pallas_call(
        body, name=name, grid=(n_heads, t // tt),
        in_specs=[head, pl.BlockSpec((tt, HD), lambda h, i: (i, gate_blk + h)), vec, tok],
        out_specs=[head, tok, vec],
        out_shape=[jax.ShapeDtypeStruct((n_heads, t, HD), F32), jax.ShapeDtypeStruct((t, n_heads * HD), MXU),
                   jax.ShapeDtypeStruct((1, HD), F32)],
        compiler_params=_cp("arbitrary", "arbitrary"),
    )(o, z, gain, doa)


def _gmlp_f(ups, vps, lngs, wss, bcols):
    n_groups = len(ups)
    width = HD * n_groups
    us = [_gelu(a) for a in ups]
    vs = [_gelu(a) for a in vps]
    mu = sum(jnp.sum(a, axis=-1, keepdims=True) for a in vs) * (1.0 / width)
    xcs = [a - mu for a in vs]
    var = sum(jnp.sum(a * a, axis=-1, keepdims=True) for a in xcs) * (1.0 / width)
    rstd = lax.rsqrt(var + EPS)
    r, s = _iota2((HD, HD), 0), _iota2((HD, HD), 1)
    causal = (s // CHUNK) <= (r // CHUNK)
    outs = []
    for gi in range(n_groups):
        vb = xcs[gi] * rstd * lngs[gi]
        sp = mm(jnp.where(causal, wss[gi], 0.0), vb, "nn", "lo") + bcols[gi]
        outs.append(us[gi] * sp)
    return tuple(outs)


def _gmlp_load(uv_u, uv_v, lng, ws, bt, n_groups):
    seg = lambda ref, gi: ref[:, gi * HD:(gi + 1) * HD]
    return ([seg(uv_u, gi) for gi in range(n_groups)], [seg(uv_v, gi) for gi in range(n_groups)],
            [seg(lng, gi) for gi in range(n_groups)], [ws[gi] for gi in range(n_groups)],
            [bt[:, gi:gi + 1] for gi in range(n_groups)])


def _gmlp_specs(width, u_blk, n_groups):
    u = pl.BlockSpec((HD, width), lambda i: (i, u_blk))
    v = pl.BlockSpec((HD, width), lambda i: (i, u_blk + 1))
    lng = pl.BlockSpec((1, width), lambda i: (0, 0))
    ws = pl.BlockSpec((n_groups, HD, HD), lambda i: (0, 0, 0))
    bt = pl.BlockSpec((HD, LANE), lambda i: (0, 0))
    return u, v, lng, ws, bt


def gmlp_fwd(z, uv_off, width, lng, ws, bt, name):
    t = z.shape[0]
    n_groups = width // HD
    specs = _gmlp_specs(width, uv_off // width, n_groups)

    def body(u_r, v_r, l_r, w_r, b_r, out):
        outs = _gmlp_f(*_gmlp_load(u_r, v_r, l_r, w_r, b_r, n_groups))
        for gi in range(n_groups):
            out[:, gi * HD:(gi + 1) * HD] = outs[gi].astype(out.dtype)

    return pl.pallas_call(
        body, name=name, grid=(t // HD,), in_specs=list(specs), out_specs=pl.BlockSpec((HD, width), lambda i: (i, 0)),
        out_shape=jax.ShapeDtypeStruct((t, width), MXU), compiler_params=_cp("parallel"),
    )(z, z, lng, ws, bt)


def gmlp_bwd(z, uv_off, width, lng, ws, bt, dob, name):
    t = z.shape[0]
    n_groups = width // HD
    specs = _gmlp_specs(width, uv_off // width, n_groups)

    def body(u_r, v_r, l_r, w_r, b_r, d_r, duv, dl, dws, dbt):
        _, vjp = jax.vjp(_gmlp_f, *_gmlp_load(u_r, v_r, l_r, w_r, b_r, n_groups))
        gu, gv, gl, gw, gb = vjp(tuple(d_r[:, gi * HD:(gi + 1) * HD] for gi in range(n_groups)))

        @pl.when(pl.program_id(0) == 0)
        def _():
            dl[...] = jnp.zeros_like(dl)
            dws[...] = jnp.zeros_like(dws)
            dbt[...] = jnp.zeros_like(dbt)

        for gi in range(n_groups):
            duv[:, gi * HD:(gi + 1) * HD] = gu[gi].astype(duv.dtype)
            duv[:, width + gi * HD:width + (gi + 1) * HD] = gv[gi].astype(duv.dtype)
            dl[:, gi * HD:(gi + 1) * HD] += gl[gi]
            dws[gi] += gw[gi]
            dbt[:, gi:gi + 1] += gb[gi]

    return pl.pallas_call(
        body, name=name, grid=(t // HD,), in_specs=list(specs) + [pl.BlockSpec((HD, width), lambda i: (i, 0))],
        out_specs=[pl.BlockSpec((HD, 2 * width), lambda i: (i, 0)), specs[2], specs[3], specs[4]],
        out_shape=[jax.ShapeDtypeStruct((t, 2 * width), MXU), jax.ShapeDtypeStruct((1, width), F32),
                   jax.ShapeDtypeStruct((n_groups, HD, HD), F32), jax.ShapeDtypeStruct((HD, LANE), F32)],
        compiler_params=_cp("arbitrary"),
    )(z, z, lng, ws, bt, dob)


def _sba_pre_f(qp, kp, qg, kg):
    return _rms(qp, qg), _rms(kp, kg)


def sba_pre_fwd(z, c_blk, n_heads, qg, kg, name):
    t = z.shape[0]
    tt = _tile(t, 512)
    zs = [pl.BlockSpec((tt, HD), functools.partial(lambda h, i, s: (i, c_blk + s * n_heads + h), s=s)) for s in range(3)]
    vec = pl.BlockSpec((1, HD), lambda h, i: (0, 0))
    head = pl.BlockSpec((None, tt, HD), lambda h, i: (h, i, 0))

    def body(qp, kp, vp, qg_r, kg_r, q_o, k_o, v_o):
        q, k = _sba_pre_f(qp[...], kp[...], qg_r[...], kg_r[...])
        q_o[...] = q.astype(q_o.dtype)
        k_o[...] = k.astype(k_o.dtype)
        v_o[...] = vp[...].astype(v_o.dtype)

    return pl.pallas_call(
        body, name=name, grid=(n_heads, t // tt), in_specs=zs + [vec, vec], out_specs=[head] * 3,
        out_shape=[jax.ShapeDtypeStruct((n_heads, t, HD), MXU)] * 3, compiler_params=_cp("parallel", "parallel"),
    )(z, z, z, qg, kg)


def sba_pre_bwd(z, c_blk, n_heads, qg, kg, dq, dk, dv, name):
    t = z.shape[0]
    tt = _tile(t, 512)
    zs = [pl.BlockSpec((tt, HD), functools.partial(lambda h, i, s: (i, c_blk + s * n_heads + h), s=s)) for s in range(2)]
    vec = pl.BlockSpec((1, HD), lambda h, i: (0, 0))
    head = pl.BlockSpec((None, tt, HD), lambda h, i: (h, i, 0))
    tok = pl.BlockSpec((tt, HD), lambda h, i: (i, h))

    def body(qp, kp, qg_r, kg_r, dq_r, dk_r, dv_r, dqp, dkp, dvp, dqg, dkg):
        _, vjp = jax.vjp(_sba_pre_f, qp[...], kp[...], qg_r[...], kg_r[...])
        gq, gk, gqg, gkg = vjp((dq_r[...], dk_r[...]))
        dqp[...] = gq.astype(dqp.dtype)
        dkp[...] = gk.astype(dkp.dtype)
        dvp[...] = dv_r[...].astype(dvp.dtype)

        @pl.when((pl.program_id(0) == 0) & (pl.program_id(1) == 0))
        def _():
            dqg[...] = jnp.zeros_like(dqg)
            dkg[...] = jnp.zeros_like(dkg)

        dqg[...] += gqg
        dkg[...] += gkg

    return pl.pallas_call(
        body, name=name, grid=(n_heads, t // tt), in_specs=zs + [vec, vec] + [head] * 3,
        out_specs=[tok] * 3 + [vec, vec],
        out_shape=[jax.ShapeDtypeStruct((t, n_heads * HD), MXU)] * 3 + [jax.ShapeDtypeStruct((1, HD), F32)] * 2,
        compiler_params=_cp("arbitrary", "arbitrary"),
    )(z, z, qg, kg, dq, dk, dv)


def _sba_block(q, kj, i, j):
    zz = lax.dot_general(q, kj, _DN["nt"], preferred_element_type=F32) * (HD ** -0.5)
    ls = _logsig(zz)
    strict = (j * HD + _iota2((HD, HD), 1)) < (i * HD + _iota2((HD, HD), 0))
    return zz, ls, jnp.where(strict, ls - zz, 0.0), strict


def sba_fwd(q, k, v, name, jobs=()):
    n_heads, t, _ = q.shape
    nb_h = _group(n_heads)

    def body(q_r, k_r, v_r, o_ref, tot_ref):
        i = pl.program_id(1)
        after = (_iota2((HD, HD), 0) > _iota2((HD, HD), 1)).astype(F32)

        def step(it, carry):
            j = i - it
            rows = pl.ds(pl.multiple_of(j * HD, HD), HD)
            new = []
            for hh in range(nb_h):
                acc, cs = carry[hh]
                _, ls, lk, strict = _sba_block(q_r[hh], k_r[hh, rows, :], i, j)
                suffix = _mm_raw(lk, after, "nn", "rb") + cs
                att = jnp.where(strict, jnp.exp(ls + suffix), 0.0)
                acc = acc + _mm_raw(att, v_r[hh, rows, :], "nn", "lo")
                new.append((acc, cs + jnp.sum(lk, axis=-1, keepdims=True)))
            return tuple(new)

        init = tuple((jnp.zeros((HD, HD), F32), jnp.zeros((HD, 1), F32)) for _ in range(nb_h))
        final = lax.fori_loop(0, i + 1, step, init)
        for hh in range(nb_h):
            o_ref[:, hh * HD:(hh + 1) * HD] = final[hh][0].astype(o_ref.dtype)
            tot_ref[hh] = final[hh][1]

    full = pl.BlockSpec((nb_h, t, HD), lambda h, i: (h, 0, 0))
    return hosted_call(
        body, (q, k, v), name=name, grid=(n_heads // nb_h, t // HD),
        in_specs=[pl.BlockSpec((nb_h, HD, HD), lambda h, i: (h, i, 0)), full, full],
        out_specs=[pl.BlockSpec((HD, nb_h * HD), lambda h, i: (i, h)), pl.BlockSpec((nb_h, HD, 1), lambda h, i: (h, i, 0))],
        out_shape=[jax.ShapeDtypeStruct((t, n_heads * HD), MXU), jax.ShapeDtypeStruct((n_heads, t, 1), F32)],
        sem=("parallel", "parallel"), jobs=jobs)


def sba_bwd(q, k, v, tot, do, name, jobs=()):
    n_heads, t, _ = q.shape
    nb_h = _group(n_heads)

    def body(q_r, k_r, v_r, tot_r, do_r, dq_ref, dk_ref, dv_ref):
        i = pl.program_id(1)

        @pl.when(i == 0)
        def _():
            dk_ref[...] = jnp.zeros_like(dk_ref)
            dv_ref[...] = jnp.zeros_like(dv_ref)

        r, s = _iota2((HD, HD), 0), _iota2((HD, HD), 1)
        upto = (r <= s).astype(F32)
        before = (r < s).astype(F32)

        def step(j, carry):
            rows = pl.ds(pl.multiple_of(j * HD, HD), HD)
            new = []
            for hh in range(nb_h):
                dq, cp, cd = carry[hh]
                qb, dob = q_r[hh], do_r[:, hh * HD:(hh + 1) * HD]
                kj, vj = k_r[hh, rows, :], v_r[hh, rows, :]
                _, ls, lk, strict = _sba_block(qb, kj, i, j)
                sig = jnp.exp(ls)
                suffix = tot_r[hh] - (cp + _mm_raw(lk, upto, "nn", "rb"))
                att = jnp.where(strict, jnp.exp(ls + suffix), 0.0)
                dp = _mm_raw(dob, vj, "nt", "lo") * att
                dlk = cd + _mm_raw(dp, before, "nn", "rb")
                dz = jnp.where(strict, dp * (1.0 - sig) - dlk * sig, 0.0) * (HD ** -0.5)
                dk_ref[hh, rows, :] += _mm_raw(dz, qb, "tn", "lo")
                dv_ref[hh, rows, :] += _mm_raw(att, dob, "tn", "lo")
                new.append((dq + _mm_raw(dz, kj, "nn", "lo"), cp + jnp.sum(lk, axis=-1, keepdims=True),
                            cd + jnp.sum(dp, axis=-1, keepdims=True)))
            return tuple(new)

        zero_col = jnp.zeros((HD, 1), F32)
        final = lax.fori_loop(0, i + 1, step, tuple((jnp.zeros((HD, HD), F32), zero_col, zero_col) for _ in range(nb_h)))
        for hh in range(nb_h):
            dq_ref[hh] = final[hh][0]

    full = pl.BlockSpec((nb_h, t, HD), lambda h, i: (h, 0, 0))
    blk = pl.BlockSpec((nb_h, HD, HD), lambda h, i: (h, i, 0))
    return hosted_call(
        body, (q, k, v, tot, do), name=name, grid=(n_heads // nb_h, t // HD),
        in_specs=[blk, full, full, pl.BlockSpec((nb_h, HD, 1), lambda h, i: (h, i, 0)),
                  pl.BlockSpec((HD, nb_h * HD), lambda h, i: (i, h))],
        out_specs=[blk, full, full], out_shape=[jax.ShapeDtypeStruct((n_heads, t, HD), F32)] * 3,
        sem=("parallel", "arbitrary"), jobs=jobs)


def small_adam(parts, w, m, v, name):
    n_parts, rows, _ = parts.shape
    tr = _tile(rows, 512) if rows % LANE == 0 else rows

    def body(p_ref, w_ref, m_ref, v_ref, g_out, d_out, m_out, v_out):
        g = p_ref[0]
        for k in range(1, n_parts):
            g = g + p_ref[k]
        delta, m2, v2 = _adam_math(w_ref[...], g, m_ref[...], v_ref[...])
        g_out[...] = g
        d_out[...] = delta
        m_out[...] = m2
        v_out[...] = v2

    blk = pl.BlockSpec((tr, LANE), lambda i: (i, 0))
    return pl.pallas_call(
        body, name=name, grid=(rows // tr,),
        in_specs=[pl.BlockSpec((n_parts, tr, LANE), lambda i: (0, i, 0)), blk, blk, blk], out_specs=[blk] * 4,
        out_shape=[jax.ShapeDtypeStruct((rows, LANE), F32)] * 4, compiler_params=_cp("parallel"),
    )(parts, w, m, v)


def _pack(arrays):
    flat = jnp.concatenate([a.reshape(-1).astype(F32) for a in arrays])
    pad = (-flat.shape[0]) % (8 * LANE)
    return jnp.pad(flat, (0, pad)).reshape(-1, LANE)


def _unpack(packed, shapes):
    flat, outs, pos = packed.reshape(-1), [], 0
    for shp in shapes:
        n = 1
        for s in shp:
            n *= s
        outs.append(flat[pos:pos + n].reshape(shp))
        pos += n
    return outs


def _pad_lanes(a):
    return jnp.pad(a, ((0, 0), (0, LANE - a.shape[1])))


def kernel(x, w_in, conv_w, a_log, dt_bias, gdn_norm_g, gmlp_ln_g, w_spatial, b_spatial, sba_q_g, sba_k_g, w_out_a, w_out_b, w_out_c, w_out, norm_mix_g, norm_mlp_g, w_ff1, w_ff2, loss_target, m_w_in, m_conv_w, m_a_log, m_dt_bias, m_gdn_norm_g, m_gmlp_ln_g, m_w_spatial, m_b_spatial, m_sba_q_g, m_sba_k_g, m_w_out_a, m_w_out_b, m_w_out_c, m_w_out, m_norm_mix_g, m_norm_mlp_g, m_w_ff1, m_w_ff2, v_w_in, v_conv_w, v_a_log, v_dt_bias, v_gdn_norm_g, v_gmlp_ln_g, v_w_spatial, v_b_spatial, v_sba_q_g, v_sba_k_g, v_w_out_a, v_w_out_b, v_w_out_c, v_w_out, v_norm_mix_g, v_norm_mlp_g, v_w_ff1, v_w_ff2):
    depth = w_in.shape[0]
    _, t, d = x.shape
    n_heads = d // 256
    gw = n_heads * HD
    width = d // 2
    n_groups = width // HD
    off_gate, off_uv, off_c, off_gl = 3 * gw, 4 * gw, 4 * gw + 2 * width, 7 * gw + 2 * width
    off_ab = off_gl + 3 * d
    n_packed = off_ab + LANE
    n_in = off_ab + 2 * n_heads
    assert w_in.shape[2] * N_DEV == n_in and t % LANE == 0 and d % 256 == 0

    ix, iy, ic = lax.axis_index("x"), lax.axis_index("y"), lax.axis_index("c")
    dev = 4 * ix + 2 * iy + ic
    c_idx = jnp.reshape(ic, (1,)).astype(jnp.int32)
    xs, target = x[0], loss_target[0]

    w_in_t, m_in_t, v_in_t = (jnp.swapaxes(a, 1, 2) for a in (w_in, m_w_in, v_w_in))
    shard_of = dict(w_in=w_in_t, w_out_a=w_out_a, w_out_b=w_out_b, w_out_c=w_out_c, w_out=w_out, w_ff1=w_ff1, w_ff2=w_ff2)
    ag_jobs = lambda l: {nm: GatherJob(w[l].astype(MXU)) for nm, w in shard_of.items()}
    layer0 = ag_jobs(0)
    conv_job = GatherJob(conv_w)
    run_jobs(list(layer0.values()) + [conv_job], "ag_layer0")
    gathered_w = [{nm: job.results[0] for nm, job in layer0.items()}]
    conv_full = jnp.transpose(conv_job.results[0], (1, 2, 0, 3)).reshape(depth, CONV_K, 3 * gw)

    per_dev = n_in // N_DEV

    def pack_in(g):
        w = g.reshape(n_in, d)
        return jnp.concatenate([w[:3 * gw], w[3 * gw + 2 * n_heads:], w[3 * gw:3 * gw + 2 * n_heads],
                                jnp.zeros((LANE - 2 * n_heads, d), w.dtype)], axis=0)

    def unpack_in(wp):
        w = jnp.concatenate([wp[:3 * gw], wp[off_ab:off_ab + 2 * n_heads], wp[3 * gw:off_ab]], axis=0)
        return w.reshape(N_DEV, per_dev, d)

    alog_p, dtb_p = _pad_lanes(a_log), _pad_lanes(dt_bias)
    bt_all = jnp.pad(jnp.transpose(b_spatial, (0, 2, 1)), ((0, 0), (0, 0), (0, LANE - n_groups)))

    saved = []
    cur = xs
    for l in range(depth):
        gl_w = gathered_w[l]
        lw = dict(w_in=pack_in(gl_w["w_in"]), conv=conv_full[l], alog=alog_p[l:l + 1], dtb=dtb_p[l:l + 1],
                  gng=gdn_norm_g[l:l + 1], lng=gmlp_ln_g[l:l + 1], ws=w_spatial[l], bt=bt_all[l],
                  qg=sba_q_g[l:l + 1], kg=sba_k_g[l:l + 1], gmix=norm_mix_g[l:l + 1], gmlp=norm_mlp_g[l:l + 1])
        nxt = ag_jobs(l + 1) if l + 1 < depth else {}
        ride = lambda *names: [nxt[nm] for nm in names if nm in nxt]
        s = dict(lw=lw, x=cur, gw=gl_w)
        s["h1"] = rms_fwd(cur, lw["gmix"], "rms_mix")
        z = s["z"] = matmul(s["h1"], lw["w_in"], "nt", F32, "mm_in", caps=(1024, 1408, 2048), jobs=ride("w_ff2"))
        s["pre"] = gdn_pre_fwd(z, off_ab // LANE, lw["conv"], lw["alog"], lw["dtb"], n_heads, "gdn_pre")
        s["chunks"] = gdn_chunk_fwd(*s["pre"], "gdn_chunk", jobs=ride("w_in"))
        s["o"], s["states"] = gdn_scan_fwd(s["chunks"], "gdn_scan")
        s["oa"] = gdn_post_fwd(s["o"], z, off_gate // HD, lw["gng"], "gdn_post")
        s["ob"] = gmlp_fwd(z, off_uv, width, lw["lng"], lw["ws"], lw["bt"], "gmlp")
        s["qkv_c"] = sba_pre_fwd(z, off_c // HD, n_heads, lw["qg"], lw["kg"], "sba_pre")
        s["oc"], s["tot"] = sba_fwd(*s["qkv_c"], "sba", jobs=ride("w_ff1"))
        s["ba"] = matmul(s["oa"], gl_w["w_out_a"], "nn", F32, "mm_oa", b_view="cols")
        s["bb"] = matmul(s["ob"], gl_w["w_out_b"], "nn", F32, "mm_ob", b_view="cols")
        s["bc"] = matmul(s["oc"], gl_w["w_out_c"], "nn", F32, "mm_oc", b_view="cols")
        s["y"] = merge_fwd(z, off_gl, s["ba"], s["bb"], s["bc"], "merge")
        s["x1"] = matmul(s["y"], gl_w["w_out"], "nn", F32, "mm_out", res=cur, b_view="rows")
        s["h2"] = rms_fwd(s["x1"], lw["gmlp"], "rms_mlp")
        s["f"], s["a"] = matmul(s["h2"], gl_w["w_ff1"], "nn", F32, "mm_ff1", b_view="cols", post="sqrelu",
                                jobs=ride("w_out", "w_out_a"))
        cur = matmul(s["a"], gl_w["w_ff2"], "nn", F32, "mm_ff2", res=s["x1"], b_view="rows",
                     jobs=ride("w_out_b", "w_out_c"))
        if nxt:
            gathered_w.append({nm: job.results[0] for nm, job in nxt.items()})
        saved.append(s)

    dx, dxb, loss_tile = loss_head(cur, target, "loss_head")
    loss = lax.psum(loss_tile[0, 0], AXES)

    big = dict(w_in=(w_in_t, m_in_t, v_in_t), w_out_a=(w_out_a, m_w_out_a, v_w_out_a), w_out_b=(w_out_b, m_w_out_b, v_w_out_b),
               w_out_c=(w_out_c, m_w_out_c, v_w_out_c), w_out=(w_out, m_w_out, v_w_out), w_ff1=(w_ff1, m_w_ff1, v_w_ff1),
               w_ff2=(w_ff2, m_w_ff2, v_w_ff2))
    bufs = {nm: tuple(lax.empty(w.shape, F32) for _ in range(4)) for nm, (w, _, _) in big.items()}
    waiting = []

    def pair_reduce(nm, g8, l):
        w = big[nm][0]
        g8 = g8.reshape(N_DEV, w.shape[1], w.shape[2])
        r4 = rs_pair_exchange(g8, "rs_pair_" + nm)
        waiting.append((nm, l, ChipExchangeJob(pair_sum(g8, r4, c_idx, "rs_pair_sum_" + nm))))

    def take(*names):
        picked = [e for e in waiting if e[0] in names]
        for e in picked:
            waiting.remove(e)
        return picked

    def update(picked):
        for nm, l, job in picked:
            w, m, v = big[nm]
            bufs[nm] = tuple(adam_layer(job.results[0], w, m, v, bufs[nm], l, "adam_" + nm))

    small_grads = []
    for l in reversed(range(depth)):
        s = saved[l]
        lw, z, gl_w = s["lw"], s["z"], s["gw"]
        df = matmul(dxb, gl_w["w_ff2"], "nt", MXU, "mm_ff2_dx", b_view="rows", res=s["f"], post="sqrelu_bwd")
        pair_reduce("w_ff2", matmul(s["a"], dxb, "tn", MXU, "mm_ff2_dw"), l)
        dh2 = matmul(df, gl_w["w_ff1"], "nt", F32, "mm_ff1_dx", b_view="cols")
        pair_reduce("w_ff1", matmul(s["h2"], df, "tn", MXU, "mm_ff1_dw", out_slabs=w_ff1.shape[2]), l)
        dx1, dx1b, d_gmlp = rms_bwd(s["x1"], lw["gmlp"], dh2, dx, "rms_mlp_bwd")
        dy = matmul(dx1b, gl_w["w_out"], "nt", F32, "mm_out_dx", b_view="rows")
        pair_reduce("w_out", matmul(s["y"], dx1b, "tn", MXU, "mm_out_dw"), l)
        dgl, dba, dbb, dbc = merge_bwd(z, off_gl, s["ba"], s["bb"], s["bc"], dy, "merge_bwd")
        doa = matmul(dba, gl_w["w_out_a"], "nt", F32, "mm_oa_dx", b_view="cols")
        dob = matmul(dbb, gl_w["w_out_b"], "nt", F32, "mm_ob_dx", b_view="cols")
        doc = matmul(dbc, gl_w["w_out_c"], "nt", F32, "mm_oc_dx", b_view="cols")
        slab = w_out_a.shape[2]
        pair_reduce("w_out_a", matmul(s["oa"], dba, "tn", MXU, "mm_oa_dw", out_slabs=slab), l)
        pair_reduce("w_out_b", matmul(s["ob"], dbb, "tn", MXU, "mm_ob_dw", out_slabs=slab), l)
        pair_reduce("w_out_c", matmul(s["oc"], dbc, "tn", MXU, "mm_oc_dw", out_slabs=slab), l)
        riding = take("w_ff2", "w_ff1", "w_out")
        dqc, dkc, dvc = sba_bwd(*s["qkv_c"], s["tot"], doc, "sba_bwd", jobs=[e[2] for e in riding])
        update(riding)
        dz_qc, dz_kc, dz_vc, d_qg, d_kg = sba_pre_bwd(z, off_c // HD, n_heads, lw["qg"], lw["kg"], dqc, dkc, dvc, "sba_pre_bwd")
        dz_uv, d_lng, d_ws, d_bt = gmlp_bwd(z, off_uv, width, lw["lng"], lw["ws"], lw["bt"], dob, "gmlp_bwd")
        do, dz_gate, d_gng = gdn_post_bwd(s["o"], z, off_gate // HD, lw["gng"], doa, "gdn_post_bwd")
        chunk_cts = gdn_scan_bwd(s["chunks"], s["states"], do, "gdn_scan_bwd")
        riding = take("w_in", "w_out_a", "w_out_b", "w_out_c")
        dqa, dka, dva, dga, dba_ = gdn_chunk_bwd(*s["pre"], chunk_cts, "gdn_chunk_bwd", jobs=[e[2] for e in riding])
        update(riding)
        dz_q, dz_k, dz_v, d_ab, d_cq, d_ck, d_cv, d_alog, d_dtb = gdn_pre_bwd(
            z, off_ab // LANE, lw["conv"], lw["alog"], lw["dtb"], n_heads, dqa, dka, dva, dga, dba_, "gdn_pre_bwd")
        dz = jnp.concatenate([dz_q, dz_k, dz_v, dz_gate, dz_uv, dz_qc, dz_kc, dz_vc, dgl[0], dgl[1], dgl[2],
                              d_ab.astype(MXU)], axis=1)
        dh1 = matmul(dz, lw["w_in"], "nn", F32, "mm_in_dx", caps=(1024, 1024, 1408))
        pair_reduce("w_in", unpack_in(matmul(dz, s["h1"], "tn", MXU, "mm_in_dw", caps=(1408, 1024, 2048))), l)
        dx, dxb, d_gmix = rms_bwd(s["x"], lw["gmix"], dh1, dx1, "rms_mix_bwd")
        small_grads.append(dict(
            conv_w=jnp.concatenate([d_cq, d_ck, d_cv], axis=1), a_log=d_alog[0, :n_heads], dt_bias=d_dtb[0, :n_heads],
            gdn_norm_g=d_gng[0], gmlp_ln_g=d_lng[0], w_spatial=d_ws, b_spatial=jnp.transpose(d_bt[:, :n_groups]),
            sba_q_g=d_qg[0], sba_k_g=d_kg[0], norm_mix_g=d_gmix[0], norm_mlp_g=d_gmlp[0]))
    small_grads = small_grads[::-1]
    rest = take("w_in")
    run_jobs([e[2] for e in rest], "rs_chip_last")
    update(rest)

    rep_names = ["a_log", "dt_bias", "gdn_norm_g", "gmlp_ln_g", "w_spatial", "b_spatial", "sba_q_g", "sba_k_g",
                 "norm_mix_g", "norm_mlp_g"]
    rep = dict(a_log=(a_log, m_a_log, v_a_log), dt_bias=(dt_bias, m_dt_bias, v_dt_bias),
               gdn_norm_g=(gdn_norm_g, m_gdn_norm_g, v_gdn_norm_g), gmlp_ln_g=(gmlp_ln_g, m_gmlp_ln_g, v_gmlp_ln_g),
               w_spatial=(w_spatial, m_w_spatial, v_w_spatial), b_spatial=(b_spatial, m_b_spatial, v_b_spatial),
               sba_q_g=(sba_q_g, m_sba_q_g, v_sba_q_g), sba_k_g=(sba_k_g, m_sba_k_g, v_sba_k_g),
               norm_mix_g=(norm_mix_g, m_norm_mix_g, v_norm_mix_g), norm_mlp_g=(norm_mlp_g, m_norm_mlp_g, v_norm_mlp_g))
    stack = lambda nm: jnp.stack([sg[nm] for sg in small_grads])
    conv_cols = conv_w.shape[2]
    conv_pad = jnp.zeros((depth, CONV_K, 3 * gw - conv_cols), F32)
    widen = lambda a: jnp.concatenate([a, conv_pad], axis=2)
    grads_packed = _pack([stack(nm) for nm in rep_names] + [stack("conv_w")])
    small_job = GatherJob(grads_packed)
    run_jobs([small_job], "ag_small_grads")
    gathered = small_job.results[0]
    packed = [_pack([rep[nm][k] for nm in rep_names] + [widen((conv_w, m_conv_w, v_conv_w)[k])]) for k in range(3)]
    shapes = [rep[nm][0].shape for nm in rep_names] + [(depth, CONV_K, 3 * gw)]
    flat = gathered.reshape(N_DEV, -1)
    n_rep = sum(int(rep[nm][0].size) for nm in rep_names)
    conv_part = flat[:, n_rep:n_rep + depth * CONV_K * 3 * gw].reshape(N_DEV, depth, CONV_K, 3 * gw)
    conv_mine = lax.dynamic_slice_in_dim(conv_part, dev * conv_cols, conv_cols, axis=3)
    conv_mine = jnp.concatenate([conv_mine, jnp.zeros((N_DEV, depth, CONV_K, 3 * gw - conv_cols), F32)], axis=3)
    tail = flat[:, n_rep + depth * CONV_K * 3 * gw:]
    parts = jnp.concatenate([flat[:, :n_rep], conv_mine.reshape(N_DEV, -1), tail], axis=1).reshape(gathered.shape)
    outs_small = small_adam(parts, packed[0], packed[1], packed[2], "adam_small")
    small = [dict(zip(rep_names + ["conv_w"], _unpack(o, shapes))) for o in outs_small]
    for sm in small:
        sm["conv_w"] = sm["conv_w"][:, :, :conv_cols]

    order = ["w_in", "conv_w", "a_log", "dt_bias", "gdn_norm_g", "gmlp_ln_g", "w_spatial", "b_spatial", "sba_q_g",
             "sba_k_g", "w_out_a", "w_out_b", "w_out_c", "w_out", "norm_mix_g", "norm_mlp_g", "w_ff1", "w_ff2"]
    result = [loss, dx[None]]
    for kind in range(4):
        for nm in order:
            if nm == "w_in":
                result.append(jnp.swapaxes(bufs[nm][kind], 1, 2))
            else:
                result.append(bufs[nm][kind] if nm in bufs else small[kind][nm])
    return tuple(result)
```

```python
import functools

import jax
import jax.numpy as jnp
from jax import lax
from jax.experimental import pallas as pl
from jax.experimental.pallas import tpu as pltpu

F32 = jnp.float32
MXU = jnp.bfloat16
N_DEV = 8
AXES = ("x", "y", "c")
CHUNK = 64
HD = 128
CONV_K = 4
EPS = 1e-6
LANE = 128
VMEM_LIMIT = 56 * 1024 * 1024
ADAM_LR, ADAM_B1, ADAM_B2, ADAM_EPS, ADAM_WD, ADAM_STEP = 0.001, 0.9, 0.999, 1e-08, 0.01, 10

_ANY = pl.BlockSpec(memory_space=pl.ANY)
_MESH = pl.DeviceIdType.MESH
_DN = {"nn": (((1,), (0,)), ((), ())), "nt": (((1,), (1,)), ((), ())), "tn": (((0,), (0,)), ((), ()))}


def _cp(*sem):
    return pltpu.CompilerParams(dimension_semantics=sem, vmem_limit_bytes=VMEM_LIMIT)


def _tile(n, cap):
    if n <= cap:
        return n
    best = LANE
    for t in range(LANE, cap + 1, LANE):
        if n % t == 0:
            best = t
    assert n % best == 0, (n, cap)
    return best


def _split(x, n):
    parts, rest = [], x.astype(F32)
    for _ in range(n):
        p = rest.astype(MXU)
        parts.append(p)
        rest = rest - p.astype(F32)
    return parts


def _mm_raw(a, b, mode, prec):
    dot = lambda p, q: lax.dot_general(p, q, _DN[mode], preferred_element_type=F32)
    if prec == "lo" or MXU == F32:
        return dot(a.astype(MXU), b.astype(MXU))
    if prec == "x3":
        (ah, al), (bh, bl) = _split(a, 2), _split(b, 2)
        return dot(ah, bh) + (dot(al, bh) + dot(ah, bl))
    if prec == "la":
        ae, (bh, bl) = a.astype(MXU), _split(b, 2)
        return dot(ae, bh) + dot(ae, bl)
    assert prec == "rb"
    (ah, al), be = _split(a, 2), b.astype(MXU)
    return dot(ah, be) + dot(al, be)


@functools.partial(jax.custom_vjp, nondiff_argnums=(2, 3))
def mm(a, b, mode, prec):
    return _mm_raw(a, b, mode, prec)


def _mm_fwd(a, b, mode, prec):
    return _mm_raw(a, b, mode, prec), (a, b)


def _mm_bwd(mode, prec, res, ct):
    a, b = res
    pa = {"la": None, "rb": "rb"}.get(prec, prec)
    pb = {"la": "la", "rb": None}.get(prec, prec)
    if mode == "nn":
        da = _mm_raw(ct, b, "nt", pa) if pa else None
        db = _mm_raw(a, ct, "tn", pb) if pb else None
    elif mode == "nt":
        da = _mm_raw(ct, b, "nn", pa) if pa else None
        db = _mm_raw(ct, a, "tn", {"la": "rb"}.get(pb, pb)) if pb else None
    else:
        da = _mm_raw(b, ct, "nt", {"rb": "la"}.get(pa, pa)) if pa else None
        db = _mm_raw(a, ct, "nn", pb) if pb else None
    da = jnp.zeros_like(a) if da is None else da.astype(a.dtype)
    db = jnp.zeros_like(b) if db is None else db.astype(b.dtype)
    return da, db


mm.defvjp(_mm_fwd, _mm_bwd)


def _shift_rows(x, j):
    n = x.shape[0]
    row = lax.broadcasted_iota(jnp.int32, x.shape, 0)
    if j > 0:
        return jnp.where(row >= j, pltpu.roll(x, j, 0), 0.0)
    return jnp.where(row < n + j, pltpu.roll(x, n + j, 0), 0.0)


@functools.partial(jax.custom_vjp, nondiff_argnums=(1,))
def shift(x, j):
    return _shift_rows(x, j)


shift.defvjp(lambda x, j: (_shift_rows(x, j), None), lambda j, _, ct: (_shift_rows(ct, -j),))


def _sigmoid(x):
    return 1.0 / (1.0 + jnp.exp(-x))


def _silu(x):
    return x * _sigmoid(x)


def _softplus(x):
    return jnp.maximum(x, 0.0) + jnp.log(1.0 + jnp.exp(-jnp.abs(x)))


def _logsig(x):
    return jnp.minimum(x, 0.0) - jnp.log(1.0 + jnp.exp(-jnp.abs(x)))


def _gelu(x):
    return 0.5 * x * (1.0 + lax.erf(x * (2.0 ** -0.5)))


def _rms(x, g):
    return x * lax.rsqrt(jnp.mean(x * x, axis=-1, keepdims=True) + EPS) * g


def _iota2(shape, dim):
    return lax.broadcasted_iota(jnp.int32, shape, dim)


class GatherJob:
    def __init__(self, x):
        self.inputs = [x]
        self.out_shapes = [jax.ShapeDtypeStruct((N_DEV,) + x.shape, x.dtype)]
        self.sems = [pltpu.SemaphoreType.DMA((7,)), pltpu.SemaphoreType.DMA((7,)), pltpu.SemaphoreType.DMA(())]
        self.results = None

    @staticmethod
    def _plan(ins, outs, sems):
        (x_ref,), (out_ref,), (send_sems, recv_sems, local_sem) = ins, outs, sems
        ix, iy, ic = lax.axis_index("x"), lax.axis_index("y"), lax.axis_index("c")
        me, sibling = (ix, iy, ic), (ix, iy, 1 - ic)
        chips = [(1 - ix, iy), (ix, 1 - iy), (1 - ix, 1 - iy)]

        def slot(px, py, pc):
            return out_ref.at[4 * px + 2 * py + pc]

        def copy(k, block, to, src=None):
            return pltpu.make_async_remote_copy(
                src_ref=slot(*block) if src is None else src, dst_ref=slot(*block),
                send_sem=send_sems.at[k], recv_sem=recv_sems.at[k], device_id=to, device_id_type=_MESH)

        mine = pltpu.make_async_copy(x_ref, slot(*me), local_sem)
        first = [copy(0, me, sibling, src=x_ref)]
        first += [copy(1 + j, me, (*chip, ic), src=x_ref) for j, chip in enumerate(chips)]
        return ic, me, sibling, chips, copy, mine, first

    def start(self, ins, outs, sems):
        *_, mine, first = self._plan(ins, outs, sems)
        mine.start()
        for cp in first:
            cp.start()

    def finish(self, ins, outs, sems):
        ic, me, sibling, chips, copy, mine, first = self._plan(ins, outs, sems)
        passed = [copy(4 + j, (*chip, ic), sibling) for j, chip in enumerate(chips)]
        for j, chip in enumerate(chips):
            copy(1 + j, (*chip, ic), me).wait_recv()
            passed[j].start()
        copy(0, sibling, me).wait_recv()
        for j, chip in enumerate(chips):
            copy(4 + j, (*chip, 1 - ic), me).wait_recv()
        for cp in first + passed:
            cp.wait_send()
        mine.wait()


class ChipExchangeJob:
    def __init__(self, p4):
        self.inputs = [p4]
        self.out_shapes = [jax.ShapeDtypeStruct(p4.shape, p4.dtype)]
        self.sems = [pltpu.SemaphoreType.DMA((3,)), pltpu.SemaphoreType.DMA((3,)), pltpu.SemaphoreType.DMA(())]
        self.results = None

    @staticmethod
    def _plan(ins, outs, sems):
        (p_ref,), (r_ref,), (send_sems, recv_sems, local_sem) = ins, outs, sems
        ix, iy, ic = lax.axis_index("x"), lax.axis_index("y"), lax.axis_index("c")
        my_xy = 2 * ix + iy
        local = pltpu.make_async_copy(p_ref.at[my_xy], r_ref.at[my_xy], local_sem)
        chips = [(1 - ix, iy), (ix, 1 - iy), (1 - ix, 1 - iy)]
        copies = [
            pltpu.make_async_remote_copy(
                src_ref=p_ref.at[2 * px + py], dst_ref=r_ref.at[my_xy],
                send_sem=send_sems.at[k], recv_sem=recv_sems.at[k],
                device_id=(px, py, ic), device_id_type=_MESH)
            for k, (px, py) in enumerate(chips)
        ]
        return local, copies

    def start(self, ins, outs, sems):
        local, copies = self._plan(ins, outs, sems)
        local.start()
        for cp in copies:
            cp.start()

    def finish(self, ins, outs, sems):
        local, copies = self._plan(ins, outs, sems)
        for cp in copies:
            cp.wait()
        local.wait()


class PairExchangeJob:
    def __init__(self, g8):
        self.inputs = [g8]
        self.out_shapes = [jax.ShapeDtypeStruct((4,) + g8.shape[1:], g8.dtype)]
        self.sems = [pltpu.SemaphoreType.DMA((4,)), pltpu.SemaphoreType.DMA((4,))]
        self.results = None

    @staticmethod
    def _plan(ins, outs, sems):
        (g_ref,), (r_ref,), (send_sems, recv_sems) = ins, outs, sems
        ix, iy, ic = lax.axis_index("x"), lax.axis_index("y"), lax.axis_index("c")
        return [
            pltpu.make_async_remote_copy(
                src_ref=g_ref.at[2 * xy + (1 - ic)], dst_ref=r_ref.at[xy],
                send_sem=send_sems.at[xy], recv_sem=recv_sems.at[xy],
                device_id=(ix, iy, 1 - ic), device_id_type=_MESH)
            for xy in range(4)
        ]

    def start(self, ins, outs, sems):
        for cp in self._plan(ins, outs, sems):
            cp.start()

    def finish(self, ins, outs, sems):
        for cp in self._plan(ins, outs, sems):
            cp.wait()


def _each_job(jobs, method, ins, outs, sems):
    i = o = s = 0
    for job in jobs:
        ni, no, ns = len(job.inputs), len(job.out_shapes), len(job.sems)
        getattr(job, method)(ins[i:i + ni], outs[o:o + no], sems[s:s + ns])
        i, o, s = i + ni, o + no, s + ns


def run_jobs(jobs, name):
    j_in = [a for job in jobs for a in job.inputs]
    j_out = [sh for job in jobs for sh in job.out_shapes]
    j_sem = [sm for job in jobs for sm in job.sems]

    def body(*refs):
        ins, outs, sems = refs[:len(j_in)], refs[len(j_in):len(j_in) + len(j_out)], refs[len(j_in) + len(j_out):]
        _each_job(jobs, "start", ins, outs, sems)
        _each_job(jobs, "finish", ins, outs, sems)

    res = pl.pallas_call(body, name=name, out_shape=j_out, in_specs=[_ANY] * len(j_in), out_specs=[_ANY] * len(j_out),
                         scratch_shapes=j_sem)(*j_in)
    _hand_out(jobs, res)


def _hand_out(jobs, res):
    o = 0
    for job in jobs:
        job.results = list(res[o:o + len(job.out_shapes)])
        o += len(job.out_shapes)


def hosted_call(body, args, *, name, grid, in_specs, out_specs, out_shape, scratch_shapes=(), sem=None, jobs=()):
    outs_l, specs_l = list(out_shape), list(out_specs)
    if not jobs:
        return pl.pallas_call(body, name=name, grid=grid, in_specs=list(in_specs), out_specs=specs_l, out_shape=outs_l,
                              scratch_shapes=list(scratch_shapes), compiler_params=_cp(*sem))(*args)
    j_in = [a for job in jobs for a in job.inputs]
    j_out = [sh for job in jobs for sh in job.out_shapes]
    j_sem = [sm for job in jobs for sm in job.sems]
    n_in, n_out, n_scr = len(in_specs), len(outs_l), len(scratch_shapes)

    def wrapped(*refs):
        pos = [0]

        def take(n):
            pos[0] += n
            return refs[pos[0] - n:pos[0]]

        ins, jin, outs, jout, scr, jsem = take(n_in), take(len(j_in)), take(n_out), take(len(j_out)), take(n_scr), take(len(j_sem))
        ids = [pl.program_id(a) for a in range(len(grid))]
        first = functools.reduce(lambda p, q: p & q, [i == 0 for i in ids])
        last = functools.reduce(lambda p, q: p & q, [i == g - 1 for i, g in zip(ids, grid)])

        @pl.when(first)
        def _():
            _each_job(jobs, "start", jin, jout, jsem)

        body(*ins, *outs, *scr)

        @pl.when(last)
        def _():
            _each_job(jobs, "finish", jin, jout, jsem)

    res = pl.pallas_call(
        wrapped, name=name, grid=grid, in_specs=list(in_specs) + [_ANY] * len(j_in),
        out_specs=specs_l + [_ANY] * len(j_out), out_shape=outs_l + j_out,
        scratch_shapes=list(scratch_shapes) + j_sem, compiler_params=_cp(*["arbitrary"] * len(grid)),
    )(*args, *j_in)
    _hand_out(jobs, res[n_out:])
    return list(res[:n_out])


def pair_sum(g8, r4, c_idx, name):
    _, rows, cols = g8.shape
    tr, tc = _tile_2d(rows, cols)

    def body(c_ref, g_ref, r_ref, o_ref):
        o_ref[...] = (g_ref[...].astype(F32) + r_ref[...].astype(F32)).astype(o_ref.dtype)

    grid_spec = pltpu.PrefetchScalarGridSpec(
        num_scalar_prefetch=1, grid=(4, rows // tr, cols // tc),
        in_specs=[pl.BlockSpec((None, tr, tc), lambda s, i, j, c: (2 * s + c[0], i, j)),
                  pl.BlockSpec((None, tr, tc), lambda s, i, j, c: (s, i, j))],
        out_specs=pl.BlockSpec((None, tr, tc), lambda s, i, j, c: (s, i, j)))
    return pl.pallas_call(
        body, name=name, grid_spec=grid_spec, out_shape=jax.ShapeDtypeStruct((4, rows, cols), g8.dtype),
        compiler_params=_cp("parallel", "parallel", "parallel"),
    )(c_idx, g8, r4)


def _tile_2d(rows, cols):
    budget = 128 * 2048
    tr, tc = rows, cols
    if rows % 16 == 0:
        while tr * cols > budget and tr % 32 == 0:
            tr //= 2
    else:
        while rows * tc > budget and tc % (2 * LANE) == 0:
            tc //= 2
    return tr, tc


def _adam_math(w, g, m, v):
    m2 = ADAM_B1 * m + (1.0 - ADAM_B1) * g
    v2 = ADAM_B2 * v + (1.0 - ADAM_B2) * (g * g)
    m_hat = m2 / (1.0 - ADAM_B1 ** ADAM_STEP)
    v_hat = v2 / (1.0 - ADAM_B2 ** ADAM_STEP)
    delta = -ADAM_LR * (m_hat / (jnp.sqrt(v_hat) + ADAM_EPS) + ADAM_WD * w)
    return delta, m2, v2


def adam_layer(parts, w, m, v, bufs, layer, name):
    n_parts, rows, cols = parts.shape
    tr, tc = _tile_2d(rows, cols)

    def body(p_ref, w_ref, m_ref, v_ref, g_in, d_in, m_in, v_in, g_out, d_out, m_out, v_out):
        g = p_ref[0].astype(F32)
        for k in range(1, n_parts):
            g = g + p_ref[k].astype(F32)
        delta, m2, v2 = _adam_math(w_ref[...], g, m_ref[...], v_ref[...])
        g_out[...] = g
        d_out[...] = delta
        m_out[...] = m2
        v_out[...] = v2

    lay = pl.BlockSpec((None, tr, tc), lambda i, j: (layer, i, j))
    return pl.pallas_call(
        body, name=name, grid=(rows // tr, cols // tc),
        in_specs=[pl.BlockSpec((n_parts, tr, tc), lambda i, j: (0, i, j)), lay, lay, lay, _ANY, _ANY, _ANY, _ANY],
        out_specs=[lay, lay, lay, lay],
        out_shape=[jax.ShapeDtypeStruct(w.shape, F32)] * 4,
        input_output_aliases={4: 0, 5: 1, 6: 2, 7: 3},
        compiler_params=_cp("parallel", "parallel"),
    )(parts, w, m, v, *bufs)


def matmul(a, b, mode, out_dtype, name, res=None, caps=(1024, 1024, 2048), b_view=None, out_slabs=None, jobs=(), post=None):
    if mode == "tn":
        k_dim, m_dim = a.shape
    else:
        m_dim, k_dim = a.shape
    if b_view is None:
        b_rows, b_cols = b.shape
    else:
        kind = b_view
        shard_r, shard_c = b.shape[1:]
        b_rows, b_cols = (shard_r, N_DEV * shard_c) if kind == "cols" else (N_DEV * shard_r, shard_c)
    n_dim = b_rows if mode == "nt" else b_cols
    cap_n, cap_k = caps[1], caps[2]
    tn = _tile(out_slabs, cap_n) if out_slabs else _tile(n_dim, cap_n)
    tk = _tile(k_dim, cap_k)
    stacked = 0
    if b_view is not None:
        along_n = (kind == "cols") == (mode == "nn")
        shard_len = shard_c if kind == "cols" else shard_r
        if along_n:
            tn = _tile(shard_len, cap_n)
        elif kind == "rows" and tk >= shard_r and shard_r % 16 == 0:
            stacked = tk // shard_r
        else:
            tk = _tile(shard_len, cap_k)
    tm = _tile(m_dim, caps[0])
    nk = k_dim // tk

    def body(*refs):
        a_ref, b_ref = refs[:2]
        r_ref = refs[2] if res is not None else None
        o_idx = 3 if res is not None else 2
        o_ref = refs[o_idx]

        def emit(r):
            if post == "sqrelu_bwd":
                r = r * (2.0 * jnp.maximum(r_ref[...], 0.0))
            elif res is not None:
                r = r + r_ref[...]
            o_ref[...] = r.astype(out_dtype)
            if post == "sqrelu":
                act = jnp.maximum(r, 0.0)
                refs[o_idx + 1][...] = (act * act).astype(MXU)

        b_val = b_ref[...].reshape(tk, tn) if stacked else b_ref[...]
        part = lax.dot_general(a_ref[...], b_val, _DN[mode], preferred_element_type=F32)
        if nk == 1:
            emit(part)
            return
        acc = refs[-1]
        k = pl.program_id(2)

        @pl.when(k == 0)
        def _():
            acc[...] = part

        @pl.when(k > 0)
        def _():
            acc[...] += part

        @pl.when(k == nk - 1)
        def _():
            emit(acc[...])

    a_spec = pl.BlockSpec((tk, tm), lambda i, j, k: (k, i)) if mode == "tn" else pl.BlockSpec((tm, tk), lambda i, j, k: (i, k))
    b_blk = (tn, tk) if mode == "nt" else (tk, tn)
    b_pos = (lambda i, j, k: (j, k)) if mode == "nt" else (lambda i, j, k: (k, j))
    if b_view is None:
        b_spec = pl.BlockSpec(b_blk, b_pos)
    elif stacked:
        b_spec = pl.BlockSpec((stacked, shard_r, tn), lambda i, j, k: (k, 0, j))
    elif kind == "cols":
        per = shard_c // b_blk[1]
        b_spec = pl.BlockSpec((None,) + b_blk, lambda i, j, k: (b_pos(i, j, k)[1] // per, b_pos(i, j, k)[0], b_pos(i, j, k)[1] % per))
    else:
        per = shard_r // b_blk[0]
        b_spec = pl.BlockSpec((None,) + b_blk, lambda i, j, k: (b_pos(i, j, k)[0] // per, b_pos(i, j, k)[0] % per, b_pos(i, j, k)[1]))
    if out_slabs:
        per_o = out_slabs // tn
        o_spec = pl.BlockSpec((None, tm, tn), lambda i, j, k: (j // per_o, i, j % per_o))
        out_shape = jax.ShapeDtypeStruct((n_dim // out_slabs, m_dim, out_slabs), out_dtype)
    else:
        o_spec = pl.BlockSpec((tm, tn), lambda i, j, k: (i, j))
        out_shape = jax.ShapeDtypeStruct((m_dim, n_dim), out_dtype)
    in_specs, args = [a_spec, b_spec], [a, b]
    if res is not None:
        in_specs.append(o_spec)
        args.append(res)
    out_specs, out_shapes = [o_spec], [out_shape]
    if post == "sqrelu":
        out_specs.append(o_spec)
        out_shapes.append(jax.ShapeDtypeStruct(out_shape.shape, MXU))
    outs = hosted_call(
        body, args, name=name, grid=(m_dim // tm, n_dim // tn, nk), in_specs=in_specs, out_specs=out_specs,
        out_shape=out_shapes, scratch_shapes=[pltpu.VMEM((tm, tn), F32)] if nk > 1 else [],
        sem=("parallel", "parallel", "arbitrary"), jobs=jobs)
    return tuple(outs) if post == "sqrelu" else outs[0]


def rms_fwd(x, gain, name):
    t, d = x.shape
    tt = _tile(t, 256)

    def body(x_ref, g_ref, o_ref):
        o_ref[...] = _rms(x_ref[...], g_ref[...]).astype(o_ref.dtype)

    return pl.pallas_call(
        body, name=name, grid=(t // tt,),
        in_specs=[pl.BlockSpec((tt, d), lambda i: (i, 0)), pl.BlockSpec((1, d), lambda i: (0, 0))],
        out_specs=pl.BlockSpec((tt, d), lambda i: (i, 0)),
        out_shape=jax.ShapeDtypeStruct((t, d), MXU), compiler_params=_cp("parallel"),
    )(x, gain)


def rms_bwd(x, gain, dh, dres, name):
    t, d = x.shape
    tt = _tile(t, 256)

    def body(x_ref, g_ref, dh_ref, dr_ref, dx_ref, dxb_ref, dg_ref):
        _, vjp = jax.vjp(_rms, x_ref[...], g_ref[...])
        dx, dg = vjp(dh_ref[...])
        dx = dx + dr_ref[...]
        dx_ref[...] = dx
        dxb_ref[...] = dx.astype(dxb_ref.dtype)

        @pl.when(pl.program_id(0) == 0)
        def _():
            dg_ref[...] = jnp.zeros_like(dg_ref)

        dg_ref[...] += dg

    row = pl.BlockSpec((tt, d), lambda i: (i, 0))
    vec = pl.BlockSpec((1, d), lambda i: (0, 0))
    return pl.pallas_call(
        body, name=name, grid=(t // tt,), in_specs=[row, vec, row, row], out_specs=[row, row, vec],
        out_shape=[jax.ShapeDtypeStruct((t, d), F32), jax.ShapeDtypeStruct((t, d), MXU), jax.ShapeDtypeStruct((1, d), F32)],
        compiler_params=_cp("arbitrary"),
    )(x, gain, dh, dres)


def loss_head(x, target, name):
    t, d = x.shape
    tt = _tile(t, 256)

    def body(x_ref, t_ref, dx_ref, dxb_ref, l_ref):
        e = x_ref[...] - t_ref[...]
        dx = e * (1.0 / d)
        dx_ref[...] = dx
        dxb_ref[...] = dx.astype(dxb_ref.dtype)

        @pl.when(pl.program_id(0) == 0)
        def _():
            l_ref[...] = jnp.zeros_like(l_ref)

        part = jnp.sum(jnp.sum(e * e, axis=-1, keepdims=True) * (1.0 / d), axis=0, keepdims=True)
        l_ref[...] += 0.5 * part

    row = pl.BlockSpec((tt, d), lambda i: (i, 0))
    return pl.pallas_call(
        body, name=name, grid=(t // tt,), in_specs=[row, row],
        out_specs=[row, row, pl.BlockSpec((8, LANE), lambda i: (0, 0))],
        out_shape=[jax.ShapeDtypeStruct((t, d), F32), jax.ShapeDtypeStruct((t, d), MXU), jax.ShapeDtypeStruct((8, LANE), F32)],
        compiler_params=_cp("arbitrary"),
    )(x, target)


def _merge_f(g0, g1, g2, ba, bb, bc):
    return _sigmoid(g0) * ba + _sigmoid(g1) * bb + _sigmoid(g2) * bc


def merge_fwd(z, off, ba, bb, bc, name):
    t, d = ba.shape
    tt, td = _tile(t, 256), _tile(d // 2, 1024)
    nd, ob = d // td, off // td

    def body(g0, g1, g2, a, b, c, o_ref):
        o_ref[...] = _merge_f(g0[...], g1[...], g2[...], a[...], b[...], c[...]).astype(o_ref.dtype)

    gates = [pl.BlockSpec((tt, td), functools.partial(lambda i, j, s: (i, ob + s * nd + j), s=s)) for s in range(3)]
    blk = pl.BlockSpec((tt, td), lambda i, j: (i, j))
    return pl.pallas_call(body, name=name, grid=(t // tt, nd), in_specs=gates + [blk] * 3, out_specs=blk,
                          out_shape=jax.ShapeDtypeStruct((t, d), MXU), compiler_params=_cp("parallel", "parallel"))(z, z, z, ba, bb, bc)


def merge_bwd(z, off, ba, bb, bc, dy, name):
    t, d = ba.shape
    tt, td = _tile(t, 256), _tile(d // 2, 1024)
    nd, ob = d // td, off // td

    def body(g0, g1, g2, a, b, c, dy_ref, dgl, da, db, dc):
        _, vjp = jax.vjp(_merge_f, g0[...], g1[...], g2[...], a[...], b[...], c[...])
        d0, d1, d2, xa, xb, xc = vjp(dy_ref[...])
        for s, dv in enumerate((d0, d1, d2)):
            dgl[s] = dv.astype(dgl.dtype)
        da[...] = xa.astype(da.dtype)
        db[...] = xb.astype(db.dtype)
        dc[...] = xc.astype(dc.dtype)

    gates = [pl.BlockSpec((tt, td), functools.partial(lambda i, j, s: (i, ob + s * nd + j), s=s)) for s in range(3)]
    blk = pl.BlockSpec((tt, td), lambda i, j: (i, j))
    dgl, da, db, dc = pl.pallas_call(
        body, name=name, grid=(t // tt, nd), in_specs=gates + [blk] * 4,
        out_specs=[pl.BlockSpec((3, tt, td), lambda i, j: (0, i, j)), blk, blk, blk],
        out_shape=[jax.ShapeDtypeStruct((3, t, d), MXU)] + [jax.ShapeDtypeStruct((t, d), MXU)] * 3,
        compiler_params=_cp("parallel", "parallel"),
    )(z, z, z, ba, bb, bc, dy)
    return dgl, da, db, dc


def _gdn_pre_f(qp, kp, vp, ab, cq, ck, cv, alog, dtb, h, n_heads):
    def conv(xp, cw):
        acc = xp * cw[CONV_K - 1]
        for j in range(1, CONV_K):
            acc = acc + shift(xp, j) * cw[CONV_K - 1 - j]
        return _silu(acc)

    q, k, v = conv(qp, cq), conv(kp, ck), conv(vp, cv)
    q = q * lax.rsqrt(jnp.sum(q * q, axis=-1, keepdims=True) + EPS) * (HD ** -0.5)
    k = k * lax.rsqrt(jnp.sum(k * k, axis=-1, keepdims=True) + EPS)
    lane = _iota2(ab.shape, 1)
    a_col = jnp.sum(jnp.where(lane == h, ab, 0.0), axis=-1, keepdims=True)
    b_col = jnp.sum(jnp.where(lane == n_heads + h, ab, 0.0), axis=-1, keepdims=True)
    lane1 = _iota2(alog.shape, 1)
    al = jnp.sum(jnp.where(lane1 == h, alog, 0.0), axis=-1, keepdims=True)
    dt = jnp.sum(jnp.where(lane1 == h, dtb, 0.0), axis=-1, keepdims=True)
    g = -jnp.exp(al) * _softplus(a_col + dt)
    return q, k, v, g, _sigmoid(b_col)


def _gdn_pre_specs(t, n_heads, ab_blk):
    zq = [pl.BlockSpec((t, HD), functools.partial(lambda h, s: (0, s * n_heads + h), s=s)) for s in range(3)]
    ab = pl.BlockSpec((t, LANE), lambda h: (0, ab_blk))
    cw = [pl.BlockSpec((CONV_K, HD), functools.partial(lambda h, s: (0, s * n_heads + h), s=s)) for s in range(3)]
    vec = pl.BlockSpec((1, LANE), lambda h: (0, 0))
    return zq, ab, cw, vec


def gdn_pre_fwd(z, ab_blk, conv_w, alog, dtb, n_heads, name):
    t = z.shape[0]
    zq, ab, cw, vec = _gdn_pre_specs(t, n_heads, ab_blk)

    def body(qp, kp, vp, ab_ref, cq, ck, cv, al, dt, q_o, k_o, v_o, g_o, b_o):
        rows = lambda r: tuple(r[j:j + 1, :] for j in range(CONV_K))
        outs = _gdn_pre_f(qp[...], kp[...], vp[...], ab_ref[...], rows(cq), rows(ck), rows(cv), al[...], dt[...],
                          pl.program_id(0), n_heads)
        for o_ref, val in zip((q_o, k_o, v_o, g_o, b_o), outs):
            o_ref[...] = val

    head = pl.BlockSpec((None, t, HD), lambda h: (h, 0, 0))
    col = pl.BlockSpec((None, t, 1), lambda h: (h, 0, 0))
    return pl.pallas_call(
        body, name=name, grid=(n_heads,), in_specs=zq + [ab] + cw + [vec, vec], out_specs=[head] * 3 + [col] * 2,
        out_shape=[jax.ShapeDtypeStruct((n_heads, t, HD), F32)] * 3 + [jax.ShapeDtypeStruct((n_heads, t, 1), F32)] * 2,
        compiler_params=_cp("parallel"),
    )(z, z, z, z, conv_w, conv_w, conv_w, alog, dtb)


def gdn_pre_bwd(z, ab_blk, conv_w, alog, dtb, n_heads, dq, dk, dv, dg, db, name):
    t = z.shape[0]
    gw = n_heads * HD
    zq, ab, cw, vec = _gdn_pre_specs(t, n_heads, ab_blk)

    def body(qp, kp, vp, ab_ref, cq, ck, cv, al, dt, dq_r, dk_r, dv_r, dg_r, db_r,
             dqp, dkp, dvp, dab, dcq, dck, dcv, dal, ddt):
        h = pl.program_id(0)
        rows = lambda r: tuple(r[j:j + 1, :] for j in range(CONV_K))
        f = functools.partial(_gdn_pre_f, h=h, n_heads=n_heads)
        _, vjp = jax.vjp(f, qp[...], kp[...], vp[...], ab_ref[...], rows(cq), rows(ck), rows(cv), al[...], dt[...])
        gq, gk, gv, gab, gcq, gck, gcv, gal, gdt = vjp((dq_r[...], dk_r[...], dv_r[...], dg_r[...], db_r[...]))
        dqp[...] = gq.astype(dqp.dtype)
        dkp[...] = gk.astype(dkp.dtype)
        dvp[...] = gv.astype(dvp.dtype)
        for ref, gr in ((dcq, gcq), (dck, gck), (dcv, gcv)):
            for j in range(CONV_K):
                ref[j:j + 1, :] = gr[j]

        @pl.when(h == 0)
        def _():
            dab[...] = jnp.zeros_like(dab)
            dal[...] = jnp.zeros_like(dal)
            ddt[...] = jnp.zeros_like(ddt)

        dab[...] += gab
        dal[...] += gal
        ddt[...] += gdt

    head = pl.BlockSpec((None, t, HD), lambda h: (h, 0, 0))
    col = pl.BlockSpec((None, t, 1), lambda h: (h, 0, 0))
    seg = pl.BlockSpec((t, HD), lambda h: (0, h))
    cseg = pl.BlockSpec((CONV_K, HD), lambda h: (0, h))
    return pl.pallas_call(
        body, name=name, grid=(n_heads,),
        in_specs=zq + [ab] + cw + [vec, vec] + [head] * 3 + [col] * 2,
        out_specs=[seg] * 3 + [pl.BlockSpec((t, LANE), lambda h: (0, 0))] + [cseg] * 3 + [vec, vec],
        out_shape=[jax.ShapeDtypeStruct((t, gw), MXU)] * 3 + [jax.ShapeDtypeStruct((t, LANE), F32)]
        + [jax.ShapeDtypeStruct((CONV_K, gw), F32)] * 3 + [jax.ShapeDtypeStruct((1, LANE), F32)] * 2,
        compiler_params=_cp("arbitrary"),
    )(z, z, z, z, conv_w, conv_w, conv_w, alog, dtb, dq, dk, dv, dg, db)


def _gdn_chunk_f(q, k, v, g, b):
    c = CHUNK
    r, s = _iota2((c, c), 0), _iota2((c, c), 1)
    tril = (s <= r).astype(F32)
    gc_w = mm(tril, jnp.broadcast_to(g, (c, HD)), "nn", "la")
    gc_i = mm(tril, jnp.broadcast_to(g, (c, c)), "nn", "la")
    gc_j = mm(jnp.ones((c, c), F32), jnp.where(r <= s, jnp.broadcast_to(g, (c, c)), 0.0), "nn", "la")
    decay = jnp.exp(jnp.where(s <= r, gc_i - gc_j, -1e30))
    kb = k * b
    low = jnp.where(s < r, mm(kb, k, "nt", "x3") * decay, 0.0)
    inv = jnp.where(r == s, 1.0, 0.0) - low
    pw = mm(low, low, "nn", "x3")
    n_sq = 1
    while 2 * n_sq < c:
        inv = inv + mm(inv, pw, "nn", "x3")
        n_sq *= 2
        if 2 * n_sq < c:
            pw = mm(pw, pw, "nn", "x3")
    egc = jnp.exp(gc_w)
    u = mm(inv, v * b, "nn", "x3")
    w = mm(inv, kb * egc, "nn", "x3")
    intra = mm(q, k, "nt", "lo") * decay
    g_last = jnp.sum(g, axis=0, keepdims=True)
    kd = k * jnp.exp(g_last - gc_w)
    egl = jnp.exp(jnp.broadcast_to(g_last, (1, HD)))
    return u, w, intra, q * egc, kd, egl


def _group(n, cap=4):
    return max(g for g in range(1, cap + 1) if n % g == 0)


def _chunk_specs(nb_h, nb_c, n_chunks=None):
    cn = (lambda n: n) if n_chunks is None else (lambda n: n_chunks // nb_c - 1 - n)
    rows = nb_c * CHUNK
    vec = pl.BlockSpec((nb_h, rows, HD), lambda h, n: (h, cn(n), 0))
    col = pl.BlockSpec((nb_h, rows, 1), lambda h, n: (h, cn(n), 0))
    sq = pl.BlockSpec((nb_h, rows, CHUNK), lambda h, n: (h, cn(n), 0))
    one = pl.BlockSpec((nb_h, nb_c, 1, HD), lambda h, n: (h, cn(n), 0, 0))
    st = pl.BlockSpec((nb_h, nb_c, HD, HD), lambda h, n: (h, cn(n), 0, 0))
    return vec, col, sq, one, st


def _chunk_shapes(n_heads, t):
    vec = jax.ShapeDtypeStruct((n_heads, t, HD), F32)
    return [vec, vec, jax.ShapeDtypeStruct((n_heads, t, CHUNK), F32), vec, vec,
            jax.ShapeDtypeStruct((n_heads, t // CHUNK, 1, HD), F32)]


def _chunk_rows(ci):
    return slice(ci * CHUNK, (ci + 1) * CHUNK)


def gdn_chunk_fwd(q, k, v, g, b, name, jobs=()):
    n_heads, t, _ = q.shape
    nb_c = _group(t // CHUNK)
    vec, col, sq, one, _ = _chunk_specs(1, nb_c)

    def body(q_r, k_r, v_r, g_r, b_r, *outs):
        for ci in range(nb_c):
            rows = _chunk_rows(ci)
            vals = _gdn_chunk_f(*(r[0, rows, :] for r in (q_r, k_r, v_r, g_r, b_r)))
            for o_ref, val in zip(outs[:5], vals[:5]):
                o_ref[0, rows, :] = val
            outs[5][0, ci] = vals[5]

    return hosted_call(
        body, (q, k, v, g, b), name=name, grid=(n_heads, t // CHUNK // nb_c), in_specs=[vec] * 3 + [col] * 2,
        out_specs=[vec, vec, sq, vec, vec, one], out_shape=_chunk_shapes(n_heads, t),
        sem=("parallel", "parallel"), jobs=jobs)


def gdn_chunk_bwd(q, k, v, g, b, cts, name, jobs=()):
    n_heads, t, _ = q.shape
    nb_c = _group(t // CHUNK)
    vec, col, sq, one, _ = _chunk_specs(1, nb_c)

    def body(q_r, k_r, v_r, g_r, b_r, du, dw, di, dqd, dkd, degl, dq, dk, dv, dg, db):
        for ci in range(nb_c):
            rows = _chunk_rows(ci)
            _, vjp = jax.vjp(_gdn_chunk_f, *(r[0, rows, :] for r in (q_r, k_r, v_r, g_r, b_r)))
            grads = vjp(tuple(r[0, rows, :] for r in (du, dw, di, dqd, dkd)) + (degl[0, ci],))
            for o_ref, val in zip((dq, dk, dv, dg, db), grads):
                o_ref[0, rows, :] = val

    col_shape = jax.ShapeDtypeStruct((n_heads, t, 1), F32)
    return hosted_call(
        body, (q, k, v, g, b, *cts), name=name, grid=(n_heads, t // CHUNK // nb_c),
        in_specs=[vec] * 3 + [col] * 2 + [vec, vec, sq, vec, vec, one],
        out_specs=[vec] * 3 + [col] * 2,
        out_shape=[jax.ShapeDtypeStruct((n_heads, t, HD), F32)] * 3 + [col_shape] * 2,
        sem=("parallel", "parallel"), jobs=jobs)


def _scan_f(s, u, w, a, qd, kd, egl):
    vn = u - mm(w, s, "nn", "lo")
    o = mm(qd, s, "nn", "lo") + mm(a, vn, "nn", "lo")
    return o, s * egl + mm(kd, vn, "tn", "lo")


def gdn_scan_fwd(chunks, name):
    u = chunks[0]
    n_heads, t, _ = u.shape
    nc = t // CHUNK
    nb_h = _group(n_heads)
    vec, _, sq, one, st = _chunk_specs(nb_h, 1)

    def body(u_r, w_r, a_r, qd_r, kd_r, e_r, o_ref, s_ref, state):
        @pl.when(pl.program_id(1) == 0)
        def _():
            state[...] = jnp.zeros_like(state)

        for hh in range(nb_h):
            s = state[hh]
            s_ref[hh, 0] = s
            o, s2 = _scan_f(s, u_r[hh], w_r[hh], a_r[hh], qd_r[hh], kd_r[hh], e_r[hh, 0])
            o_ref[hh] = o
            state[hh] = s2

    return pl.pallas_call(
        body, name=name, grid=(n_heads // nb_h, nc), in_specs=[vec, vec, sq, vec, vec, one], out_specs=[vec, st],
        out_shape=[jax.ShapeDtypeStruct((n_heads, t, HD), F32), jax.ShapeDtypeStruct((n_heads, nc, HD, HD), F32)],
        scratch_shapes=[pltpu.VMEM((nb_h, HD, HD), F32)], compiler_params=_cp("parallel", "arbitrary"),
    )(*chunks)


def gdn_scan_bwd(chunks, states, do, name):
    n_heads, t, _ = do.shape
    nc = t // CHUNK
    nb_h = _group(n_heads)
    vec, _, sq, one, st = _chunk_specs(nb_h, 1, n_chunks=nc)

    def body(u_r, w_r, a_r, qd_r, kd_r, e_r, s_r, do_r, du, dw, da, dqd, dkd, de, dstate):
        @pl.when(pl.program_id(1) == 0)
        def _():
            dstate[...] = jnp.zeros_like(dstate)

        for hh in range(nb_h):
            _, vjp = jax.vjp(_scan_f, s_r[hh, 0], u_r[hh], w_r[hh], a_r[hh], qd_r[hh], kd_r[hh], e_r[hh, 0])
            grads = vjp((do_r[hh], dstate[hh]))
            dstate[hh] = grads[0]
            for o_ref, val in zip((du, dw, da, dqd, dkd), grads[1:6]):
                o_ref[hh] = val
            de[hh, 0] = grads[6]

    return pl.pallas_call(
        body, name=name, grid=(n_heads // nb_h, nc), in_specs=[vec, vec, sq, vec, vec, one, st, vec],
        out_specs=[vec, vec, sq, vec, vec, one], out_shape=_chunk_shapes(n_heads, t),
        scratch_shapes=[pltpu.VMEM((nb_h, HD, HD), F32)], compiler_params=_cp("parallel", "arbitrary"),
    )(*chunks, states, do)


def _post_f(o, gate, gain):
    return _rms(o, gain) * _silu(gate)


def gdn_post_fwd(o, z, gate_blk, gain, name):
    n_heads, t, _ = o.shape
    tt = _tile(t, 512)

    def body(o_r, gt_r, gn_r, out):
        out[...] = _post_f(o_r[...], gt_r[...], gn_r[...]).astype(out.dtype)

    return pl.pallas_call(
        body, name=name, grid=(n_heads, t // tt),
        in_specs=[pl.BlockSpec((None, tt, HD), lambda h, i: (h, i, 0)), pl.BlockSpec((tt, HD), lambda h, i: (i, gate_blk + h)),
                  pl.BlockSpec((1, HD), lambda h, i: (0, 0))],
        out_specs=pl.BlockSpec((tt, HD), lambda h, i: (i, h)),
        out_shape=jax.ShapeDtypeStruct((t, n_heads * HD), MXU), compiler_params=_cp("parallel", "parallel"),
    )(o, z, gain)


def gdn_post_bwd(o, z, gate_blk, gain, doa, name):
    n_heads, t, _ = o.shape
    tt = _tile(t, 512)

    def body(o_r, gt_r, gn_r, d_r, do_ref, dgt_ref, dgn_ref):
        _, vjp = jax.vjp(_post_f, o_r[...], gt_r[...], gn_r[...])
        go, ggt, ggn = vjp(d_r[...])
        do_ref[...] = go
        dgt_ref[...] = ggt.astype(dgt_ref.dtype)

        @pl.when((pl.program_id(0) == 0) & (pl.program_id(1) == 0))
        def _():
            dgn_ref[...] = jnp.zeros_like(dgn_ref)

        dgn_ref[...] += ggn

    tok = pl.BlockSpec((tt, HD), lambda h, i: (i, h))
    vec = pl.BlockSpec((1, HD), lambda h, i: (0, 0))
    head = pl.BlockSpec((None, tt, HD), lambda h, i: (h, i, 0))
    return pl.pallas_call(
        body, name=name, grid=(n_heads, t // tt),
        in_specs=[head, pl.BlockSpec((tt, HD), lambda h, i: (i, gate_blk + h)), vec, tok],
        out_specs=[head, tok, vec],
        out_shape=[jax.ShapeDtypeStruct((n_heads, t, HD), F32), jax.ShapeDtypeStruct((t, n_heads * HD), MXU),
                   jax.ShapeDtypeStruct((1, HD), F32)],
        compiler_params=_cp("arbitrary", "arbitrary"),
    )(o, z, gain, doa)


def _gmlp_f(ups, vps, lngs, wss, bcols):
    n_groups = len(ups)
    width = HD * n_groups
    us = [_gelu(a) for a in ups]
    vs = [_gelu(a) for a in vps]
    mu = sum(jnp.sum(a, axis=-1, keepdims=True) for a in vs) * (1.0 / width)
    xcs = [a - mu for a in vs]
    var = sum(jnp.sum(a * a, axis=-1, keepdims=True) for a in xcs) * (1.0 / width)
    rstd = lax.rsqrt(var + EPS)
    r, s = _iota2((HD, HD), 0), _iota2((HD, HD), 1)
    causal = (s // CHUNK) <= (r // CHUNK)
    outs = []
    for gi in range(n_groups):
        vb = xcs[gi] * rstd * lngs[gi]
        sp = mm(jnp.where(causal, wss[gi], 0.0), vb, "nn", "lo") + bcols[gi]
        outs.append(us[gi] * sp)
    return tuple(outs)


def _gmlp_load(uv_u, uv_v, lng, ws, bt, n_groups):
    seg = lambda ref, gi: ref[:, gi * HD:(gi + 1) * HD]
    return ([seg(uv_u, gi) for gi in range(n_groups)], [seg(uv_v, gi) for gi in range(n_groups)],
            [seg(lng, gi) for gi in range(n_groups)], [ws[gi] for gi in range(n_groups)],
            [bt[:, gi:gi + 1] for gi in range(n_groups)])


def _gmlp_specs(width, u_blk, n_groups):
    u = pl.BlockSpec((HD, width), lambda i: (i, u_blk))
    v = pl.BlockSpec((HD, width), lambda i: (i, u_blk + 1))
    lng = pl.BlockSpec((1, width), lambda i: (0, 0))
    ws = pl.BlockSpec((n_groups, HD, HD), lambda i: (0, 0, 0))
    bt = pl.BlockSpec((HD, LANE), lambda i: (0, 0))
    return u, v, lng, ws, bt


def gmlp_fwd(z, uv_off, width, lng, ws, bt, name):
    t = z.shape[0]
    n_groups = width // HD
    specs = _gmlp_specs(width, uv_off // width, n_groups)

    def body(u_r, v_r, l_r, w_r, b_r, out):
        outs = _gmlp_f(*_gmlp_load(u_r, v_r, l_r, w_r, b_r, n_groups))
        for gi in range(n_groups):
            out[:, gi * HD:(gi + 1) * HD] = outs[gi].astype(out.dtype)

    return pl.pallas_call(
        body, name=name, grid=(t // HD,), in_specs=list(specs), out_specs=pl.BlockSpec((HD, width), lambda i: (i, 0)),
        out_shape=jax.ShapeDtypeStruct((t, width), MXU), compiler_params=_cp("parallel"),
    )(z, z, lng, ws, bt)


def gmlp_bwd(z, uv_off, width, lng, ws, bt, dob, name):
    t = z.shape[0]
    n_groups = width // HD
    specs = _gmlp_specs(width, uv_off // width, n_groups)

    def body(u_r, v_r, l_r, w_r, b_r, d_r, duv, dl, dws, dbt):
        _, vjp = jax.vjp(_gmlp_f, *_gmlp_load(u_r, v_r, l_r, w_r, b_r, n_groups))
        gu, gv, gl, gw, gb = vjp(tuple(d_r[:, gi * HD:(gi + 1) * HD] for gi in range(n_groups)))

        @pl.when(pl.program_id(0) == 0)
        def _():
            dl[...] = jnp.zeros_like(dl)
            dws[...] = jnp.zeros_like(dws)
            dbt[...] = jnp.zeros_like(dbt)

        for gi in range(n_groups):
            duv[:, gi * HD:(gi + 1) * HD] = gu[gi].astype(duv.dtype)
            duv[:, width + gi * HD:width + (gi + 1) * HD] = gv[gi].astype(duv.dtype)
            dl[:, gi * HD:(gi + 1) * HD] += gl[gi]
            dws[gi] += gw[gi]
            dbt[:, gi:gi + 1] += gb[gi]

    return pl.pallas_call(
        body, name=name, grid=(t // HD,), in_specs=list(specs) + [pl.BlockSpec((HD, width), lambda i: (i, 0))],
        out_specs=[pl.BlockSpec((HD, 2 * width), lambda i: (i, 0)), specs[2], specs[3], specs[4]],
        out_shape=[jax.ShapeDtypeStruct((t, 2 * width), MXU), jax.ShapeDtypeStruct((1, width), F32),
                   jax.ShapeDtypeStruct((n_groups, HD, HD), F32), jax.ShapeDtypeStruct((HD, LANE), F32)],
        compiler_params=_cp("arbitrary"),
    )(z, z, lng, ws, bt, dob)


def _sba_pre_f(qp, kp, qg, kg):
    return _rms(qp, qg), _rms(kp, kg)


def sba_pre_fwd(z, c_blk, n_heads, qg, kg, name):
    t = z.shape[0]
    tt = _tile(t, 512)
    zs = [pl.BlockSpec((tt, HD), functools.partial(lambda h, i, s: (i, c_blk + s * n_heads + h), s=s)) for s in range(3)]
    vec = pl.BlockSpec((1, HD), lambda h, i: (0, 0))
    head = pl.BlockSpec((None, tt, HD), lambda h, i: (h, i, 0))

    def body(qp, kp, vp, qg_r, kg_r, q_o, k_o, v_o):
        q, k = _sba_pre_f(qp[...], kp[...], qg_r[...], kg_r[...])
        q_o[...] = q.astype(q_o.dtype)
        k_o[...] = k.astype(k_o.dtype)
        v_o[...] = vp[...].astype(v_o.dtype)

    return pl.pallas_call(
        body, name=name, grid=(n_heads, t // tt), in_specs=zs + [vec, vec], out_specs=[head] * 3,
        out_shape=[jax.ShapeDtypeStruct((n_heads, t, HD), MXU)] * 3, compiler_params=_cp("parallel", "parallel"),
    )(z, z, z, qg, kg)


def sba_pre_bwd(z, c_blk, n_heads, qg, kg, dq, dk, dv, name):
    t = z.shape[0]
    tt = _tile(t, 512)
    zs = [pl.BlockSpec((tt, HD), functools.partial(lambda h, i, s: (i, c_blk + s * n_heads + h), s=s)) for s in range(2)]
    vec = pl.BlockSpec((1, HD), lambda h, i: (0, 0))
    head = pl.BlockSpec((None, tt, HD), lambda h, i: (h, i, 0))
    tok = pl.BlockSpec((tt, HD), lambda h, i: (i, h))

    def body(qp, kp, qg_r, kg_r, dq_r, dk_r, dv_r, dqp, dkp, dvp, dqg, dkg):
        _, vjp = jax.vjp(_sba_pre_f, qp[...], kp[...], qg_r[...], kg_r[...])
        gq, gk, gqg, gkg = vjp((dq_r[...], dk_r[...]))
        dqp[...] = gq.astype(dqp.dtype)
        dkp[...] = gk.astype(dkp.dtype)
        dvp[...] = dv_r[...].astype(dvp.dtype)

        @pl.when((pl.program_id(0) == 0) & (pl.program_id(1) == 0))
        def _():
            dqg[...] = jnp.zeros_like(dqg)
            dkg[...] = jnp.zeros_like(dkg)

        dqg[...] += gqg
        dkg[...] += gkg

    return pl.pallas_call(
        body, name=name, grid=(n_heads, t // tt), in_specs=zs + [vec, vec] + [head] * 3,
        out_specs=[tok] * 3 + [vec, vec],
        out_shape=[jax.ShapeDtypeStruct((t, n_heads * HD), MXU)] * 3 + [jax.ShapeDtypeStruct((1, HD), F32)] * 2,
        compiler_params=_cp("arbitrary", "arbitrary"),
    )(z, z, qg, kg, dq, dk, dv)


def _sba_block(q, kj, i, j):
    zz = lax.dot_general(q, kj, _DN["nt"], preferred_element_type=F32) * (HD ** -0.5)
    ls = _logsig(zz)
    strict = (j * HD + _iota2((HD, HD), 1)) < (i * HD + _iota2((HD, HD), 0))
    return zz, ls, jnp.where(strict, ls - zz, 0.0), strict


def sba_fwd(q, k, v, name, jobs=()):
    n_heads, t, _ = q.shape
    nb_h = _group(n_heads)

    def body(q_r, k_r, v_r, o_ref, tot_ref):
        i = pl.program_id(1)
        after = (_iota2((HD, HD), 0) > _iota2((HD, HD), 1)).astype(F32)

        def step(it, carry):
            j = i - it
            rows = pl.ds(pl.multiple_of(j * HD, HD), HD)
            new = []
            for hh in range(nb_h):
                acc, cs = carry[hh]
                _, ls, lk, strict = _sba_block(q_r[hh], k_r[hh, rows, :], i, j)
                suffix = _mm_raw(lk, after, "nn", "rb") + cs
                att = jnp.where(strict, jnp.exp(ls + suffix), 0.0)
                acc = acc + _mm_raw(att, v_r[hh, rows, :], "nn", "lo")
                new.append((acc, cs + jnp.sum(lk, axis=-1, keepdims=True)))
            return tuple(new)

        init = tuple((jnp.zeros((HD, HD), F32), jnp.zeros((HD, 1), F32)) for _ in range(nb_h))
        final = lax.fori_loop(0, i + 1, step, init)
        for hh in range(nb_h):
            o_ref[:, hh * HD:(hh + 1) * HD] = final[hh][0].astype(o_ref.dtype)
            tot_ref[hh] = final[hh][1]

    full = pl.BlockSpec((nb_h, t, HD), lambda h, i: (h, 0, 0))
    return hosted_call(
        body, (q, k, v), name=name, grid=(n_heads // nb_h, t // HD),
        in_specs=[pl.BlockSpec((nb_h, HD, HD), lambda h, i: (h, i, 0)), full, full],
        out_specs=[pl.BlockSpec((HD, nb_h * HD), lambda h, i: (i, h)), pl.BlockSpec((nb_h, HD, 1), lambda h, i: (h, i, 0))],
        out_shape=[jax.ShapeDtypeStruct((t, n_heads * HD), MXU), jax.ShapeDtypeStruct((n_heads, t, 1), F32)],
        sem=("parallel", "parallel"), jobs=jobs)


def sba_bwd(q, k, v, tot, do, name, jobs=()):
    n_heads, t, _ = q.shape
    nb_h = _group(n_heads)

    def body(q_r, k_r, v_r, tot_r, do_r, dq_ref, dk_ref, dv_ref):
        i = pl.program_id(1)

        @pl.when(i == 0)
        def _():
            dk_ref[...] = jnp.zeros_like(dk_ref)
            dv_ref[...] = jnp.zeros_like(dv_ref)

        r, s = _iota2((HD, HD), 0), _iota2((HD, HD), 1)
        upto = (r <= s).astype(F32)
        before = (r < s).astype(F32)

        def step(j, carry):
            rows = pl.ds(pl.multiple_of(j * HD, HD), HD)
            new = []
            for hh in range(nb_h):
                dq, cp, cd = carry[hh]
                qb, dob = q_r[hh], do_r[:, hh * HD:(hh + 1) * HD]
                kj, vj = k_r[hh, rows, :], v_r[hh, rows, :]
                _, ls, lk, strict = _sba_block(qb, kj, i, j)
                sig = jnp.exp(ls)
                suffix = tot_r[hh] - (cp + _mm_raw(lk, upto, "nn", "rb"))
                att = jnp.where(strict, jnp.exp(ls + suffix), 0.0)
                dp = _mm_raw(dob, vj, "nt", "lo") * att
                dlk = cd + _mm_raw(dp, before, "nn", "rb")
                dz = jnp.where(strict, dp * (1.0 - sig) - dlk * sig, 0.0) * (HD ** -0.5)
                dk_ref[hh, rows, :] += _mm_raw(dz, qb, "tn", "lo")
                dv_ref[hh, rows, :] += _mm_raw(att, dob, "tn", "lo")
                new.append((dq + _mm_raw(dz, kj, "nn", "lo"), cp + jnp.sum(lk, axis=-1, keepdims=True),
                            cd + jnp.sum(dp, axis=-1, keepdims=True)))
            return tuple(new)

        zero_col = jnp.zeros((HD, 1), F32)
        final = lax.fori_loop(0, i + 1, step, tuple((jnp.zeros((HD, HD), F32), zero_col, zero_col) for _ in range(nb_h)))
        for hh in range(nb_h):
            dq_ref[hh] = final[hh][0]

    full = pl.BlockSpec((nb_h, t, HD), lambda h, i: (h, 0, 0))
    blk = pl.BlockSpec((nb_h, HD, HD), lambda h, i: (h, i, 0))
    return hosted_call(
        body, (q, k, v, tot, do), name=name, grid=(n_heads // nb_h, t // HD),
        in_specs=[blk, full, full, pl.BlockSpec((nb_h, HD, 1), lambda h, i: (h, i, 0)),
                  pl.BlockSpec((HD, nb_h * HD), lambda h, i: (i, h))],
        out_specs=[blk, full, full], out_shape=[jax.ShapeDtypeStruct((n_heads, t, HD), F32)] * 3,
        sem=("parallel", "arbitrary"), jobs=jobs)


def small_adam(parts, w, m, v, name):
    n_parts, rows, _ = parts.shape
    tr = _tile(rows, 512) if rows % LANE == 0 else rows

    def body(p_ref, w_ref, m_ref, v_ref, g_out, d_out, m_out, v_out):
        g = p_ref[0]
        for k in range(1, n_parts):
            g = g + p_ref[k]
        delta, m2, v2 = _adam_math(w_ref[...], g, m_ref[...], v_ref[...])
        g_out[...] = g
        d_out[...] = delta
        m_out[...] = m2
        v_out[...] = v2

    blk = pl.BlockSpec((tr, LANE), lambda i: (i, 0))
    return pl.pallas_call(
        body, name=name, grid=(rows // tr,),
        in_specs=[pl.BlockSpec((n_parts, tr, LANE), lambda i: (0, i, 0)), blk, blk, blk], out_specs=[blk] * 4,
        out_shape=[jax.ShapeDtypeStruct((rows, LANE), F32)] * 4, compiler_params=_cp("parallel"),
    )(parts, w, m, v)


def _pack(arrays):
    flat = jnp.concatenate([a.reshape(-1).astype(F32) for a in arrays])
    pad = (-flat.shape[0]) % (8 * LANE)
    return jnp.pad(flat, (0, pad)).reshape(-1, LANE)


def _unpack(packed, shapes):
    flat, outs, pos = packed.reshape(-1), [], 0
    for shp in shapes:
        n = 1
        for s in shp:
            n *= s
        outs.append(flat[pos:pos + n].reshape(shp))
        pos += n
    return outs


def _pad_lanes(a):
    return jnp.pad(a, ((0, 0), (0, LANE - a.shape[1])))


def kernel(x, w_in, conv_w, a_log, dt_bias, gdn_norm_g, gmlp_ln_g, w_spatial, b_spatial, sba_q_g, sba_k_g, w_out_a, w_out_b, w_out_c, w_out, norm_mix_g, norm_mlp_g, w_ff1, w_ff2, loss_target, m_w_in, m_conv_w, m_a_log, m_dt_bias, m_gdn_norm_g, m_gmlp_ln_g, m_w_spatial, m_b_spatial, m_sba_q_g, m_sba_k_g, m_w_out_a, m_w_out_b, m_w_out_c, m_w_out, m_norm_mix_g, m_norm_mlp_g, m_w_ff1, m_w_ff2, v_w_in, v_conv_w, v_a_log, v_dt_bias, v_gdn_norm_g, v_gmlp_ln_g, v_w_spatial, v_b_spatial, v_sba_q_g, v_sba_k_g, v_w_out_a, v_w_out_b, v_w_out_c, v_w_out, v_norm_mix_g, v_norm_mlp_g, v_w_ff1, v_w_ff2):
    depth = w_in.shape[0]
    _, t, d = x.shape
    n_heads = d // 256
    gw = n_heads * HD
    width = d // 2
    n_groups = width // HD
    off_gate, off_uv, off_c, off_gl = 3 * gw, 4 * gw, 4 * gw + 2 * width, 7 * gw + 2 * width
    off_ab = off_gl + 3 * d
    n_packed = off_ab + LANE
    n_in = off_ab + 2 * n_heads
    assert w_in.shape[2] * N_DEV == n_in and t % LANE == 0 and d % 256 == 0

    ix, iy, ic = lax.axis_index("x"), lax.axis_index("y"), lax.axis_index("c")
    dev = 4 * ix + 2 * iy + ic
    c_idx = jnp.reshape(ic, (1,)).astype(jnp.int32)
    xs, target = x[0], loss_target[0]

    w_in_t, m_in_t, v_in_t = (jnp.swapaxes(a, 1, 2) for a in (w_in, m_w_in, v_w_in))
    shard_of = dict(w_in=w_in_t, w_out_a=w_out_a, w_out_b=w_out_b, w_out_c=w_out_c, w_out=w_out, w_ff1=w_ff1, w_ff2=w_ff2)
    ag_jobs = lambda l: {nm: GatherJob(w[l].astype(MXU)) for nm, w in shard_of.items()}
    layer0 = ag_jobs(0)
    conv_job = GatherJob(conv_w)
    run_jobs(list(layer0.values()) + [conv_job], "ag_layer0")
    gathered_w = [{nm: job.results[0] for nm, job in layer0.items()}]
    conv_full = jnp.transpose(conv_job.results[0], (1, 2, 0, 3)).reshape(depth, CONV_K, 3 * gw)

    per_dev = n_in // N_DEV

    def pack_in(g):
        w = g.reshape(n_in, d)
        return jnp.concatenate([w[:3 * gw], w[3 * gw + 2 * n_heads:], w[3 * gw:3 * gw + 2 * n_heads],
                                jnp.zeros((LANE - 2 * n_heads, d), w.dtype)], axis=0)

    def unpack_in(wp):
        w = jnp.concatenate([wp[:3 * gw], wp[off_ab:off_ab + 2 * n_heads], wp[3 * gw:off_ab]], axis=0)
        return w.reshape(N_DEV, per_dev, d)

    alog_p, dtb_p = _pad_lanes(a_log), _pad_lanes(dt_bias)
    bt_all = jnp.pad(jnp.transpose(b_spatial, (0, 2, 1)), ((0, 0), (0, 0), (0, LANE - n_groups)))

    saved = []
    cur = xs
    for l in range(depth):
        gl_w = gathered_w[l]
        lw = dict(w_in=pack_in(gl_w["w_in"]), conv=conv_full[l], alog=alog_p[l:l + 1], dtb=dtb_p[l:l + 1],
                  gng=gdn_norm_g[l:l + 1], lng=gmlp_ln_g[l:l + 1], ws=w_spatial[l], bt=bt_all[l],
                  qg=sba_q_g[l:l + 1], kg=sba_k_g[l:l + 1], gmix=norm_mix_g[l:l + 1], gmlp=norm_mlp_g[l:l + 1])
        nxt = ag_jobs(l + 1) if l + 1 < depth else {}
        ride = lambda *names: [nxt[nm] for nm in names if nm in nxt]
        s = dict(lw=lw, x=cur, gw=gl_w)
        s["h1"] = rms_fwd(cur, lw["gmix"], "rms_mix")
        z = s["z"] = matmul(s["h1"], lw["w_in"], "nt", F32, "mm_in", caps=(1024, 1408, 2048), jobs=ride("w_ff2"))
        s["pre"] = gdn_pre_fwd(z, off_ab // LANE, lw["conv"], lw["alog"], lw["dtb"], n_heads, "gdn_pre")
        s["chunks"] = gdn_chunk_fwd(*s["pre"], "gdn_chunk", jobs=ride("w_in"))
        s["o"], s["states"] = gdn_scan_fwd(s["chunks"], "gdn_scan")
        s["oa"] = gdn_post_fwd(s["o"], z, off_gate // HD, lw["gng"], "gdn_post")
        s["ob"] = gmlp_fwd(z, off_uv, width, lw["lng"], lw["ws"], lw["bt"], "gmlp")
        s["qkv_c"] = sba_pre_fwd(z, off_c // HD, n_heads, lw["qg"], lw["kg"], "sba_pre")
        s["oc"], s["tot"] = sba_fwd(*s["qkv_c"], "sba", jobs=ride("w_ff1"))
        s["ba"] = matmul(s["oa"], gl_w["w_out_a"], "nn", F32, "mm_oa", b_view="cols")
        s["bb"] = matmul(s["ob"], gl_w["w_out_b"], "nn", F32, "mm_ob", b_view="cols")
        s["bc"] = matmul(s["oc"], gl_w["w_out_c"], "nn", F32, "mm_oc", b_view="cols")
        s["y"] = merge_fwd(z, off_gl, s["ba"], s["bb"], s["bc"], "merge")
        s["x1"] = matmul(s["y"], gl_w["w_out"], "nn", F32, "mm_out", res=cur, b_view="rows")
        s["h2"] = rms_fwd(s["x1"], lw["gmlp"], "rms_mlp")
        s["f"], s["a"] = matmul(s["h2"], gl_w["w_ff1"], "nn", F32, "mm_ff1", b_view="cols", post="sqrelu",
                                jobs=ride("w_out", "w_out_a"))
        cur = matmul(s["a"], gl_w["w_ff2"], "nn", F32, "mm_ff2", res=s["x1"], b_view="rows",
                     jobs=ride("w_out_b", "w_out_c"))
        if nxt:
            gathered_w.append({nm: job.results[0] for nm, job in nxt.items()})
        saved.append(s)

    dx, dxb, loss_tile = loss_head(cur, target, "loss_head")
    loss = lax.psum(loss_tile[0, 0], AXES)

    big = dict(w_in=(w_in_t, m_in_t, v_in_t), w_out_a=(w_out_a, m_w_out_a, v_w_out_a), w_out_b=(w_out_b, m_w_out_b, v_w_out_b),
               w_out_c=(w_out_c, m_w_out_c, v_w_out_c), w_out=(w_out, m_w_out, v_w_out), w_ff1=(w_ff1, m_w_ff1, v_w_ff1),
               w_ff2=(w_ff2, m_w_ff2, v_w_ff2))
    bufs = {nm: tuple(lax.empty(w.shape, F32) for _ in range(4)) for nm, (w, _, _) in big.items()}
    pairing = []
    waiting = []

    def pair_reduce(nm, g8, l):
        w = big[nm][0]
        g8 = g8.reshape(N_DEV, w.shape[1], w.shape[2])
        pairing.append((nm, l, g8, PairExchangeJob(g8)))

    def pair_jobs():
        return [e[3] for e in pairing]

    def pair_done():
        for nm, l, g8, job in pairing:
            waiting.append((nm, l, ChipExchangeJob(pair_sum(g8, job.results[0], c_idx, "rs_pair_sum_" + nm))))
        pairing.clear()

    def take(*names):
        picked = [e for e in waiting if e[0] in names]
        for e in picked:
            waiting.remove(e)
        return picked

    def update(picked):
        for nm, l, job in picked:
            w, m, v = big[nm]
            bufs[nm] = tuple(adam_layer(job.results[0], w, m, v, bufs[nm], l, "adam_" + nm))

    small_grads = []
    for l in reversed(range(depth)):
        s = saved[l]
        lw, z, gl_w = s["lw"], s["z"], s["gw"]
        df = matmul(dxb, gl_w["w_ff2"], "nt", MXU, "mm_ff2_dx", b_view="rows", res=s["f"], post="sqrelu_bwd", jobs=pair_jobs())
        pair_done()
        pair_reduce("w_ff2", matmul(s["a"], dxb, "tn", MXU, "mm_ff2_dw"), l)
        dh2 = matmul(df, gl_w["w_ff1"], "nt", F32, "mm_ff1_dx", b_view="cols", jobs=pair_jobs())
        pair_done()
        pair_reduce("w_ff1", matmul(s["h2"], df, "tn", MXU, "mm_ff1_dw", out_slabs=w_ff1.shape[2]), l)
        dx1, dx1b, d_gmlp = rms_bwd(s["x1"], lw["gmlp"], dh2, dx, "rms_mlp_bwd")
        dy = matmul(dx1b, gl_w["w_out"], "nt", F32, "mm_out_dx", b_view="rows", jobs=pair_jobs())
        pair_done()
        pair_reduce("w_out", matmul(s["y"], dx1b, "tn", MXU, "mm_out_dw"), l)
        dgl, dba, dbb, dbc = merge_bwd(z, off_gl, s["ba"], s["bb"], s["bc"], dy, "merge_bwd")
        doa = matmul(dba, gl_w["w_out_a"], "nt", F32, "mm_oa_dx", b_view="cols", jobs=pair_jobs())
        pair_done()
        dob = matmul(dbb, gl_w["w_out_b"], "nt", F32, "mm_ob_dx", b_view="cols")
        doc = matmul(dbc, gl_w["w_out_c"], "nt", F32, "mm_oc_dx", b_view="cols")
        slab = w_out_a.shape[2]
        pair_reduce("w_out_a", matmul(s["oa"], dba, "tn", MXU, "mm_oa_dw", out_slabs=slab), l)
        pair_reduce("w_out_b", matmul(s["ob"], dbb, "tn", MXU, "mm_ob_dw", out_slabs=slab), l)
        pair_reduce("w_out_c", matmul(s["oc"], dbc, "tn", MXU, "mm_oc_dw", out_slabs=slab), l)
        riding = take("w_ff2", "w_ff1", "w_out")
        dqc, dkc, dvc = sba_bwd(*s["qkv_c"], s["tot"], doc, "sba_bwd", jobs=[e[2] for e in riding] + pair_jobs())
        update(riding)
        pair_done()
        dz_qc, dz_kc, dz_vc, d_qg, d_kg = sba_pre_bwd(z, off_c // HD, n_heads, lw["qg"], lw["kg"], dqc, dkc, dvc, "sba_pre_bwd")
        dz_uv, d_lng, d_ws, d_bt = gmlp_bwd(z, off_uv, width, lw["lng"], lw["ws"], lw["bt"], dob, "gmlp_bwd")
        do, dz_gate, d_gng = gdn_post_bwd(s["o"], z, off_gate // HD, lw["gng"], doa, "gdn_post_bwd")
        chunk_cts = gdn_scan_bwd(s["chunks"], s["states"], do, "gdn_scan_bwd")
        riding = take("w_in", "w_out_a", "w_out_b", "w_out_c")
        dqa, dka, dva, dga, dba_ = gdn_chunk_bwd(*s["pre"], chunk_cts, "gdn_chunk_bwd", jobs=[e[2] for e in riding])
        update(riding)
        dz_q, dz_k, dz_v, d_ab, d_cq, d_ck, d_cv, d_alog, d_dtb = gdn_pre_bwd(
            z, off_ab // LANE, lw["conv"], lw["alog"], lw["dtb"], n_heads, dqa, dka, dva, dga, dba_, "gdn_pre_bwd")
        dz = jnp.concatenate([dz_q, dz_k, dz_v, dz_gate, dz_uv, dz_qc, dz_kc, dz_vc, dgl[0], dgl[1], dgl[2],
                              d_ab.astype(MXU)], axis=1)
        dh1 = matmul(dz, lw["w_in"], "nn", F32, "mm_in_dx", caps=(1024, 1024, 1408))
        pair_reduce("w_in", unpack_in(matmul(dz, s["h1"], "tn", MXU, "mm_in_dw", caps=(1408, 1024, 2048))), l)
        dx, dxb, d_gmix = rms_bwd(s["x"], lw["gmix"], dh1, dx1, "rms_mix_bwd")
        small_grads.append(dict(
            conv_w=jnp.concatenate([d_cq, d_ck, d_cv], axis=1), a_log=d_alog[0, :n_heads], dt_bias=d_dtb[0, :n_heads],
            gdn_norm_g=d_gng[0], gmlp_ln_g=d_lng[0], w_spatial=d_ws, b_spatial=jnp.transpose(d_bt[:, :n_groups]),
            sba_q_g=d_qg[0], sba_k_g=d_kg[0], norm_mix_g=d_gmix[0], norm_mlp_g=d_gmlp[0]))
    small_grads = small_grads[::-1]
    run_jobs(pair_jobs(), "rs_pair_last")
    pair_done()
    rest = take("w_in")
    run_jobs([e[2] for e in rest], "rs_chip_last")
    update(rest)

    rep_names = ["a_log", "dt_bias", "gdn_norm_g", "gmlp_ln_g", "w_spatial", "b_spatial", "sba_q_g", "sba_k_g",
                 "norm_mix_g", "norm_mlp_g"]
    rep = dict(a_log=(a_log, m_a_log, v_a_log), dt_bias=(dt_bias, m_dt_bias, v_dt_bias),
               gdn_norm_g=(gdn_norm_g, m_gdn_norm_g, v_gdn_norm_g), gmlp_ln_g=(gmlp_ln_g, m_gmlp_ln_g, v_gmlp_ln_g),
               w_spatial=(w_spatial, m_w_spatial, v_w_spatial), b_spatial=(b_spatial, m_b_spatial, v_b_spatial),
               sba_q_g=(sba_q_g, m_sba_q_g, v_sba_q_g), sba_k_g=(sba_k_g, m_sba_k_g, v_sba_k_g),
               norm_mix_g=(norm_mix_g, m_norm_mix_g, v_norm_mix_g), norm_mlp_g=(norm_mlp_g, m_norm_mlp_g, v_norm_mlp_g))
    stack = lambda nm: jnp.stack([sg[nm] for sg in small_grads])
    conv_cols = conv_w.shape[2]
    conv_pad = jnp.zeros((depth, CONV_K, 3 * gw - conv_cols), F32)
    widen = lambda a: jnp.concatenate([a, conv_pad], axis=2)
    grads_packed = _pack([stack(nm) for nm in rep_names] + [stack("conv_w")])
    small_job = GatherJob(grads_packed)
    run_jobs([small_job], "ag_small_grads")
    gathered = small_job.results[0]
    packed = [_pack([rep[nm][k] for nm in rep_names] + [widen((conv_w, m_conv_w, v_conv_w)[k])]) for k in range(3)]
    shapes = [rep[nm][0].shape for nm in rep_names] + [(depth, CONV_K, 3 * gw)]
    flat = gathered.reshape(N_DEV, -1)
    n_rep = sum(int(rep[nm][0].size) for nm in rep_names)
    conv_part = flat[:, n_rep:n_rep + depth * CONV_K * 3 * gw].reshape(N_DEV, depth, CONV_K, 3 * gw)
    conv_mine = lax.dynamic_slice_in_dim(conv_part, dev * conv_cols, conv_cols, axis=3)
    conv_mine = jnp.concatenate([conv_mine, jnp.zeros((N_DEV, depth, CONV_K, 3 * gw - conv_cols), F32)], axis=3)
    tail = flat[:, n_rep + depth * CONV_K * 3 * gw:]
    parts = jnp.concatenate([flat[:, :n_rep], conv_mine.reshape(N_DEV, -1), tail], axis=1).reshape(gathered.shape)
    outs_small = small_adam(parts, packed[0], packed[1], packed[2], "adam_small")
    small = [dict(zip(rep_names + ["conv_w"], _unpack(o, shapes))) for o in outs_small]
    for sm in small:
        sm["conv_w"] = sm["conv_w"][:, :, :conv_cols]

    order = ["w_in", "conv_w", "a_log", "dt_bias", "gdn_norm_g", "gmlp_ln_g", "w_spatial", "b_spatial", "sba_q_g",
             "sba_k_g", "w_out_a", "w_out_b", "w_out_c", "w_out", "norm_mix_g", "norm_mlp_g", "w_ff1", "w_ff2"]
    result = [loss, dx[None]]
    for kind in range(4):
        for nm in order:
            if nm == "w_in":
                result.append(jnp.swapaxes(bufs[nm][kind], 1, 2))
            else:
                result.append(bufs[nm][kind] if nm in bufs else small[kind][nm])
    return tuple(result)
```

```python
import functools

import jax
import jax.numpy as jnp
from jax import lax
from jax.experimental import pallas as pl
from jax.experimental.pallas import tpu as pltpu

F32 = jnp.float32
MXU = jnp.bfloat16
N_DEV = 8
AXES = ("x", "y", "c")
CHUNK = 64
HD = 128
CONV_K = 4
EPS = 1e-6
LANE = 128
VMEM_LIMIT = 56 * 1024 * 1024
ADAM_LR, ADAM_B1, ADAM_B2, ADAM_EPS, ADAM_WD, ADAM_STEP = 0.001, 0.9, 0.999, 1e-08, 0.01, 10

_ANY = pl.BlockSpec(memory_space=pl.ANY)
_MESH = pl.DeviceIdType.MESH
_DN = {"nn": (((1,), (0,)), ((), ())), "nt": (((1,), (1,)), ((), ())), "tn": (((0,), (0,)), ((), ()))}


def _cp(*sem):
    return pltpu.CompilerParams(dimension_semantics=sem, vmem_limit_bytes=VMEM_LIMIT)


def _tile(n, cap):
    if n <= cap:
        return n
    best = LANE
    for t in range(LANE, cap + 1, LANE):
        if n % t == 0:
            best = t
    assert n % best == 0, (n, cap)
    return best


def _split(x, n):
    parts, rest = [], x.astype(F32)
    for _ in range(n):
        p = rest.astype(MXU)
        parts.append(p)
        rest = rest - p.astype(F32)
    return parts


def _mm_raw(a, b, mode, prec):
    dot = lambda p, q: lax.dot_general(p, q, _DN[mode], preferred_element_type=F32)
    if prec == "lo" or MXU == F32:
        return dot(a.astype(MXU), b.astype(MXU))
    if prec == "x3":
        (ah, al), (bh, bl) = _split(a, 2), _split(b, 2)
        return dot(ah, bh) + (dot(al, bh) + dot(ah, bl))
    if prec == "la":
        ae, (bh, bl) = a.astype(MXU), _split(b, 2)
        return dot(ae, bh) + dot(ae, bl)
    assert prec == "rb"
    (ah, al), be = _split(a, 2), b.astype(MXU)
    return dot(ah, be) + dot(al, be)


@functools.partial(jax.custom_vjp, nondiff_argnums=(2, 3))
def mm(a, b, mode, prec):
    return _mm_raw(a, b, mode, prec)


def _mm_fwd(a, b, mode, prec):
    return _mm_raw(a, b, mode, prec), (a, b)


def _mm_bwd(mode, prec, res, ct):
    a, b = res
    pa = {"la": None, "rb": "rb"}.get(prec, prec)
    pb = {"la": "la", "rb": None}.get(prec, prec)
    if mode == "nn":
        da = _mm_raw(ct, b, "nt", pa) if pa else None
        db = _mm_raw(a, ct, "tn", pb) if pb else None
    elif mode == "nt":
        da = _mm_raw(ct, b, "nn", pa) if pa else None
        db = _mm_raw(ct, a, "tn", {"la": "rb"}.get(pb, pb)) if pb else None
    else:
        da = _mm_raw(b, ct, "nt", {"rb": "la"}.get(pa, pa)) if pa else None
        db = _mm_raw(a, ct, "nn", pb) if pb else None
    da = jnp.zeros_like(a) if da is None else da.astype(a.dtype)
    db = jnp.zeros_like(b) if db is None else db.astype(b.dtype)
    return da, db


mm.defvjp(_mm_fwd, _mm_bwd)


def _shift_rows(x, j):
    n = x.shape[0]
    row = lax.broadcasted_iota(jnp.int32, x.shape, 0)
    if j > 0:
        return jnp.where(row >= j, pltpu.roll(x, j, 0), 0.0)
    return jnp.where(row < n + j, pltpu.roll(x, n + j, 0), 0.0)


@functools.partial(jax.custom_vjp, nondiff_argnums=(1,))
def shift(x, j):
    return _shift_rows(x, j)


shift.defvjp(lambda x, j: (_shift_rows(x, j), None), lambda j, _, ct: (_shift_rows(ct, -j),))


def _sigmoid(x):
    return 1.0 / (1.0 + jnp.exp(-x))


def _silu(x):
    return x * _sigmoid(x)


def _softplus(x):
    return jnp.maximum(x, 0.0) + jnp.log(1.0 + jnp.exp(-jnp.abs(x)))


def _logsig(x):
    return jnp.minimum(x, 0.0) - jnp.log(1.0 + jnp.exp(-jnp.abs(x)))


def _gelu(x):
    return 0.5 * x * (1.0 + lax.erf(x * (2.0 ** -0.5)))


def _rms(x, g):
    return x * lax.rsqrt(jnp.mean(x * x, axis=-1, keepdims=True) + EPS) * g


def _iota2(shape, dim):
    return lax.broadcasted_iota(jnp.int32, shape, dim)


class GatherJob:
    def __init__(self, x):
        self.inputs = [x]
        self.out_shapes = [jax.ShapeDtypeStruct((N_DEV,) + x.shape, x.dtype)]
        self.sems = [pltpu.SemaphoreType.DMA((7,)), pltpu.SemaphoreType.DMA((7,)), pltpu.SemaphoreType.DMA(())]
        self.results = None

    @staticmethod
    def _plan(ins, outs, sems):
        (x_ref,), (out_ref,), (send_sems, recv_sems, local_sem) = ins, outs, sems
        ix, iy, ic = lax.axis_index("x"), lax.axis_index("y"), lax.axis_index("c")
        me, sibling = (ix, iy, ic), (ix, iy, 1 - ic)
        chips = [(1 - ix, iy), (ix, 1 - iy), (1 - ix, 1 - iy)]

        def slot(px, py, pc):
            return out_ref.at[4 * px + 2 * py + pc]

        def copy(k, block, to, src=None):
            return pltpu.make_async_remote_copy(
                src_ref=slot(*block) if src is None else src, dst_ref=slot(*block),
                send_sem=send_sems.at[k], recv_sem=recv_sems.at[k], device_id=to, device_id_type=_MESH)

        mine = pltpu.make_async_copy(x_ref, slot(*me), local_sem)
        first = [copy(0, me, sibling, src=x_ref)]
        first += [copy(1 + j, me, (*chip, ic), src=x_ref) for j, chip in enumerate(chips)]
        return ic, me, sibling, chips, copy, mine, first

    def start(self, ins, outs, sems):
        *_, mine, first = self._plan(ins, outs, sems)
        mine.start()
        for cp in first:
            cp.start()

    def finish(self, ins, outs, sems):
        ic, me, sibling, chips, copy, mine, first = self._plan(ins, outs, sems)
        passed = [copy(4 + j, (*chip, ic), sibling) for j, chip in enumerate(chips)]
        for j, chip in enumerate(chips):
            copy(1 + j, (*chip, ic), me).wait_recv()
            passed[j].start()
        copy(0, sibling, me).wait_recv()
        for j, chip in enumerate(chips):
            copy(4 + j, (*chip, 1 - ic), me).wait_recv()
        for cp in first + passed:
            cp.wait_send()
        mine.wait()


class ChipExchangeJob:
    def __init__(self, p4):
        self.inputs = [p4]
        self.out_shapes = [jax.ShapeDtypeStruct(p4.shape, p4.dtype)]
        self.sems = [pltpu.SemaphoreType.DMA((3,)), pltpu.SemaphoreType.DMA((3,)), pltpu.SemaphoreType.DMA(())]
        self.results = None

    @staticmethod
    def _plan(ins, outs, sems):
        (p_ref,), (r_ref,), (send_sems, recv_sems, local_sem) = ins, outs, sems
        ix, iy, ic = lax.axis_index("x"), lax.axis_index("y"), lax.axis_index("c")
        my_xy = 2 * ix + iy
        local = pltpu.make_async_copy(p_ref.at[my_xy], r_ref.at[my_xy], local_sem)
        chips = [(1 - ix, iy), (ix, 1 - iy), (1 - ix, 1 - iy)]
        copies = [
            pltpu.make_async_remote_copy(
                src_ref=p_ref.at[2 * px + py], dst_ref=r_ref.at[my_xy],
                send_sem=send_sems.at[k], recv_sem=recv_sems.at[k],
                device_id=(px, py, ic), device_id_type=_MESH)
            for k, (px, py) in enumerate(chips)
        ]
        return local, copies

    def start(self, ins, outs, sems):
        local, copies = self._plan(ins, outs, sems)
        local.start()
        for cp in copies:
            cp.start()

    def finish(self, ins, outs, sems):
        local, copies = self._plan(ins, outs, sems)
        for cp in copies:
            cp.wait()
        local.wait()


class PairExchangeJob:
    def __init__(self, g8):
        self.inputs = [g8]
        self.out_shapes = [jax.ShapeDtypeStruct((4,) + g8.shape[1:], g8.dtype)]
        self.sems = [pltpu.SemaphoreType.DMA((4,)), pltpu.SemaphoreType.DMA((4,))]
        self.results = None

    @staticmethod
    def _plan(ins, outs, sems):
        (g_ref,), (r_ref,), (send_sems, recv_sems) = ins, outs, sems
        ix, iy, ic = lax.axis_index("x"), lax.axis_index("y"), lax.axis_index("c")
        return [
            pltpu.make_async_remote_copy(
                src_ref=g_ref.at[2 * xy + (1 - ic)], dst_ref=r_ref.at[xy],
                send_sem=send_sems.at[xy], recv_sem=recv_sems.at[xy],
                device_id=(ix, iy, 1 - ic), device_id_type=_MESH)
            for xy in range(4)
        ]

    def start(self, ins, outs, sems):
        for cp in self._plan(ins, outs, sems):
            cp.start()

    def finish(self, ins, outs, sems):
        for cp in self._plan(ins, outs, sems):
            cp.wait()


def _each_job(jobs, method, ins, outs, sems):
    i = o = s = 0
    for job in jobs:
        ni, no, ns = len(job.inputs), len(job.out_shapes), len(job.sems)
        getattr(job, method)(ins[i:i + ni], outs[o:o + no], sems[s:s + ns])
        i, o, s = i + ni, o + no, s + ns


def run_jobs(jobs, name):
    j_in = [a for job in jobs for a in job.inputs]
    j_out = [sh for job in jobs for sh in job.out_shapes]
    j_sem = [sm for job in jobs for sm in job.sems]

    def body(*refs):
        ins, outs, sems = refs[:len(j_in)], refs[len(j_in):len(j_in) + len(j_out)], refs[len(j_in) + len(j_out):]
        _each_job(jobs, "start", ins, outs, sems)
        _each_job(jobs, "finish", ins, outs, sems)

    res = pl.pallas_call(body, name=name, out_shape=j_out, in_specs=[_ANY] * len(j_in), out_specs=[_ANY] * len(j_out),
                         scratch_shapes=j_sem)(*j_in)
    _hand_out(jobs, res)


def _hand_out(jobs, res):
    o = 0
    for job in jobs:
        job.results = list(res[o:o + len(job.out_shapes)])
        o += len(job.out_shapes)


def hosted_call(body, args, *, name, grid, in_specs, out_specs, out_shape, scratch_shapes=(), sem=None, jobs=()):
    outs_l, specs_l = list(out_shape), list(out_specs)
    if not jobs:
        return pl.pallas_call(body, name=name, grid=grid, in_specs=list(in_specs), out_specs=specs_l, out_shape=outs_l,
                              scratch_shapes=list(scratch_shapes), compiler_params=_cp(*sem))(*args)
    j_in = [a for job in jobs for a in job.inputs]
    j_out = [sh for job in jobs for sh in job.out_shapes]
    j_sem = [sm for job in jobs for sm in job.sems]
    n_in, n_out, n_scr = len(in_specs), len(outs_l), len(scratch_shapes)

    def wrapped(*refs):
        pos = [0]

        def take(n):
            pos[0] += n
            return refs[pos[0] - n:pos[0]]

        ins, jin, outs, jout, scr, jsem = take(n_in), take(len(j_in)), take(n_out), take(len(j_out)), take(n_scr), take(len(j_sem))
        ids = [pl.program_id(a) for a in range(len(grid))]
        first = functools.reduce(lambda p, q: p & q, [i == 0 for i in ids])
        last = functools.reduce(lambda p, q: p & q, [i == g - 1 for i, g in zip(ids, grid)])

        @pl.when(first)
        def _():
            _each_job(jobs, "start", jin, jout, jsem)

        body(*ins, *outs, *scr)

        @pl.when(last)
        def _():
            _each_job(jobs, "finish", jin, jout, jsem)

    res = pl.pallas_call(
        wrapped, name=name, grid=grid, in_specs=list(in_specs) + [_ANY] * len(j_in),
        out_specs=specs_l + [_ANY] * len(j_out), out_shape=outs_l + j_out,
        scratch_shapes=list(scratch_shapes) + j_sem, compiler_params=_cp(*["arbitrary"] * len(grid)),
    )(*args, *j_in)
    _hand_out(jobs, res[n_out:])
    return list(res[:n_out])


def pair_sum(g8, r4, c_idx, name):
    _, rows, cols = g8.shape
    tr, tc = _tile_2d(rows, cols)

    def body(c_ref, g_ref, r_ref, o_ref):
        o_ref[...] = (g_ref[...].astype(F32) + r_ref[...].astype(F32)).astype(o_ref.dtype)

    grid_spec = pltpu.PrefetchScalarGridSpec(
        num_scalar_prefetch=1, grid=(4, rows // tr, cols // tc),
        in_specs=[pl.BlockSpec((None, tr, tc), lambda s, i, j, c: (2 * s + c[0], i, j)),
                  pl.BlockSpec((None, tr, tc), lambda s, i, j, c: (s, i, j))],
        out_specs=pl.BlockSpec((None, tr, tc), lambda s, i, j, c: (s, i, j)))
    return pl.pallas_call(
        body, name=name, grid_spec=grid_spec, out_shape=jax.ShapeDtypeStruct((4, rows, cols), g8.dtype),
        compiler_params=_cp("parallel", "parallel", "parallel"),
    )(c_idx, g8, r4)


def _tile_2d(rows, cols):
    budget = 128 * 2048
    tr, tc = rows, cols
    if rows % 16 == 0:
        while tr * cols > budget and tr % 32 == 0:
            tr //= 2
    else:
        while rows * tc > budget and tc % (2 * LANE) == 0:
            tc //= 2
    return tr, tc


def _adam_math(w, g, m, v):
    m2 = ADAM_B1 * m + (1.0 - ADAM_B1) * g
    v2 = ADAM_B2 * v + (1.0 - ADAM_B2) * (g * g)
    m_hat = m2 / (1.0 - ADAM_B1 ** ADAM_STEP)
    v_hat = v2 / (1.0 - ADAM_B2 ** ADAM_STEP)
    delta = -ADAM_LR * (m_hat / (jnp.sqrt(v_hat) + ADAM_EPS) + ADAM_WD * w)
    return delta, m2, v2


def adam_layer(parts, w, m, v, bufs, layer, name):
    n_parts, rows, cols = parts.shape
    tr, tc = _tile_2d(rows, cols)

    def body(p_ref, w_ref, m_ref, v_ref, g_in, d_in, m_in, v_in, g_out, d_out, m_out, v_out):
        g = p_ref[0].astype(F32)
        for k in range(1, n_parts):
            g = g + p_ref[k].astype(F32)
        delta, m2, v2 = _adam_math(w_ref[...], g, m_ref[...], v_ref[...])
        g_out[...] = g
        d_out[...] = delta
        m_out[...] = m2
        v_out[...] = v2

    lay = pl.BlockSpec((None, tr, tc), lambda i, j: (layer, i, j))
    return pl.pallas_call(
        body, name=name, grid=(rows // tr, cols // tc),
        in_specs=[pl.BlockSpec((n_parts, tr, tc), lambda i, j: (0, i, j)), lay, lay, lay, _ANY, _ANY, _ANY, _ANY],
        out_specs=[lay, lay, lay, lay],
        out_shape=[jax.ShapeDtypeStruct(w.shape, F32)] * 4,
        input_output_aliases={4: 0, 5: 1, 6: 2, 7: 3},
        compiler_params=_cp("parallel", "parallel"),
    )(parts, w, m, v, *bufs)


def matmul(a, b, mode, out_dtype, name, res=None, caps=(1024, 1024, 2048), b_view=None, out_slabs=None, jobs=(), post=None):
    if mode == "tn":
        k_dim, m_dim = a.shape
    else:
        m_dim, k_dim = a.shape
    if b_view is None:
        b_rows, b_cols = b.shape
    else:
        kind = b_view
        shard_r, shard_c = b.shape[1:]
        b_rows, b_cols = (shard_r, N_DEV * shard_c) if kind == "cols" else (N_DEV * shard_r, shard_c)
    n_dim = b_rows if mode == "nt" else b_cols
    cap_n, cap_k = caps[1], caps[2]
    tn = _tile(out_slabs, cap_n) if out_slabs else _tile(n_dim, cap_n)
    tk = _tile(k_dim, cap_k)
    stacked = 0
    if b_view is not None:
        along_n = (kind == "cols") == (mode == "nn")
        shard_len = shard_c if kind == "cols" else shard_r
        if along_n:
            tn = _tile(shard_len, cap_n)
        elif kind == "rows" and tk >= shard_r and shard_r % 16 == 0:
            stacked = tk // shard_r
        else:
            tk = _tile(shard_len, cap_k)
    tm = _tile(m_dim, caps[0])
    nk = k_dim // tk

    def body(*refs):
        a_ref, b_ref = refs[:2]
        r_ref = refs[2] if res is not None else None
        o_idx = 3 if res is not None else 2
        o_ref = refs[o_idx]

        def emit(r):
            if post == "sqrelu_bwd":
                r = r * (2.0 * jnp.maximum(r_ref[...], 0.0))
            elif res is not None:
                r = r + r_ref[...]
            o_ref[...] = r.astype(out_dtype)
            if post == "sqrelu":
                act = jnp.maximum(r, 0.0)
                refs[o_idx + 1][...] = (act * act).astype(MXU)

        b_val = b_ref[...].reshape(tk, tn) if stacked else b_ref[...]
        part = lax.dot_general(a_ref[...], b_val, _DN[mode], preferred_element_type=F32)
        if nk == 1:
            emit(part)
            return
        acc = refs[-1]
        k = pl.program_id(2)

        @pl.when(k == 0)
        def _():
            acc[...] = part

        @pl.when(k > 0)
        def _():
            acc[...] += part

        @pl.when(k == nk - 1)
        def _():
            emit(acc[...])

    a_spec = pl.BlockSpec((tk, tm), lambda i, j, k: (k, i)) if mode == "tn" else pl.BlockSpec((tm, tk), lambda i, j, k: (i, k))
    b_blk = (tn, tk) if mode == "nt" else (tk, tn)
    b_pos = (lambda i, j, k: (j, k)) if mode == "nt" else (lambda i, j, k: (k, j))
    if b_view is None:
        b_spec = pl.BlockSpec(b_blk, b_pos)
    elif stacked:
        b_spec = pl.BlockSpec((stacked, shard_r, tn), lambda i, j, k: (k, 0, j))
    elif kind == "cols":
        per = shard_c // b_blk[1]
        b_spec = pl.BlockSpec((None,) + b_blk, lambda i, j, k: (b_pos(i, j, k)[1] // per, b_pos(i, j, k)[0], b_pos(i, j, k)[1] % per))
    else:
        per = shard_r // b_blk[0]
        b_spec = pl.BlockSpec((None,) + b_blk, lambda i, j, k: (b_pos(i, j, k)[0] // per, b_pos(i, j, k)[0] % per, b_pos(i, j, k)[1]))
    if out_slabs:
        per_o = out_slabs // tn
        o_spec = pl.BlockSpec((None, tm, tn), lambda i, j, k: (j // per_o, i, j % per_o))
        out_shape = jax.ShapeDtypeStruct((n_dim // out_slabs, m_dim, out_slabs), out_dtype)
    else:
        o_spec = pl.BlockSpec((tm, tn), lambda i, j, k: (i, j))
        out_shape = jax.ShapeDtypeStruct((m_dim, n_dim), out_dtype)
    in_specs, args = [a_spec, b_spec], [a, b]
    if res is not None:
        in_specs.append(o_spec)
        args.append(res)
    out_specs, out_shapes = [o_spec], [out_shape]
    if post == "sqrelu":
        out_specs.append(o_spec)
        out_shapes.append(jax.ShapeDtypeStruct(out_shape.shape, MXU))
    outs = hosted_call(
        body, args, name=name, grid=(m_dim // tm, n_dim // tn, nk), in_specs=in_specs, out_specs=out_specs,
        out_shape=out_shapes, scratch_shapes=[pltpu.VMEM((tm, tn), F32)] if nk > 1 else [],
        sem=("parallel", "parallel", "arbitrary"), jobs=jobs)
    return tuple(outs) if post == "sqrelu" else outs[0]


def rms_fwd(x, gain, name):
    t, d = x.shape
    tt = _tile(t, 256)

    def body(x_ref, g_ref, o_ref):
        o_ref[...] = _rms(x_ref[...], g_ref[...]).astype(o_ref.dtype)

    return pl.pallas_call(
        body, name=name, grid=(t // tt,),
        in_specs=[pl.BlockSpec((tt, d), lambda i: (i, 0)), pl.BlockSpec((1, d), lambda i: (0, 0))],
        out_specs=pl.BlockSpec((tt, d), lambda i: (i, 0)),
        out_shape=jax.ShapeDtypeStruct((t, d), MXU), compiler_params=_cp("parallel"),
    )(x, gain)


def rms_bwd(x, gain, dh, dres, name):
    t, d = x.shape
    tt = _tile(t, 256)

    def body(x_ref, g_ref, dh_ref, dr_ref, dx_ref, dxb_ref, dg_ref):
        _, vjp = jax.vjp(_rms, x_ref[...], g_ref[...])
        dx, dg = vjp(dh_ref[...])
        dx = dx + dr_ref[...]
        dx_ref[...] = dx
        dxb_ref[...] = dx.astype(dxb_ref.dtype)

        @pl.when(pl.program_id(0) == 0)
        def _():
            dg_ref[...] = jnp.zeros_like(dg_ref)

        dg_ref[...] += dg

    row = pl.BlockSpec((tt, d), lambda i: (i, 0))
    vec = pl.BlockSpec((1, d), lambda i: (0, 0))
    return pl.pallas_call(
        body, name=name, grid=(t // tt,), in_specs=[row, vec, row, row], out_specs=[row, row, vec],
        out_shape=[jax.ShapeDtypeStruct((t, d), F32), jax.ShapeDtypeStruct((t, d), MXU), jax.ShapeDtypeStruct((1, d), F32)],
        compiler_params=_cp("arbitrary"),
    )(x, gain, dh, dres)


def loss_head(x, target, name):
    t, d = x.shape
    tt = _tile(t, 256)

    def body(x_ref, t_ref, dx_ref, dxb_ref, l_ref):
        e = x_ref[...] - t_ref[...]
        dx = e * (1.0 / d)
        dx_ref[...] = dx
        dxb_ref[...] = dx.astype(dxb_ref.dtype)

        @pl.when(pl.program_id(0) == 0)
        def _():
            l_ref[...] = jnp.zeros_like(l_ref)

        part = jnp.sum(jnp.sum(e * e, axis=-1, keepdims=True) * (1.0 / d), axis=0, keepdims=True)
        l_ref[...] += 0.5 * part

    row = pl.BlockSpec((tt, d), lambda i: (i, 0))
    return pl.pallas_call(
        body, name=name, grid=(t // tt,), in_specs=[row, row],
        out_specs=[row, row, pl.BlockSpec((8, LANE), lambda i: (0, 0))],
        out_shape=[jax.ShapeDtypeStruct((t, d), F32), jax.ShapeDtypeStruct((t, d), MXU), jax.ShapeDtypeStruct((8, LANE), F32)],
        compiler_params=_cp("arbitrary"),
    )(x, target)


def _merge_f(g0, g1, g2, ba, bb, bc):
    return _sigmoid(g0) * ba + _sigmoid(g1) * bb + _sigmoid(g2) * bc


def merge_fwd(z, off, ba, bb, bc, name):
    t, d = ba.shape
    tt, td = _tile(t, 256), _tile(d // 2, 1024)
    nd, ob = d // td, off // td

    def body(g0, g1, g2, a, b, c, o_ref):
        o_ref[...] = _merge_f(g0[...], g1[...], g2[...], a[...], b[...], c[...]).astype(o_ref.dtype)

    gates = [pl.BlockSpec((tt, td), functools.partial(lambda i, j, s: (i, ob + s * nd + j), s=s)) for s in range(3)]
    blk = pl.BlockSpec((tt, td), lambda i, j: (i, j))
    return pl.pallas_call(body, name=name, grid=(t // tt, nd), in_specs=gates + [blk] * 3, out_specs=blk,
                          out_shape=jax.ShapeDtypeStruct((t, d), MXU), compiler_params=_cp("parallel", "parallel"))(z, z, z, ba, bb, bc)


def merge_bwd(z, off, ba, bb, bc, dy, name):
    t, d = ba.shape
    tt, td = _tile(t, 256), _tile(d // 2, 1024)
    nd, ob = d // td, off // td

    def body(g0, g1, g2, a, b, c, dy_ref, dgl, da, db, dc):
        _, vjp = jax.vjp(_merge_f, g0[...], g1[...], g2[...], a[...], b[...], c[...])
        d0, d1, d2, xa, xb, xc = vjp(dy_ref[...])
        for s, dv in enumerate((d0, d1, d2)):
            dgl[s] = dv.astype(dgl.dtype)
        da[...] = xa.astype(da.dtype)
        db[...] = xb.astype(db.dtype)
        dc[...] = xc.astype(dc.dtype)

    gates = [pl.BlockSpec((tt, td), functools.partial(lambda i, j, s: (i, ob + s * nd + j), s=s)) for s in range(3)]
    blk = pl.BlockSpec((tt, td), lambda i, j: (i, j))
    dgl, da, db, dc = pl.pallas_call(
        body, name=name, grid=(t // tt, nd), in_specs=gates + [blk] * 4,
        out_specs=[pl.BlockSpec((3, tt, td), lambda i, j: (0, i, j)), blk, blk, blk],
        out_shape=[jax.ShapeDtypeStruct((3, t, d), MXU)] + [jax.ShapeDtypeStruct((t, d), MXU)] * 3,
        compiler_params=_cp("parallel", "parallel"),
    )(z, z, z, ba, bb, bc, dy)
    return dgl, da, db, dc


def _gdn_pre_f(qp, kp, vp, ab, cq, ck, cv, alog, dtb, h, n_heads):
    def conv(xp, cw):
        acc = xp * cw[CONV_K - 1]
        for j in range(1, CONV_K):
            acc = acc + shift(xp, j) * cw[CONV_K - 1 - j]
        return _silu(acc)

    q, k, v = conv(qp, cq), conv(kp, ck), conv(vp, cv)
    q = q * lax.rsqrt(jnp.sum(q * q, axis=-1, keepdims=True) + EPS) * (HD ** -0.5)
    k = k * lax.rsqrt(jnp.sum(k * k, axis=-1, keepdims=True) + EPS)
    lane = _iota2(ab.shape, 1)
    a_col = jnp.sum(jnp.where(lane == h, ab, 0.0), axis=-1, keepdims=True)
    b_col = jnp.sum(jnp.where(lane == n_heads + h, ab, 0.0), axis=-1, keepdims=True)
    lane1 = _iota2(alog.shape, 1)
    al = jnp.sum(jnp.where(lane1 == h, alog, 0.0), axis=-1, keepdims=True)
    dt = jnp.sum(jnp.where(lane1 == h, dtb, 0.0), axis=-1, keepdims=True)
    g = -jnp.exp(al) * _softplus(a_col + dt)
    return q, k, v, g, _sigmoid(b_col)


def _gdn_pre_specs(t, n_heads, ab_blk):
    zq = [pl.BlockSpec((t, HD), functools.partial(lambda h, s: (0, s * n_heads + h), s=s)) for s in range(3)]
    ab = pl.BlockSpec((t, LANE), lambda h: (0, ab_blk))
    cw = [pl.BlockSpec((CONV_K, HD), functools.partial(lambda h, s: (0, s * n_heads + h), s=s)) for s in range(3)]
    vec = pl.BlockSpec((1, LANE), lambda h: (0, 0))
    return zq, ab, cw, vec


def gdn_pre_fwd(z, ab_blk, conv_w, alog, dtb, n_heads, name):
    t = z.shape[0]
    zq, ab, cw, vec = _gdn_pre_specs(t, n_heads, ab_blk)

    def body(qp, kp, vp, ab_ref, cq, ck, cv, al, dt, q_o, k_o, v_o, g_o, b_o):
        rows = lambda r: tuple(r[j:j + 1, :] for j in range(CONV_K))
        outs = _gdn_pre_f(qp[...], kp[...], vp[...], ab_ref[...], rows(cq), rows(ck), rows(cv), al[...], dt[...],
                          pl.program_id(0), n_heads)
        for o_ref, val in zip((q_o, k_o, v_o, g_o, b_o), outs):
            o_ref[...] = val

    head = pl.BlockSpec((None, t, HD), lambda h: (h, 0, 0))
    col = pl.BlockSpec((None, t, 1), lambda h: (h, 0, 0))
    return pl.pallas_call(
        body, name=name, grid=(n_heads,), in_specs=zq + [ab] + cw + [vec, vec], out_specs=[head] * 3 + [col] * 2,
        out_shape=[jax.ShapeDtypeStruct((n_heads, t, HD), F32)] * 3 + [jax.ShapeDtypeStruct((n_heads, t, 1), F32)] * 2,
        compiler_params=_cp("parallel"),
    )(z, z, z, z, conv_w, conv_w, conv_w, alog, dtb)


def gdn_pre_bwd(z, ab_blk, conv_w, alog, dtb, n_heads, dq, dk, dv, dg, db, name):
    t = z.shape[0]
    gw = n_heads * HD
    zq, ab, cw, vec = _gdn_pre_specs(t, n_heads, ab_blk)

    def body(qp, kp, vp, ab_ref, cq, ck, cv, al, dt, dq_r, dk_r, dv_r, dg_r, db_r,
             dqp, dkp, dvp, dab, dcq, dck, dcv, dal, ddt):
        h = pl.program_id(0)
        rows = lambda r: tuple(r[j:j + 1, :] for j in range(CONV_K))
        f = functools.partial(_gdn_pre_f, h=h, n_heads=n_heads)
        _, vjp = jax.vjp(f, qp[...], kp[...], vp[...], ab_ref[...], rows(cq), rows(ck), rows(cv), al[...], dt[...])
        gq, gk, gv, gab, gcq, gck, gcv, gal, gdt = vjp((dq_r[...], dk_r[...], dv_r[...], dg_r[...], db_r[...]))
        dqp[...] = gq.astype(dqp.dtype)
        dkp[...] = gk.astype(dkp.dtype)
        dvp[...] = gv.astype(dvp.dtype)
        for ref, gr in ((dcq, gcq), (dck, gck), (dcv, gcv)):
            for j in range(CONV_K):
                ref[j:j + 1, :] = gr[j]

        @pl.when(h == 0)
        def _():
            dab[...] = jnp.zeros_like(dab)
            dal[...] = jnp.zeros_like(dal)
            ddt[...] = jnp.zeros_like(ddt)

        dab[...] += gab
        dal[...] += gal
        ddt[...] += gdt

    head = pl.BlockSpec((None, t, HD), lambda h: (h, 0, 0))
    col = pl.BlockSpec((None, t, 1), lambda h: (h, 0, 0))
    seg = pl.BlockSpec((t, HD), lambda h: (0, h))
    cseg = pl.BlockSpec((CONV_K, HD), lambda h: (0, h))
    return pl.pallas_call(
        body, name=name, grid=(n_heads,),
        in_specs=zq + [ab] + cw + [vec, vec] + [head] * 3 + [col] * 2,
        out_specs=[seg] * 3 + [pl.BlockSpec((t, LANE), lambda h: (0, 0))] + [cseg] * 3 + [vec, vec],
        out_shape=[jax.ShapeDtypeStruct((t, gw), MXU)] * 3 + [jax.ShapeDtypeStruct((t, LANE), F32)]
        + [jax.ShapeDtypeStruct((CONV_K, gw), F32)] * 3 + [jax.ShapeDtypeStruct((1, LANE), F32)] * 2,
        compiler_params=_cp("arbitrary"),
    )(z, z, z, z, conv_w, conv_w, conv_w, alog, dtb, dq, dk, dv, dg, db)


def _gdn_chunk_f(q, k, v, g, b):
    c = CHUNK
    r, s = _iota2((c, c), 0), _iota2((c, c), 1)
    tril = (s <= r).astype(F32)
    gc_w = mm(tril, jnp.broadcast_to(g, (c, HD)), "nn", "la")
    gc_i = mm(tril, jnp.broadcast_to(g, (c, c)), "nn", "la")
    gc_j = mm(jnp.ones((c, c), F32), jnp.where(r <= s, jnp.broadcast_to(g, (c, c)), 0.0), "nn", "la")
    decay = jnp.exp(jnp.where(s <= r, gc_i - gc_j, -1e30))
    kb = k * b
    low = jnp.where(s < r, mm(kb, k, "nt", "x3") * decay, 0.0)
    inv = jnp.where(r == s, 1.0, 0.0) - low
    pw = mm(low, low, "nn", "x3")
    n_sq = 1
    while 2 * n_sq < c:
        inv = inv + mm(inv, pw, "nn", "x3")
        n_sq *= 2
        if 2 * n_sq < c:
            pw = mm(pw, pw, "nn", "x3")
    egc = jnp.exp(gc_w)
    u = mm(inv, v * b, "nn", "x3")
    w = mm(inv, kb * egc, "nn", "x3")
    intra = mm(q, k, "nt", "lo") * decay
    g_last = jnp.sum(g, axis=0, keepdims=True)
    kd = k * jnp.exp(g_last - gc_w)
    egl = jnp.exp(jnp.broadcast_to(g_last, (1, HD)))
    return u, w, intra, q * egc, kd, egl


def _group(n, cap=4):
    return max(g for g in range(1, cap + 1) if n % g == 0)


def _chunk_specs(nb_h, nb_c, n_chunks=None):
    cn = (lambda n: n) if n_chunks is None else (lambda n: n_chunks // nb_c - 1 - n)
    rows = nb_c * CHUNK
    vec = pl.BlockSpec((nb_h, rows, HD), lambda h, n: (h, cn(n), 0))
    col = pl.BlockSpec((nb_h, rows, 1), lambda h, n: (h, cn(n), 0))
    sq = pl.BlockSpec((nb_h, rows, CHUNK), lambda h, n: (h, cn(n), 0))
    one = pl.BlockSpec((nb_h, nb_c, 1, HD), lambda h, n: (h, cn(n), 0, 0))
    st = pl.BlockSpec((nb_h, nb_c, HD, HD), lambda h, n: (h, cn(n), 0, 0))
    return vec, col, sq, one, st


def _chunk_shapes(n_heads, t):
    vec = jax.ShapeDtypeStruct((n_heads, t, HD), F32)
    return [vec, vec, jax.ShapeDtypeStruct((n_heads, t, CHUNK), F32), vec, vec,
            jax.ShapeDtypeStruct((n_heads, t // CHUNK, 1, HD), F32)]


def _chunk_rows(ci):
    return slice(ci * CHUNK, (ci + 1) * CHUNK)


def gdn_chunk_fwd(q, k, v, g, b, name, jobs=()):
    n_heads, t, _ = q.shape
    nb_c = _group(t // CHUNK)
    vec, col, sq, one, _ = _chunk_specs(1, nb_c)

    def body(q_r, k_r, v_r, g_r, b_r, *outs):
        for ci in range(nb_c):
            rows = _chunk_rows(ci)
            vals = _gdn_chunk_f(*(r[0, rows, :] for r in (q_r, k_r, v_r, g_r, b_r)))
            for o_ref, val in zip(outs[:5], vals[:5]):
                o_ref[0, rows, :] = val
            outs[5][0, ci] = vals[5]

    return hosted_call(
        body, (q, k, v, g, b), name=name, grid=(n_heads, t // CHUNK // nb_c), in_specs=[vec] * 3 + [col] * 2,
        out_specs=[vec, vec, sq, vec, vec, one], out_shape=_chunk_shapes(n_heads, t),
        sem=("parallel", "parallel"), jobs=jobs)


def gdn_chunk_bwd(q, k, v, g, b, cts, name, jobs=()):
    n_heads, t, _ = q.shape
    nb_c = _group(t // CHUNK)
    vec, col, sq, one, _ = _chunk_specs(1, nb_c)

    def body(q_r, k_r, v_r, g_r, b_r, du, dw, di, dqd, dkd, degl, dq, dk, dv, dg, db):
        for ci in range(nb_c):
            rows = _chunk_rows(ci)
            _, vjp = jax.vjp(_gdn_chunk_f, *(r[0, rows, :] for r in (q_r, k_r, v_r, g_r, b_r)))
            grads = vjp(tuple(r[0, rows, :] for r in (du, dw, di, dqd, dkd)) + (degl[0, ci],))
            for o_ref, val in zip((dq, dk, dv, dg, db), grads):
                o_ref[0, rows, :] = val

    col_shape = jax.ShapeDtypeStruct((n_heads, t, 1), F32)
    return hosted_call(
        body, (q, k, v, g, b, *cts), name=name, grid=(n_heads, t // CHUNK // nb_c),
        in_specs=[vec] * 3 + [col] * 2 + [vec, vec, sq, vec, vec, one],
        out_specs=[vec] * 3 + [col] * 2,
        out_shape=[jax.ShapeDtypeStruct((n_heads, t, HD), F32)] * 3 + [col_shape] * 2,
        sem=("parallel", "parallel"), jobs=jobs)


def _scan_f(s, u, w, a, qd, kd, egl):
    vn = u - mm(w, s, "nn", "lo")
    o = mm(qd, s, "nn", "lo") + mm(a, vn, "nn", "lo")
    return o, s * egl + mm(kd, vn, "tn", "lo")


def gdn_scan_fwd(chunks, name):
    u = chunks[0]
    n_heads, t, _ = u.shape
    nc = t // CHUNK
    nb_h = _group(n_heads, 8)
    vec, _, sq, one, st = _chunk_specs(nb_h, 1)

    def body(u_r, w_r, a_r, qd_r, kd_r, e_r, o_ref, s_ref, state):
        @pl.when(pl.program_id(1) == 0)
        def _():
            state[...] = jnp.zeros_like(state)

        for hh in range(nb_h):
            s = state[hh]
            s_ref[hh, 0] = s
            o, s2 = _scan_f(s, u_r[hh], w_r[hh], a_r[hh], qd_r[hh], kd_r[hh], e_r[hh, 0])
            o_ref[hh] = o
            state[hh] = s2

    return pl.pallas_call(
        body, name=name, grid=(n_heads // nb_h, nc), in_specs=[vec, vec, sq, vec, vec, one], out_specs=[vec, st],
        out_shape=[jax.ShapeDtypeStruct((n_heads, t, HD), F32), jax.ShapeDtypeStruct((n_heads, nc, HD, HD), F32)],
        scratch_shapes=[pltpu.VMEM((nb_h, HD, HD), F32)], compiler_params=_cp("parallel", "arbitrary"),
    )(*chunks)


def gdn_scan_bwd(chunks, states, do, name):
    n_heads, t, _ = do.shape
    nc = t // CHUNK
    nb_h = _group(n_heads, 8)
    vec, _, sq, one, st = _chunk_specs(nb_h, 1, n_chunks=nc)

    def body(u_r, w_r, a_r, qd_r, kd_r, e_r, s_r, do_r, du, dw, da, dqd, dkd, de, dstate):
        @pl.when(pl.program_id(1) == 0)
        def _():
            dstate[...] = jnp.zeros_like(dstate)

        for hh in range(nb_h):
            _, vjp = jax.vjp(_scan_f, s_r[hh, 0], u_r[hh], w_r[hh], a_r[hh], qd_r[hh], kd_r[hh], e_r[hh, 0])
            grads = vjp((do_r[hh], dstate[hh]))
            dstate[hh] = grads[0]
            for o_ref, val in zip((du, dw, da, dqd, dkd), grads[1:6]):
                o_ref[hh] = val
            de[hh, 0] = grads[6]

    return pl.pallas_call(
        body, name=name, grid=(n_heads // nb_h, nc), in_specs=[vec, vec, sq, vec, vec, one, st, vec],
        out_specs=[vec, vec, sq, vec, vec, one], out_shape=_chunk_shapes(n_heads, t),
        scratch_shapes=[pltpu.VMEM((nb_h, HD, HD), F32)], compiler_params=_cp("parallel", "arbitrary"),
    )(*chunks, states, do)


def _post_f(o, gate, gain):
    return _rms(o, gain) * _silu(gate)


def gdn_post_fwd(o, z, gate_blk, gain, name):
    n_heads, t, _ = o.shape
    tt = _tile(t, 512)

    def body(o_r, gt_r, gn_r, out):
        out[...] = _post_f(o_r[...], gt_r[...], gn_r[...]).astype(out.dtype)

    return pl.pallas_call(
        body, name=name, grid=(n_heads, t // tt),
        in_specs=[pl.BlockSpec((None, tt, HD), lambda h, i: (h, i, 0)), pl.BlockSpec((tt, HD), lambda h, i: (i, gate_blk + h)),
                  pl.BlockSpec((1, HD), lambda h, i: (0, 0))],
        out_specs=pl.BlockSpec((tt, HD), lambda h, i: (i, h)),
        out_shape=jax.ShapeDtypeStruct((t, n_heads * HD), MXU), compiler_params=_cp("parallel", "parallel"),
    )(o, z, gain)


def gdn_post_bwd(o, z, gate_blk, gain, doa, name):
    n_heads, t, _ = o.shape
    tt = _tile(t, 512)

    def body(o_r, gt_r, gn_r, d_r, do_ref, dgt_ref, dgn_ref):
        _, vjp = jax.vjp(_post_f, o_r[...], gt_r[...], gn_r[...])
        go, ggt, ggn = vjp(d_r[...])
        do_ref[...] = go
        dgt_ref[...] = ggt.astype(dgt_ref.dtype)

        @pl.when((pl.program_id(0) == 0) & (pl.program_id(1) == 0))
        def _():
            dgn_ref[...] = jnp.zeros_like(dgn_ref)

        dgn_ref[...] += ggn

    tok = pl.BlockSpec((tt, HD), lambda h, i: (i, h))
    vec = pl.BlockSpec((1, HD), lambda h, i: (0, 0))
    head = pl.BlockSpec((None, tt, HD), lambda h, i: (h, i, 0))
    return pl.pallas_call(
        body, name=name, grid=(n_heads, t // tt),
        in_specs=[head, pl.BlockSpec((tt, HD), lambda h, i: (i, gate_blk + h)), vec, tok],
        out_specs=[head, tok, vec],
        out_shape=[jax.ShapeDtypeStruct((n_heads, t, HD), F32), jax.ShapeDtypeStruct((t, n_heads * HD), MXU),
                   jax.ShapeDtypeStruct((1, HD), F32)],
        compiler_params=_cp("arbitrary", "arbitrary"),
    )(o, z, gain, doa)


def _gmlp_f(ups, vps, lngs, wss, bcols):
    n_groups = len(ups)
    width = HD * n_groups
    us = [_gelu(a) for a in ups]
    vs = [_gelu(a) for a in vps]
    mu = sum(jnp.sum(a, axis=-1, keepdims=True) for a in vs) * (1.0 / width)
    xcs = [a - mu for a in vs]
    var = sum(jnp.sum(a * a, axis=-1, keepdims=True) for a in xcs) * (1.0 / width)
    rstd = lax.rsqrt(var + EPS)
    r, s = _iota2((HD, HD), 0), _iota2((HD, HD), 1)
    causal = (s // CHUNK) <= (r // CHUNK)
    outs = []
    for gi in range(n_groups):
        vb = xcs[gi] * rstd * lngs[gi]
        sp = mm(jnp.where(causal, wss[gi], 0.0), vb, "nn", "lo") + bcols[gi]
        outs.append(us[gi] * sp)
    return tuple(outs)


def _gmlp_load(uv_u, uv_v, lng, ws, bt, n_groups):
    seg = lambda ref, gi: ref[:, gi * HD:(gi + 1) * HD]
    return ([seg(uv_u, gi) for gi in range(n_groups)], [seg(uv_v, gi) for gi in range(n_groups)],
            [seg(lng, gi) for gi in range(n_groups)], [ws[gi] for gi in range(n_groups)],
            [bt[:, gi:gi + 1] for gi in range(n_groups)])


def _gmlp_specs(width, u_blk, n_groups):
    u = pl.BlockSpec((HD, width), lambda i: (i, u_blk))
    v = pl.BlockSpec((HD, width), lambda i: (i, u_blk + 1))
    lng = pl.BlockSpec((1, width), lambda i: (0, 0))
    ws = pl.BlockSpec((n_groups, HD, HD), lambda i: (0, 0, 0))
    bt = pl.BlockSpec((HD, LANE), lambda i: (0, 0))
    return u, v, lng, ws, bt


def gmlp_fwd(z, uv_off, width, lng, ws, bt, name):
    t = z.shape[0]
    n_groups = width // HD
    specs = _gmlp_specs(width, uv_off // width, n_groups)

    def body(u_r, v_r, l_r, w_r, b_r, out):
        outs = _gmlp_f(*_gmlp_load(u_r, v_r, l_r, w_r, b_r, n_groups))
        for gi in range(n_groups):
            out[:, gi * HD:(gi + 1) * HD] = outs[gi].astype(out.dtype)

    return pl.pallas_call(
        body, name=name, grid=(t // HD,), in_specs=list(specs), out_specs=pl.BlockSpec((HD, width), lambda i: (i, 0)),
        out_shape=jax.ShapeDtypeStruct((t, width), MXU), compiler_params=_cp("parallel"),
    )(z, z, lng, ws, bt)


def gmlp_bwd(z, uv_off, width, lng, ws, bt, dob, name):
    t = z.shape[0]
    n_groups = width // HD
    specs = _gmlp_specs(width, uv_off // width, n_groups)

    def body(u_r, v_r, l_r, w_r, b_r, d_r, duv, dl, dws, dbt):
        _, vjp = jax.vjp(_gmlp_f, *_gmlp_load(u_r, v_r, l_r, w_r, b_r, n_groups))
        gu, gv, gl, gw, gb = vjp(tuple(d_r[:, gi * HD:(gi + 1) * HD] for gi in range(n_groups)))

        @pl.when(pl.program_id(0) == 0)
        def _():
            dl[...] = jnp.zeros_like(dl)
            dws[...] = jnp.zeros_like(dws)
            dbt[...] = jnp.zeros_like(dbt)

        for gi in range(n_groups):
            duv[:, gi * HD:(gi + 1) * HD] = gu[gi].astype(duv.dtype)
            duv[:, width + gi * HD:width + (gi + 1) * HD] = gv[gi].astype(duv.dtype)
            dl[:, gi * HD:(gi + 1) * HD] += gl[gi]
            dws[gi] += gw[gi]
            dbt[:, gi:gi + 1] += gb[gi]

    return pl.pallas_call(
        body, name=name, grid=(t // HD,), in_specs=list(specs) + [pl.BlockSpec((HD, width), lambda i: (i, 0))],
        out_specs=[pl.BlockSpec((HD, 2 * width), lambda i: (i, 0)), specs[2], specs[3], specs[4]],
        out_shape=[jax.ShapeDtypeStruct((t, 2 * width), MXU), jax.ShapeDtypeStruct((1, width), F32),
                   jax.ShapeDtypeStruct((n_groups, HD, HD), F32), jax.ShapeDtypeStruct((HD, LANE), F32)],
        compiler_params=_cp("arbitrary"),
    )(z, z, lng, ws, bt, dob)


def _sba_pre_f(qp, kp, qg, kg):
    return _rms(qp, qg), _rms(kp, kg)


def sba_pre_fwd(z, c_blk, n_heads, qg, kg, name):
    t = z.shape[0]
    tt = _tile(t, 512)
    zs = [pl.BlockSpec((tt, HD), functools.partial(lambda h, i, s: (i, c_blk + s * n_heads + h), s=s)) for s in range(3)]
    vec = pl.BlockSpec((1, HD), lambda h, i: (0, 0))
    head = pl.BlockSpec((None, tt, HD), lambda h, i: (h, i, 0))

    def body(qp, kp, vp, qg_r, kg_r, q_o, k_o, v_o):
        q, k = _sba_pre_f(qp[...], kp[...], qg_r[...], kg_r[...])
        q_o[...] = q.astype(q_o.dtype)
        k_o[...] = k.astype(k_o.dtype)
        v_o[...] = vp[...].astype(v_o.dtype)

    return pl.pallas_call(
        body, name=name, grid=(n_heads, t // tt), in_specs=zs + [vec, vec], out_specs=[head] * 3,
        out_shape=[jax.ShapeDtypeStruct((n_heads, t, HD), MXU)] * 3, compiler_params=_cp("parallel", "parallel"),
    )(z, z, z, qg, kg)


def sba_pre_bwd(z, c_blk, n_heads, qg, kg, dq, dk, dv, name):
    t = z.shape[0]
    tt = _tile(t, 512)
    zs = [pl.BlockSpec((tt, HD), functools.partial(lambda h, i, s: (i, c_blk + s * n_heads + h), s=s)) for s in range(2)]
    vec = pl.BlockSpec((1, HD), lambda h, i: (0, 0))
    head = pl.BlockSpec((None, tt, HD), lambda h, i: (h, i, 0))
    tok = pl.BlockSpec((tt, HD), lambda h, i: (i, h))

    def body(qp, kp, qg_r, kg_r, dq_r, dk_r, dv_r, dqp, dkp, dvp, dqg, dkg):
        _, vjp = jax.vjp(_sba_pre_f, qp[...], kp[...], qg_r[...], kg_r[...])
        gq, gk, gqg, gkg = vjp((dq_r[...], dk_r[...]))
        dqp[...] = gq.astype(dqp.dtype)
        dkp[...] = gk.astype(dkp.dtype)
        dvp[...] = dv_r[...].astype(dvp.dtype)

        @pl.when((pl.program_id(0) == 0) & (pl.program_id(1) == 0))
        def _():
            dqg[...] = jnp.zeros_like(dqg)
            dkg[...] = jnp.zeros_like(dkg)

        dqg[...] += gqg
        dkg[...] += gkg

    return pl.pallas_call(
        body, name=name, grid=(n_heads, t // tt), in_specs=zs + [vec, vec] + [head] * 3,
        out_specs=[tok] * 3 + [vec, vec],
        out_shape=[jax.ShapeDtypeStruct((t, n_heads * HD), MXU)] * 3 + [jax.ShapeDtypeStruct((1, HD), F32)] * 2,
        compiler_params=_cp("arbitrary", "arbitrary"),
    )(z, z, qg, kg, dq, dk, dv)


def _sba_block(q, kj, i, j):
    zz = lax.dot_general(q, kj, _DN["nt"], preferred_element_type=F32) * (HD ** -0.5)
    ls = _logsig(zz)
    strict = (j * HD + _iota2((HD, HD), 1)) < (i * HD + _iota2((HD, HD), 0))
    return zz, ls, jnp.where(strict, ls - zz, 0.0), strict


def sba_fwd(q, k, v, name, jobs=()):
    n_heads, t, _ = q.shape
    nb_h = _group(n_heads)

    def body(q_r, k_r, v_r, o_ref, tot_ref):
        i = pl.program_id(1)
        after = (_iota2((HD, HD), 0) > _iota2((HD, HD), 1)).astype(F32)

        def step(it, carry):
            j = i - it
            rows = pl.ds(pl.multiple_of(j * HD, HD), HD)
            new = []
            for hh in range(nb_h):
                acc, cs = carry[hh]
                _, ls, lk, strict = _sba_block(q_r[hh], k_r[hh, rows, :], i, j)
                suffix = _mm_raw(lk, after, "nn", "rb") + cs
                att = jnp.where(strict, jnp.exp(ls + suffix), 0.0)
                acc = acc + _mm_raw(att, v_r[hh, rows, :], "nn", "lo")
                new.append((acc, cs + jnp.sum(lk, axis=-1, keepdims=True)))
            return tuple(new)

        init = tuple((jnp.zeros((HD, HD), F32), jnp.zeros((HD, 1), F32)) for _ in range(nb_h))
        final = lax.fori_loop(0, i + 1, step, init)
        for hh in range(nb_h):
            o_ref[:, hh * HD:(hh + 1) * HD] = final[hh][0].astype(o_ref.dtype)
            tot_ref[hh] = final[hh][1]

    full = pl.BlockSpec((nb_h, t, HD), lambda h, i: (h, 0, 0))
    return hosted_call(
        body, (q, k, v), name=name, grid=(n_heads // nb_h, t // HD),
        in_specs=[pl.BlockSpec((nb_h, HD, HD), lambda h, i: (h, i, 0)), full, full],
        out_specs=[pl.BlockSpec((HD, nb_h * HD), lambda h, i: (i, h)), pl.BlockSpec((nb_h, HD, 1), lambda h, i: (h, i, 0))],
        out_shape=[jax.ShapeDtypeStruct((t, n_heads * HD), MXU), jax.ShapeDtypeStruct((n_heads, t, 1), F32)],
        sem=("parallel", "parallel"), jobs=jobs)


def sba_bwd(q, k, v, tot, do, name, jobs=()):
    n_heads, t, _ = q.shape
    nb_h = _group(n_heads)

    def body(q_r, k_r, v_r, tot_r, do_r, dq_ref, dk_ref, dv_ref):
        i = pl.program_id(1)

        @pl.when(i == 0)
        def _():
            dk_ref[...] = jnp.zeros_like(dk_ref)
            dv_ref[...] = jnp.zeros_like(dv_ref)

        r, s = _iota2((HD, HD), 0), _iota2((HD, HD), 1)
        upto = (r <= s).astype(F32)
        before = (r < s).astype(F32)

        def step(j, carry):
            rows = pl.ds(pl.multiple_of(j * HD, HD), HD)
            new = []
            for hh in range(nb_h):
                dq, cp, cd = carry[hh]
                qb, dob = q_r[hh], do_r[:, hh * HD:(hh + 1) * HD]
                kj, vj = k_r[hh, rows, :], v_r[hh, rows, :]
                _, ls, lk, strict = _sba_block(qb, kj, i, j)
                sig = jnp.exp(ls)
                suffix = tot_r[hh] - (cp + _mm_raw(lk, upto, "nn", "rb"))
                att = jnp.where(strict, jnp.exp(ls + suffix), 0.0)
                dp = _mm_raw(dob, vj, "nt", "lo") * att
                dlk = cd + _mm_raw(dp, before, "nn", "rb")
                dz = jnp.where(strict, dp * (1.0 - sig) - dlk * sig, 0.0) * (HD ** -0.5)
                dk_ref[hh, rows, :] += _mm_raw(dz, qb, "tn", "lo")
                dv_ref[hh, rows, :] += _mm_raw(att, dob, "tn", "lo")
                new.append((dq + _mm_raw(dz, kj, "nn", "lo"), cp + jnp.sum(lk, axis=-1, keepdims=True),
                            cd + jnp.sum(dp, axis=-1, keepdims=True)))
            return tuple(new)

        zero_col = jnp.zeros((HD, 1), F32)
        final = lax.fori_loop(0, i + 1, step, tuple((jnp.zeros((HD, HD), F32), zero_col, zero_col) for _ in range(nb_h)))
        for hh in range(nb_h):
            dq_ref[hh] = final[hh][0]

    full = pl.BlockSpec((nb_h, t, HD), lambda h, i: (h, 0, 0))
    blk = pl.BlockSpec((nb_h, HD, HD), lambda h, i: (h, i, 0))
    return hosted_call(
        body, (q, k, v, tot, do), name=name, grid=(n_heads // nb_h, t // HD),
        in_specs=[blk, full, full, pl.BlockSpec((nb_h, HD, 1), lambda h, i: (h, i, 0)),
                  pl.BlockSpec((HD, nb_h * HD), lambda h, i: (i, h))],
        out_specs=[blk, full, full], out_shape=[jax.ShapeDtypeStruct((n_heads, t, HD), F32)] * 3,
        sem=("parallel", "arbitrary"), jobs=jobs)


def small_adam(parts, w, m, v, name):
    n_parts, rows, _ = parts.shape
    tr = _tile(rows, 512) if rows % LANE == 0 else rows

    def body(p_ref, w_ref, m_ref, v_ref, g_out, d_out, m_out, v_out):
        g = p_ref[0]
        for k in range(1, n_parts):
            g = g + p_ref[k]
        delta, m2, v2 = _adam_math(w_ref[...], g, m_ref[...], v_ref[...])
        g_out[...] = g
        d_out[...] = delta
        m_out[...] = m2
        v_out[...] = v2

    blk = pl.BlockSpec((tr, LANE), lambda i: (i, 0))
    return pl.pallas_call(
        body, name=name, grid=(rows // tr,),
        in_specs=[pl.BlockSpec((n_parts, tr, LANE), lambda i: (0, i, 0)), blk, blk, blk], out_specs=[blk] * 4,
        out_shape=[jax.ShapeDtypeStruct((rows, LANE), F32)] * 4, compiler_params=_cp("parallel"),
    )(parts, w, m, v)


def _pack(arrays):
    flat = jnp.concatenate([a.reshape(-1).astype(F32) for a in arrays])
    pad = (-flat.shape[0]) % (8 * LANE)
    return jnp.pad(flat, (0, pad)).reshape(-1, LANE)


def _unpack(packed, shapes):
    flat, outs, pos = packed.reshape(-1), [], 0
    for shp in shapes:
        n = 1
        for s in shp:
            n *= s
        outs.append(flat[pos:pos + n].reshape(shp))
        pos += n
    return outs


def _pad_lanes(a):
    return jnp.pad(a, ((0, 0), (0, LANE - a.shape[1])))


def kernel(x, w_in, conv_w, a_log, dt_bias, gdn_norm_g, gmlp_ln_g, w_spatial, b_spatial, sba_q_g, sba_k_g, w_out_a, w_out_b, w_out_c, w_out, norm_mix_g, norm_mlp_g, w_ff1, w_ff2, loss_target, m_w_in, m_conv_w, m_a_log, m_dt_bias, m_gdn_norm_g, m_gmlp_ln_g, m_w_spatial, m_b_spatial, m_sba_q_g, m_sba_k_g, m_w_out_a, m_w_out_b, m_w_out_c, m_w_out, m_norm_mix_g, m_norm_mlp_g, m_w_ff1, m_w_ff2, v_w_in, v_conv_w, v_a_log, v_dt_bias, v_gdn_norm_g, v_gmlp_ln_g, v_w_spatial, v_b_spatial, v_sba_q_g, v_sba_k_g, v_w_out_a, v_w_out_b, v_w_out_c, v_w_out, v_norm_mix_g, v_norm_mlp_g, v_w_ff1, v_w_ff2):
    depth = w_in.shape[0]
    _, t, d = x.shape
    n_heads = d // 256
    gw = n_heads * HD
    width = d // 2
    n_groups = width // HD
    off_gate, off_uv, off_c, off_gl = 3 * gw, 4 * gw, 4 * gw + 2 * width, 7 * gw + 2 * width
    off_ab = off_gl + 3 * d
    n_packed = off_ab + LANE
    n_in = off_ab + 2 * n_heads
    assert w_in.shape[2] * N_DEV == n_in and t % LANE == 0 and d % 256 == 0

    ix, iy, ic = lax.axis_index("x"), lax.axis_index("y"), lax.axis_index("c")
    dev = 4 * ix + 2 * iy + ic
    c_idx = jnp.reshape(ic, (1,)).astype(jnp.int32)
    xs, target = x[0], loss_target[0]

    w_in_t, m_in_t, v_in_t = (jnp.swapaxes(a, 1, 2) for a in (w_in, m_w_in, v_w_in))
    shard_of = dict(w_in=w_in_t, w_out_a=w_out_a, w_out_b=w_out_b, w_out_c=w_out_c, w_out=w_out, w_ff1=w_ff1, w_ff2=w_ff2)
    ag_jobs = lambda l: {nm: GatherJob(w[l].astype(MXU)) for nm, w in shard_of.items()}
    layer0 = ag_jobs(0)
    conv_job = GatherJob(conv_w)
    run_jobs(list(layer0.values()) + [conv_job], "ag_layer0")
    gathered_w = [{nm: job.results[0] for nm, job in layer0.items()}]
    conv_full = jnp.transpose(conv_job.results[0], (1, 2, 0, 3)).reshape(depth, CONV_K, 3 * gw)

    per_dev = n_in // N_DEV

    def pack_in(g):
        w = g.reshape(n_in, d)
        return jnp.concatenate([w[:3 * gw], w[3 * gw + 2 * n_heads:], w[3 * gw:3 * gw + 2 * n_heads],
                                jnp.zeros((LANE - 2 * n_heads, d), w.dtype)], axis=0)

    def unpack_in(wp):
        w = jnp.concatenate([wp[:3 * gw], wp[off_ab:off_ab + 2 * n_heads], wp[3 * gw:off_ab]], axis=0)
        return w.reshape(N_DEV, per_dev, d)

    alog_p, dtb_p = _pad_lanes(a_log), _pad_lanes(dt_bias)
    bt_all = jnp.pad(jnp.transpose(b_spatial, (0, 2, 1)), ((0, 0), (0, 0), (0, LANE - n_groups)))

    saved = []
    cur = xs
    for l in range(depth):
        gl_w = gathered_w[l]
        lw = dict(w_in=pack_in(gl_w["w_in"]), conv=conv_full[l], alog=alog_p[l:l + 1], dtb=dtb_p[l:l + 1],
                  gng=gdn_norm_g[l:l + 1], lng=gmlp_ln_g[l:l + 1], ws=w_spatial[l], bt=bt_all[l],
                  qg=sba_q_g[l:l + 1], kg=sba_k_g[l:l + 1], gmix=norm_mix_g[l:l + 1], gmlp=norm_mlp_g[l:l + 1])
        nxt = ag_jobs(l + 1) if l + 1 < depth else {}
        ride = lambda *names: [nxt[nm] for nm in names if nm in nxt]
        s = dict(lw=lw, x=cur, gw=gl_w)
        s["h1"] = rms_fwd(cur, lw["gmix"], "rms_mix")
        z = s["z"] = matmul(s["h1"], lw["w_in"], "nt", F32, "mm_in", caps=(1024, 1408, 2048), jobs=ride("w_ff2"))
        s["pre"] = gdn_pre_fwd(z, off_ab // LANE, lw["conv"], lw["alog"], lw["dtb"], n_heads, "gdn_pre")
        s["chunks"] = gdn_chunk_fwd(*s["pre"], "gdn_chunk", jobs=ride("w_in"))
        s["o"], s["states"] = gdn_scan_fwd(s["chunks"], "gdn_scan")
        s["oa"] = gdn_post_fwd(s["o"], z, off_gate // HD, lw["gng"], "gdn_post")
        s["ob"] = gmlp_fwd(z, off_uv, width, lw["lng"], lw["ws"], lw["bt"], "gmlp")
        s["qkv_c"] = sba_pre_fwd(z, off_c // HD, n_heads, lw["qg"], lw["kg"], "sba_pre")
        s["oc"], s["tot"] = sba_fwd(*s["qkv_c"], "sba", jobs=ride("w_ff1"))
        s["ba"] = matmul(s["oa"], gl_w["w_out_a"], "nn", F32, "mm_oa", b_view="cols")
        s["bb"] = matmul(s["ob"], gl_w["w_out_b"], "nn", F32, "mm_ob", b_view="cols")
        s["bc"] = matmul(s["oc"], gl_w["w_out_c"], "nn", F32, "mm_oc", b_view="cols")
        s["y"] = merge_fwd(z, off_gl, s["ba"], s["bb"], s["bc"], "merge")
        s["x1"] = matmul(s["y"], gl_w["w_out"], "nn", F32, "mm_out", res=cur, b_view="rows")
        s["h2"] = rms_fwd(s["x1"], lw["gmlp"], "rms_mlp")
        s["f"], s["a"] = matmul(s["h2"], gl_w["w_ff1"], "nn", F32, "mm_ff1", b_view="cols", post="sqrelu",
                                jobs=ride("w_out", "w_out_a"))
        cur = matmul(s["a"], gl_w["w_ff2"], "nn", F32, "mm_ff2", res=s["x1"], b_view="rows",
                     jobs=ride("w_out_b", "w_out_c"))
        if nxt:
            gathered_w.append({nm: job.results[0] for nm, job in nxt.items()})
        saved.append(s)

    dx, dxb, loss_tile = loss_head(cur, target, "loss_head")
    loss = lax.psum(loss_tile[0, 0], AXES)

    big = dict(w_in=(w_in_t, m_in_t, v_in_t), w_out_a=(w_out_a, m_w_out_a, v_w_out_a), w_out_b=(w_out_b, m_w_out_b, v_w_out_b),
               w_out_c=(w_out_c, m_w_out_c, v_w_out_c), w_out=(w_out, m_w_out, v_w_out), w_ff1=(w_ff1, m_w_ff1, v_w_ff1),
               w_ff2=(w_ff2, m_w_ff2, v_w_ff2))
    bufs = {nm: tuple(lax.empty(w.shape, F32) for _ in range(4)) for nm, (w, _, _) in big.items()}
    pairing = []
    waiting = []

    def pair_reduce(nm, g8, l):
        w = big[nm][0]
        g8 = g8.reshape(N_DEV, w.shape[1], w.shape[2])
        pairing.append((nm, l, g8, PairExchangeJob(g8)))

    def pair_jobs():
        return [e[3] for e in pairing]

    def pair_done():
        for nm, l, g8, job in pairing:
            waiting.append((nm, l, ChipExchangeJob(pair_sum(g8, job.results[0], c_idx, "rs_pair_sum_" + nm))))
        pairing.clear()

    def take(*names):
        picked = [e for e in waiting if e[0] in names]
        for e in picked:
            waiting.remove(e)
        return picked

    def update(picked):
        for nm, l, job in picked:
            w, m, v = big[nm]
            bufs[nm] = tuple(adam_layer(job.results[0], w, m, v, bufs[nm], l, "adam_" + nm))

    small_grads = []
    for l in reversed(range(depth)):
        s = saved[l]
        lw, z, gl_w = s["lw"], s["z"], s["gw"]
        df = matmul(dxb, gl_w["w_ff2"], "nt", MXU, "mm_ff2_dx", b_view="rows", res=s["f"], post="sqrelu_bwd", jobs=pair_jobs())
        pair_done()
        pair_reduce("w_ff2", matmul(s["a"], dxb, "tn", MXU, "mm_ff2_dw"), l)
        dh2 = matmul(df, gl_w["w_ff1"], "nt", F32, "mm_ff1_dx", b_view="cols", jobs=pair_jobs())
        pair_done()
        pair_reduce("w_ff1", matmul(s["h2"], df, "tn", MXU, "mm_ff1_dw", out_slabs=w_ff1.shape[2]), l)
        dx1, dx1b, d_gmlp = rms_bwd(s["x1"], lw["gmlp"], dh2, dx, "rms_mlp_bwd")
        dy = matmul(dx1b, gl_w["w_out"], "nt", F32, "mm_out_dx", b_view="rows", jobs=pair_jobs())
        pair_done()
        pair_reduce("w_out", matmul(s["y"], dx1b, "tn", MXU, "mm_out_dw"), l)
        dgl, dba, dbb, dbc = merge_bwd(z, off_gl, s["ba"], s["bb"], s["bc"], dy, "merge_bwd")
        doa = matmul(dba, gl_w["w_out_a"], "nt", F32, "mm_oa_dx", b_view="cols", jobs=pair_jobs())
        pair_done()
        dob = matmul(dbb, gl_w["w_out_b"], "nt", F32, "mm_ob_dx", b_view="cols")
        doc = matmul(dbc, gl_w["w_out_c"], "nt", F32, "mm_oc_dx", b_view="cols")
        slab = w_out_a.shape[2]
        pair_reduce("w_out_a", matmul(s["oa"], dba, "tn", MXU, "mm_oa_dw", out_slabs=slab), l)
        pair_reduce("w_out_b", matmul(s["ob"], dbb, "tn", MXU, "mm_ob_dw", out_slabs=slab), l)
        pair_reduce("w_out_c", matmul(s["oc"], dbc, "tn", MXU, "mm_oc_dw", out_slabs=slab), l)
        riding = take("w_ff2", "w_ff1", "w_out")
        dqc, dkc, dvc = sba_bwd(*s["qkv_c"], s["tot"], doc, "sba_bwd", jobs=[e[2] for e in riding] + pair_jobs())
        update(riding)
        pair_done()
        dz_qc, dz_kc, dz_vc, d_qg, d_kg = sba_pre_bwd(z, off_c // HD, n_heads, lw["qg"], lw["kg"], dqc, dkc, dvc, "sba_pre_bwd")
        dz_uv, d_lng, d_ws, d_bt = gmlp_bwd(z, off_uv, width, lw["lng"], lw["ws"], lw["bt"], dob, "gmlp_bwd")
        do, dz_gate, d_gng = gdn_post_bwd(s["o"], z, off_gate // HD, lw["gng"], doa, "gdn_post_bwd")
        chunk_cts = gdn_scan_bwd(s["chunks"], s["states"], do, "gdn_scan_bwd")
        riding = take("w_in", "w_out_a", "w_out_b", "w_out_c")
        dqa, dka, dva, dga, dba_ = gdn_chunk_bwd(*s["pre"], chunk_cts, "gdn_chunk_bwd", jobs=[e[2] for e in riding])
        update(riding)
        dz_q, dz_k, dz_v, d_ab, d_cq, d_ck, d_cv, d_alog, d_dtb = gdn_pre_bwd(
            z, off_ab // LANE, lw["conv"], lw["alog"], lw["dtb"], n_heads, dqa, dka, dva, dga, dba_, "gdn_pre_bwd")
        dz = jnp.concatenate([dz_q, dz_k, dz_v, dz_gate, dz_uv, dz_qc, dz_kc, dz_vc, dgl[0], dgl[1], dgl[2],
                              d_ab.astype(MXU)], axis=1)
        dh1 = matmul(dz, lw["w_in"], "nn", F32, "mm_in_dx", caps=(1024, 1024, 1408))
        pair_reduce("w_in", unpack_in(matmul(dz, s["h1"], "tn", MXU, "mm_in_dw", caps=(1408, 1024, 2048))), l)
        dx, dxb, d_gmix = rms_bwd(s["x"], lw["gmix"], dh1, dx1, "rms_mix_bwd")
        small_grads.append(dict(
            conv_w=jnp.concatenate([d_cq, d_ck, d_cv], axis=1), a_log=d_alog[0, :n_heads], dt_bias=d_dtb[0, :n_heads],
            gdn_norm_g=d_gng[0], gmlp_ln_g=d_lng[0], w_spatial=d_ws, b_spatial=jnp.transpose(d_bt[:, :n_groups]),
            sba_q_g=d_qg[0], sba_k_g=d_kg[0], norm_mix_g=d_gmix[0], norm_mlp_g=d_gmlp[0]))
    small_grads = small_grads[::-1]
    run_jobs(pair_jobs(), "rs_pair_last")
    pair_done()
    rest = take("w_in")
    run_jobs([e[2] for e in rest], "rs_chip_last")
    update(rest)

    rep_names = ["a_log", "dt_bias", "gdn_norm_g", "gmlp_ln_g", "w_spatial", "b_spatial", "sba_q_g", "sba_k_g",
                 "norm_mix_g", "norm_mlp_g"]
    rep = dict(a_log=(a_log, m_a_log, v_a_log), dt_bias=(dt_bias, m_dt_bias, v_dt_bias),
               gdn_norm_g=(gdn_norm_g, m_gdn_norm_g, v_gdn_norm_g), gmlp_ln_g=(gmlp_ln_g, m_gmlp_ln_g, v_gmlp_ln_g),
               w_spatial=(w_spatial, m_w_spatial, v_w_spatial), b_spatial=(b_spatial, m_b_spatial, v_b_spatial),
               sba_q_g=(sba_q_g, m_sba_q_g, v_sba_q_g), sba_k_g=(sba_k_g, m_sba_k_g, v_sba_k_g),
               norm_mix_g=(norm_mix_g, m_norm_mix_g, v_norm_mix_g), norm_mlp_g=(norm_mlp_g, m_norm_mlp_g, v_norm_mlp_g))
    stack = lambda nm: jnp.stack([sg[nm] for sg in small_grads])
    conv_cols = conv_w.shape[2]
    conv_pad = jnp.zeros((depth, CONV_K, 3 * gw - conv_cols), F32)
    widen = lambda a: jnp.concatenate([a, conv_pad], axis=2)
    grads_packed = _pack([stack(nm) for nm in rep_names] + [stack("conv_w")])
    small_job = GatherJob(grads_packed)
    run_jobs([small_job], "ag_small_grads")
    gathered = small_job.results[0]
    packed = [_pack([rep[nm][k] for nm in rep_names] + [widen((conv_w, m_conv_w, v_conv_w)[k])]) for k in range(3)]
    shapes = [rep[nm][0].shape for nm in rep_names] + [(depth, CONV_K, 3 * gw)]
    flat = gathered.reshape(N_DEV, -1)
    n_rep = sum(int(rep[nm][0].size) for nm in rep_names)
    conv_part = flat[:, n_rep:n_rep + depth * CONV_K * 3 * gw].reshape(N_DEV, depth, CONV_K, 3 * gw)
    conv_mine = lax.dynamic_slice_in_dim(conv_part, dev * conv_cols, conv_cols, axis=3)
    conv_mine = jnp.concatenate([conv_mine, jnp.zeros((N_DEV, depth, CONV_K, 3 * gw - conv_cols), F32)], axis=3)
    tail = flat[:, n_rep + depth * CONV_K * 3 * gw:]
    parts = jnp.concatenate([flat[:, :n_rep], conv_mine.reshape(N_DEV, -1), tail], axis=1).reshape(gathered.shape)
    outs_small = small_adam(parts, packed[0], packed[1], packed[2], "adam_small")
    small = [dict(zip(rep_names + ["conv_w"], _unpack(o, shapes))) for o in outs_small]
    for sm in small:
        sm["conv_w"] = sm["conv_w"][:, :, :conv_cols]

    order = ["w_in", "conv_w", "a_log", "dt_bias", "gdn_norm_g", "gmlp_ln_g", "w_spatial", "b_spatial", "sba_q_g",
             "sba_k_g", "w_out_a", "w_out_b", "w_out_c", "w_out", "norm_mix_g", "norm_mlp_g", "w_ff1", "w_ff2"]
    result = [loss, dx[None]]
    for kind in range(4):
        for nm in order:
            if nm == "w_in":
                result.append(jnp.swapaxes(bufs[nm][kind], 1, 2))
            else:
                result.append(bufs[nm][kind] if nm in bufs else small[kind][nm])
    return tuple(result)
```
